```python
import math
import jax
import jax.numpy as jnp
from jax import lax
import numpy as np


D_MODEL = 4096
BATCH = 2
SEQ = 8192
DEPTH = 4

GRID_W = 64
CTX_LEN = 256
HEAD_DIM = 128
MIX_WIDTH = D_MODEL
A_WIDTH = MIX_WIDTH // 4
B_WIDTH = MIX_WIDTH // 4
C_WIDTH = MIX_WIDTH - A_WIDTH - B_WIDTH
A_GROUPS = A_WIDTH // HEAD_DIM
B_HEADS = B_WIDTH // HEAD_DIM
C_HEADS = C_WIDTH // HEAD_DIM
C_KV_HEADS = C_HEADS // 4
C_KV_WIDTH = C_KV_HEADS * HEAD_DIM
IN_WIDTH = 2 * A_WIDTH + 3 * B_WIDTH + C_WIDTH + 2 * C_KV_WIDTH
CHUNK = 128
NA_MAX_ROWS = 8
NA_COLS = 16
Q_BLOCK = 128
ROPE_THETA = 10000.0
ADA_RANK = 1024
N_MOD = 6
D_FF = 2 * D_MODEL
N_EXPERTS = 8
TOP_K = 2
D_FF_EXPERT = 3 * D_MODEL // 8
EPS = 1e-6

kernel_name = 'hybrid_parallel_heads_diffusion_trunk'


def rms_norm(x, g):
    xf = x.astype(jnp.float32)
    y = xf * lax.rsqrt(jnp.mean(xf * xf, axis=-1, keepdims=True) + EPS)
    return (y * g.astype(jnp.float32)).astype(x.dtype)


def modulate(h, shift, scale):
    return h * (1 + scale) + shift


def ada_mod(cond, w_down, w_up, b_up):
    return (jax.nn.silu(cond) @ w_down) @ w_up + b_up


def split_proj(p):
    sizes = [A_WIDTH, A_WIDTH, B_WIDTH, B_WIDTH, B_WIDTH, C_WIDTH, C_KV_WIDTH, C_KV_WIDTH]
    idx = np.cumsum(sizes)[:-1].tolist()
    return jnp.split(p, idx, axis=-1)


def to_heads(t, n_heads):
    return t.reshape(t.shape[0], t.shape[1], n_heads, HEAD_DIM)


def axial_rope(n_tokens):
    t = jnp.arange(n_tokens, dtype=jnp.int32)
    pos = jnp.stack([t // GRID_W, t % GRID_W], axis=-1).astype(jnp.float32)
    n_freq = HEAD_DIM // 4
    inv = 1.0 / (ROPE_THETA ** (jnp.arange(n_freq, dtype=jnp.float32) / n_freq))
    ang = pos[:, :, None] * inv
    return jnp.cos(ang)[:, None], jnp.sin(ang)[:, None]


def apply_rope(x, cos, sin):
    lead = x.shape[:-1]
    xf = x.astype(jnp.float32).reshape(*lead, 2, 2, HEAD_DIM // 4)
    a, b = xf[..., 0, :], xf[..., 1, :]
    out = jnp.stack([a * cos - b * sin, a * sin + b * cos], axis=-2)
    return out.reshape(x.shape).astype(x.dtype)


def chunk_token_mlp(u, v, w_s, b_s, g_v):
    bn, n, _ = v.shape
    u = jax.nn.gelu(u)
    v = rms_norm(jax.nn.gelu(v), g_v)
    vc = v.reshape(bn, n // CHUNK, CHUNK, A_GROUPS, HEAD_DIM)
    mixed = jnp.einsum('gpq,bnqgc->bnpgc', w_s, vc) + b_s.T[None, None, :, :, None]
    return u * mixed.reshape(bn, n, A_WIDTH)


def attend(q, k, v):
    bn, nq, h, dh = q.shape
    kvh = k.shape[2]
    qg = q.reshape(bn, nq, kvh, h // kvh, dh)
    s = jnp.einsum('bqkgd,bskd->bkgqs', qg, k).astype(jnp.float32) * (dh ** -0.5)
    p = jax.nn.softmax(s, axis=-1).astype(v.dtype)
    o = jnp.einsum('bkgqs,bskd->bqkgd', p, v)
    return o.reshape(bn, nq, h, dh)


def blocked_attention(q, k_all, v_all):
    bn, n, h, dh = q.shape
    qb = q.reshape(bn, n // Q_BLOCK, Q_BLOCK, h, dh).transpose(1, 0, 2, 3, 4)
    ob = lax.map(lambda qi: attend(qi, k_all, v_all), qb)
    return ob.transpose(1, 0, 2, 3, 4).reshape(bn, n, h, dh)


def neighbourhood_attention(q, k, v, k_ctx, v_ctx, rpb):
    bn, n, h, dh = q.shape
    rows = n // GRID_W
    win_h = min(NA_MAX_ROWS, rows)
    n_loc = win_h * NA_COLS
    qg = q.reshape(bn, rows, GRID_W, h, dh)
    kg = k.reshape(bn, rows, GRID_W, h, dh)
    vg = v.reshape(bn, rows, GRID_W, h, dh)
    cols = np.arange(GRID_W)
    col_start = np.clip(cols - NA_COLS // 2, 0, GRID_W - NA_COLS)
    col_idx_np = col_start[:, None] + np.arange(NA_COLS)[None, :]
    col_idx = jnp.asarray(col_idx_np, dtype=jnp.int32)
    col_bias_idx = jnp.asarray(col_idx_np - cols[:, None] + (NA_COLS - 1), dtype=jnp.int32)
    scale = dh ** -0.5

    def row_block(r):
        start = jnp.clip(r - win_h // 2, 0, rows - win_h)
        q_r = lax.dynamic_index_in_dim(qg, r, axis=1, keepdims=False)
        k_band = lax.dynamic_slice_in_dim(kg, start, win_h, axis=1)
        v_band = lax.dynamic_slice_in_dim(vg, start, win_h, axis=1)
        k_n = k_band[:, :, col_idx]
        v_n = v_band[:, :, col_idx]
        row_bias_idx = start + jnp.arange(win_h, dtype=jnp.int32) - r + (NA_MAX_ROWS - 1)
        bias = rpb[:, row_bias_idx][:, :, col_bias_idx]
        s_loc = (jnp.einsum('bqhd,biqjhd->bhqij', q_r, k_n).astype(jnp.float32) * scale
                 + bias.transpose(0, 2, 1, 3)[None].astype(jnp.float32))
        s_ctx = jnp.einsum('bqhd,bchd->bhqc', q_r, k_ctx).astype(jnp.float32) * scale
        s = jnp.concatenate([s_loc.reshape(bn, h, GRID_W, n_loc), s_ctx], axis=-1)
        p = jax.nn.softmax(s, axis=-1).astype(v.dtype)
        p_loc = p[..., :n_loc].reshape(bn, h, GRID_W, win_h, NA_COLS)
        return (jnp.einsum('bhqij,biqjhd->bqhd', p_loc, v_n)
                + jnp.einsum('bhqc,bchd->bqhd', p[..., n_loc:], v_ctx))

    out = lax.map(row_block, jnp.arange(rows, dtype=jnp.int32))
    return out.transpose(1, 0, 2, 3, 4).reshape(bn, n, h * dh)


def merge_groups(y_a, y_b, y_c, g):
    g_a, g_b, g_c = jnp.split(g, [A_WIDTH, A_WIDTH + B_WIDTH])
    return jnp.concatenate([rms_norm(y_a, g_a), rms_norm(y_b, g_b), rms_norm(y_c, g_c)], axis=-1)


def token_mixers(h, hc, w_in, sgu_g, sgu_w, sgu_b, rpb, qk_g, grp_g, w_out, cos, sin, need_ctx):
    bn, n, _ = h.shape
    bc, m, _ = hc.shape
    au, av, bq, bk, bv, cq, ck, cv = split_proj(h @ w_in)
    au_c, av_c, bq_c, bk_c, bv_c, cq_c, ck_c, cv_c = split_proj(hc @ w_in)
    y_a = chunk_token_mlp(au, av, sgu_w, sgu_b, sgu_g)
    kb_c, vb_c = to_heads(bk_c, B_HEADS), to_heads(bv_c, B_HEADS)
    y_b = neighbourhood_attention(to_heads(bq, B_HEADS), to_heads(bk, B_HEADS), to_heads(bv, B_HEADS),
                                  kb_c, vb_c, rpb)
    q_c = apply_rope(rms_norm(to_heads(cq, C_HEADS), qk_g[0]), cos, sin)
    k_c = apply_rope(rms_norm(to_heads(ck, C_KV_HEADS), qk_g[1]), cos, sin)
    kc_ctx = rms_norm(to_heads(ck_c, C_KV_HEADS), qk_g[1])
    vc_ctx = to_heads(cv_c, C_KV_HEADS)
    y_c = blocked_attention(q_c, jnp.concatenate([k_c, kc_ctx], axis=1),
                            jnp.concatenate([to_heads(cv, C_KV_HEADS), vc_ctx], axis=1)).reshape(bn, n, C_WIDTH)
    y = merge_groups(y_a, y_b, y_c, grp_g) @ w_out
    if not need_ctx:
        return y, None
    z_a = chunk_token_mlp(au_c, av_c, sgu_w, sgu_b, sgu_g)
    z_b = attend(to_heads(bq_c, B_HEADS), kb_c, vb_c).reshape(bc, m, B_WIDTH)
    z_c = attend(rms_norm(to_heads(cq_c, C_HEADS), qk_g[0]), kc_ctx, vc_ctx).reshape(bc, m, C_WIDTH)
    z = merge_groups(z_a, z_b, z_c, grp_g) @ w_out
    return y, z


def swiglu(h, w1, w3, w2):
    return (jax.nn.silu(h @ w1) * (h @ w3)) @ w2


def moe_swiglu(h, w_router, b_router, w1, w3, w2):
    logits = (h @ w_router).astype(jnp.float32) + b_router.astype(jnp.float32)
    top_val, top_idx = lax.top_k(logits, TOP_K)
    gates = jax.nn.softmax(top_val, axis=-1)
    dense_gate = jnp.sum(jax.nn.one_hot(top_idx, N_EXPERTS, dtype=jnp.float32) * gates[..., None], axis=-2)
    out = jnp.zeros_like(h)
    for e in range(N_EXPERTS):
        out = out + dense_gate[..., e:e + 1].astype(h.dtype) * swiglu(h, w1[e], w3[e], w2[e])
    return out


def setup_inputs(seed: int = 0) -> dict:
    key = jax.random.key(seed)
    ks = jax.random.split(key, 26)
    f32 = jnp.float32
    n_dense = (DEPTH + 1) // 2
    n_moe = DEPTH // 2

    def nrm(k, shape, scale):
        return jax.random.normal(k, shape, f32) * scale

    return {
        'x': nrm(ks[0], (BATCH, SEQ, D_MODEL), 1.0),
        'c': nrm(ks[1], (BATCH, D_MODEL), 1.0),
        'ctx': nrm(ks[2], (BATCH, CTX_LEN, D_MODEL), 1.0),
        'c_ctx': nrm(ks[3], (D_MODEL,), 1.0),
        'ada_down': nrm(ks[4], (DEPTH, D_MODEL, ADA_RANK), D_MODEL ** -0.5),
        'ada_up': nrm(ks[5], (DEPTH, ADA_RANK, N_MOD * D_MODEL), 0.5 * ADA_RANK ** -0.5),
        'ada_bias': nrm(ks[6], (DEPTH, N_MOD * D_MODEL), 0.01),
        'norm1_g': 1.0 + nrm(ks[7], (DEPTH, D_MODEL), 0.02),
        'norm2_g': 1.0 + nrm(ks[8], (DEPTH, D_MODEL), 0.02),
        'w_in': nrm(ks[9], (DEPTH, D_MODEL, IN_WIDTH), D_MODEL ** -0.5),
        'sgu_norm_g': 1.0 + nrm(ks[10], (DEPTH, A_WIDTH), 0.02),
        'sgu_w': nrm(ks[11], (DEPTH, A_GROUPS, CHUNK, CHUNK), CHUNK ** -0.5),
        'sgu_b': 1.0 + nrm(ks[12], (DEPTH, A_GROUPS, CHUNK), 0.02),
        'na_rpb': nrm(ks[13], (DEPTH, B_HEADS, 2 * NA_MAX_ROWS - 1, 2 * NA_COLS - 1), 0.1),
        'qk_norm_g': 1.0 + nrm(ks[14], (DEPTH, 2, HEAD_DIM), 0.02),
        'group_norm_g': 1.0 + nrm(ks[15], (DEPTH, MIX_WIDTH), 0.02),
        'w_out': nrm(ks[16], (DEPTH, MIX_WIDTH, D_MODEL), MIX_WIDTH ** -0.5),
        'ffn_w1': nrm(ks[17], (n_dense, D_MODEL, D_FF), D_MODEL ** -0.5),
        'ffn_w3': nrm(ks[18], (n_dense, D_MODEL, D_FF), D_MODEL ** -0.5),
        'ffn_w2': nrm(ks[19], (n_dense, D_FF, D_MODEL), D_FF ** -0.5),
        'moe_router': nrm(ks[20], (n_moe, D_MODEL, N_EXPERTS), D_MODEL ** -0.5),
        'moe_router_b': nrm(ks[21], (n_moe, N_EXPERTS), 0.01),
        'moe_w1': nrm(ks[22], (n_moe, N_EXPERTS, D_MODEL, D_FF_EXPERT), D_MODEL ** -0.5),
        'moe_w3': nrm(ks[23], (n_moe, N_EXPERTS, D_MODEL, D_FF_EXPERT), D_MODEL ** -0.5),
        'moe_w2': nrm(ks[24], (n_moe, N_EXPERTS, D_FF_EXPERT, D_MODEL), D_FF_EXPERT ** -0.5),
        'final_norm_g': 1.0 + nrm(ks[25], (D_MODEL,), 0.02),
    }


def reference(x, c, ctx, c_ctx, ada_down, ada_up, ada_bias, norm1_g, norm2_g, w_in, sgu_norm_g, sgu_w,
              sgu_b, na_rpb, qk_norm_g, group_norm_g, w_out, ffn_w1, ffn_w3, ffn_w2, moe_router,
              moe_router_b, moe_w1, moe_w3, moe_w2, final_norm_g):
    cos, sin = axial_rope(x.shape[1])
    xc = ctx
    for l in range(DEPTH):
        need_ctx = l < DEPTH - 1
        mod = ada_mod(c, ada_down[l], ada_up[l], ada_bias[l])[:, None, :]
        mod_c = ada_mod(c_ctx, ada_down[l], ada_up[l], ada_bias[l])[None, None, :]
        sh1, sc1, g1, sh2, sc2, g2 = jnp.split(mod, N_MOD, axis=-1)
        csh1, csc1, cg1, csh2, csc2, cg2 = jnp.split(mod_c, N_MOD, axis=-1)
        h = modulate(rms_norm(x, norm1_g[l]), sh1, sc1)
        hc = modulate(rms_norm(xc, norm1_g[l]), csh1, csc1)
        y, yc = token_mixers(h, hc, w_in[l], sgu_norm_g[l], sgu_w[l], sgu_b[l], na_rpb[l], qk_norm_g[l],
                             group_norm_g[l], w_out[l], cos, sin, need_ctx)
        x = x + g1 * y
        h2 = modulate(rms_norm(x, norm2_g[l]), sh2, sc2)
        if need_ctx:
            xc = xc + cg1 * yc
            h2c = modulate(rms_norm(xc, norm2_g[l]), csh2, csc2)
        j = l // 2
        if l % 2 == 0:
            x = x + g2 * swiglu(h2, ffn_w1[j], ffn_w3[j], ffn_w2[j])
            if need_ctx:
                xc = xc + cg2 * swiglu(h2c, ffn_w1[j], ffn_w3[j], ffn_w2[j])
        else:
            x = x + g2 * moe_swiglu(h2, moe_router[j], moe_router_b[j], moe_w1[j], moe_w3[j], moe_w2[j])
            if need_ctx:
                xc = xc + cg2 * moe_swiglu(h2c, moe_router[j], moe_router_b[j], moe_w1[j], moe_w3[j], moe_w2[j])
    return rms_norm(x, final_norm_g)
```

```python
import functools
import math

import numpy as np
import jax
import jax.numpy as jnp
from jax import lax
from jax.experimental import pallas as pl
from jax.experimental.pallas import tpu as pltpu

F32 = jnp.float32
BF16 = jnp.bfloat16

HEAD_DIM = 128
GRID_W = 64
CHUNK = 128
NA_ROWS = 8
NA_COLS = 16
ROPE_THETA = 10000.0
EPS = 1e-6
N_MOD = 6
MASK_VALUE = -1e30
ROUTER_LANES = 128
NA_Q_ROWS = 8
NA_BAND_ROWS = 16
V7X_VMEM_BYTES = 64 * 1024 * 1024
VMEM_LIMIT = V7X_VMEM_BYTES - 8 * 1024 * 1024


def _params(*sem):
    return pltpu.CompilerParams(dimension_semantics=sem, vmem_limit_bytes=VMEM_LIMIT)


def _rms(x, g):
    return x * lax.rsqrt(jnp.mean(x * x, axis=-1, keepdims=True) + EPS) * g


def _ada_down_kernel(c_ref, w_ref, o_ref):
    c = c_ref[...]
    a = (c * jax.nn.sigmoid(c)).astype(BF16)
    o_ref[0] = jnp.dot(a, w_ref[0].astype(BF16), preferred_element_type=F32)


def _ada_up_kernel(t_ref, w_ref, b_ref, o_ref):
    t = t_ref[0].astype(BF16)
    o_ref[0] = jnp.dot(t, w_ref[0].astype(BF16), preferred_element_type=F32) + b_ref[0]


def _ada_mods(cond, w_down, w_up, b_up):
    depth, d, r = w_down.shape
    n_out = w_up.shape[2]
    rows = cond.shape[0]
    tn1 = min(r, 512)
    t = pl.pallas_call(
        _ada_down_kernel,
        grid=(depth, r // tn1),
        in_specs=[pl.BlockSpec((rows, d), lambda l, j: (0, 0)),
                  pl.BlockSpec((1, d, tn1), lambda l, j: (l, 0, j))],
        out_specs=pl.BlockSpec((1, rows, tn1), lambda l, j: (l, 0, j)),
        out_shape=jax.ShapeDtypeStruct((depth, rows, r), F32),
        compiler_params=_params("parallel", "parallel"),
        name="ada_down",
    )(cond, w_down)
    tn2 = min(n_out, 2048)
    return pl.pallas_call(
        _ada_up_kernel,
        grid=(depth, n_out // tn2),
        in_specs=[pl.BlockSpec((1, rows, r), lambda l, j: (l, 0, 0)),
                  pl.BlockSpec((1, r, tn2), lambda l, j: (l, 0, j)),
                  pl.BlockSpec((1, 1, tn2), lambda l, j: (l, 0, j))],
        out_specs=pl.BlockSpec((1, rows, tn2), lambda l, j: (l, 0, j)),
        out_shape=jax.ShapeDtypeStruct((depth, rows, n_out), F32),
        compiler_params=_params("parallel", "parallel"),
        name="ada_up",
    )(t, w_up, b_up.reshape(depth, 1, n_out))


def _norm_mod_kernel(x_ref, g_ref, sh_ref, sc_ref, o_ref):
    y = _rms(x_ref[...], g_ref[...])
    o_ref[...] = (y * (1.0 + sc_ref[0]) + sh_ref[0]).astype(o_ref.dtype)


def _norm_kernel(x_ref, g_ref, o_ref):
    o_ref[...] = _rms(x_ref[...], g_ref[...]).astype(o_ref.dtype)


def _seg_fn(tm, n_lat, n_batch):
    return lambda i: jnp.minimum((i * tm) // n_lat, n_batch)


def _norm_mod(x, g, mods, k_shift, k_scale, *, n_lat, n_batch, tm=256):
    t, d = x.shape
    seg = _seg_fn(tm, n_lat, n_batch)
    return pl.pallas_call(
        _norm_mod_kernel,
        grid=(t // tm,),
        in_specs=[pl.BlockSpec((tm, d), lambda i: (i, 0)),
                  pl.BlockSpec((1, d), lambda i: (0, 0)),
                  pl.BlockSpec((1, 1, d), lambda i: (seg(i) * N_MOD + k_shift, 0, 0)),
                  pl.BlockSpec((1, 1, d), lambda i: (seg(i) * N_MOD + k_scale, 0, 0))],
        out_specs=pl.BlockSpec((tm, d), lambda i: (i, 0)),
        out_shape=jax.ShapeDtypeStruct((t, d), BF16),
        compiler_params=_params("parallel"),
        name="norm_mod",
    )(x, g.reshape(1, d), mods, mods)


def _final_norm(x, g, rows, tm=256):
    d = x.shape[1]
    return pl.pallas_call(
        _norm_kernel,
        grid=(rows // tm,),
        in_specs=[pl.BlockSpec((tm, d), lambda i: (i, 0)),
                  pl.BlockSpec((1, d), lambda i: (0, 0))],
        out_specs=pl.BlockSpec((tm, d), lambda i: (i, 0)),
        out_shape=jax.ShapeDtypeStruct((rows, d), F32),
        compiler_params=_params("parallel"),
        name="final_norm",
    )(x, g.reshape(1, d))


def _mm_kernel(a_ref, b_ref, o_ref):
    o_ref[...] = jnp.dot(a_ref[...], b_ref[...],
                         preferred_element_type=F32).astype(o_ref.dtype)


def _mm_swiglu_kernel(a_ref, w1_ref, w3_ref, o_ref):
    a = a_ref[...]
    h1 = jnp.dot(a, w1_ref[...], preferred_element_type=F32)
    h3 = jnp.dot(a, w3_ref[...], preferred_element_type=F32)
    o_ref[...] = (h1 * jax.nn.sigmoid(h1) * h3).astype(o_ref.dtype)


def _mm_res_kernel(a_ref, b_ref, x_ref, g_ref, o_ref):
    acc = jnp.dot(a_ref[...], b_ref[...], preferred_element_type=F32)
    o_ref[...] = x_ref[...] + g_ref[0] * acc


def _mm_res_gate_kernel(a_ref, b_ref, x_ref, g_ref, rg_ref, o_ref, *, col):
    acc = jnp.dot(a_ref[...], b_ref[...], preferred_element_type=F32)
    o_ref[...] = x_ref[...] + (g_ref[0] * rg_ref[:, col:col + 1]) * acc


def _w_spec(w, tn, expert):
    k = w.shape[-2]
    if expert is None:
        return pl.BlockSpec((k, tn), lambda i, j: (0, j))
    return pl.BlockSpec((None, k, tn), lambda i, j: (expert, 0, j))


def _matmul(a, w, *, out_dtype, tm=512, tn=1024, expert=None):
    m, k = a.shape
    n = w.shape[-1]
    tn = min(tn, n)
    return pl.pallas_call(
        _mm_kernel,
        grid=(m // tm, n // tn),
        in_specs=[pl.BlockSpec((tm, k), lambda i, j: (i, 0)), _w_spec(w, tn, expert)],
        out_specs=pl.BlockSpec((tm, tn), lambda i, j: (i, j)),
        out_shape=jax.ShapeDtypeStruct((m, n), out_dtype),
        compiler_params=_params("parallel", "arbitrary"),
        name="matmul",
    )(a, w)


def _matmul_swiglu(a, w1, w3, *, tm=512, tn=512, expert=None):
    m, k = a.shape
    n = w1.shape[-1]
    tn = min(tn, n)
    return pl.pallas_call(
        _mm_swiglu_kernel,
        grid=(m // tm, n // tn),
        in_specs=[pl.BlockSpec((tm, k), lambda i, j: (i, 0)),
                  _w_spec(w1, tn, expert), _w_spec(w3, tn, expert)],
        out_specs=pl.BlockSpec((tm, tn), lambda i, j: (i, j)),
        out_shape=jax.ShapeDtypeStruct((m, n), BF16),
        compiler_params=_params("parallel", "arbitrary"),
        name="matmul_swiglu",
    )(a, w1, w3)


def _matmul_residual(a, w, x, mods, k_gate, *, n_lat, n_batch, tm=512, tn=512,
                     expert=None, row_gate=None):
    m, k = a.shape
    n = w.shape[-1]
    tn = min(tn, n)
    seg = _seg_fn(tm, n_lat, n_batch)
    in_specs = [pl.BlockSpec((tm, k), lambda i, j: (i, 0)), _w_spec(w, tn, expert),
                pl.BlockSpec((tm, tn), lambda i, j: (i, j)),
                pl.BlockSpec((1, 1, tn), lambda i, j: (seg(i) * N_MOD + k_gate, 0, j))]
    args = [a, w, x, mods]
    if row_gate is None:
        body = _mm_res_kernel
    else:
        body = functools.partial(_mm_res_gate_kernel, col=expert)
        in_specs.append(pl.BlockSpec((tm, ROUTER_LANES), lambda i, j: (i, 0)))
        args.append(row_gate)
    return pl.pallas_call(
        body,
        grid=(m // tm, n // tn),
        in_specs=in_specs,
        out_specs=pl.BlockSpec((tm, tn), lambda i, j: (i, j)),
        out_shape=jax.ShapeDtypeStruct((m, n), F32),
        compiler_params=_params("parallel", "arbitrary"),
        name="matmul_residual",
    )(*args)


def _sgu_kernel(u_ref, v_ref, gv_ref, ws_ref, bs_ref, ga_ref, o_ref, *, n_chunks, n_groups):
    for c in range(n_chunks):
        rows = slice(c * CHUNK, (c + 1) * CHUNK)
        u = jax.nn.gelu(u_ref[rows, :].astype(F32), approximate=True)
        v = jax.nn.gelu(v_ref[rows, :].astype(F32), approximate=True)
        vb = _rms(v, gv_ref[...]).astype(BF16)
        parts = [jnp.dot(ws_ref[g], vb[:, g * HEAD_DIM:(g + 1) * HEAD_DIM],
                         preferred_element_type=F32) for g in range(n_groups)]
        y = u * (jnp.concatenate(parts, axis=1) + bs_ref[...])
        o_ref[rows, :] = _rms(y, ga_ref[...]).astype(o_ref.dtype)


def _sgu(p, g_v, w_s, b_s, g_a, *, width, ta=256):
    t = p.shape[0]
    n_groups = width // HEAD_DIM
    bias =jnp.repeat(b_s.T, HEAD_DIM, axis=1)
    return pl.pallas_call(
        functools.partial(_sgu_kernel, n_chunks=ta // CHUNK, n_groups=n_groups),
        grid=(t // ta,),
        in_specs=[pl.BlockSpec((ta, width), lambda i: (i, 0)),
                  pl.BlockSpec((ta, width), lambda i: (i, 1)),
                  pl.BlockSpec((1, width), lambda i: (0, 0)),
                  pl.BlockSpec((n_groups, CHUNK, CHUNK), lambda i: (0, 0, 0)),
                  pl.BlockSpec((CHUNK, width), lambda i: (0, 0)),
                  pl.BlockSpec((1, width), lambda i: (0, 0))],
        out_specs=pl.BlockSpec((ta, width), lambda i: (i, 0)),
        out_shape=jax.ShapeDtypeStruct((t, width), BF16),
        compiler_params=_params("parallel"),
        name="sgu",
    )(p, p, g_v.reshape(1, width), w_s.astype(BF16), bias, g_a.reshape(1, width))


def _na_bias_tables(rpb, rows):
    n_blocks = rows // NA_Q_ROWS
    cols = np.arange(GRID_W)
    col_start = np.clip(cols - NA_COLS // 2, 0, GRID_W - NA_COLS)
    dc = cols[None, :] - cols[:, None]
    col_ok = (cols[None, :] >= col_start[:, None]) & (cols[None, :] < col_start[:, None] + NA_COLS)
    col_idx = np.where(col_ok, dc + NA_COLS - 1, 0)
    n_dr = 2 * NA_ROWS - 1
    t = jnp.where(col_ok[None, None], rpb[:, :, col_idx], MASK_VALUE)
    t = jnp.concatenate([t, jnp.full((rpb.shape[0], 1, GRID_W, GRID_W), MASK_VALUE, F32)], axis=1)
    slab = np.full((3, NA_Q_ROWS, NA_BAND_ROWS), n_dr, np.int32)
    rep = [0, min(1, n_blocks - 1), n_blocks - 1]
    for ty, blk in enumerate(rep):
        band0 = int(np.clip(blk * NA_Q_ROWS - NA_ROWS // 2, 0, rows - NA_BAND_ROWS))
        for rq in range(NA_Q_ROWS):
            r = blk * NA_Q_ROWS + rq
            start = int(np.clip(r - NA_ROWS // 2, 0, rows - NA_ROWS))
            for kr in range(NA_BAND_ROWS):
                ar = band0 + kr
                if start <= ar < start + NA_ROWS:
                    slab[ty, rq, kr] = ar - r + NA_ROWS - 1
    full = t[:, slab]
    full = full.transpose(0, 1, 2, 4, 3, 5)
    return full.reshape(rpb.shape[0], 3, NA_Q_ROWS * GRID_W, NA_BAND_ROWS * GRID_W)


def _na_kernel(q_ref, k_ref, v_ref, kc_ref, vc_ref, bias_ref, o_ref, *, rows, scale):
    i = pl.program_id(2)
    band0 = jnp.clip(i * NA_Q_ROWS - NA_ROWS // 2, 0, rows - NA_BAND_ROWS)
    start = pl.multiple_of(band0 * GRID_W, GRID_W)
    nb = NA_BAND_ROWS * GRID_W
    q = q_ref[...]
    kb = k_ref[pl.ds(start, nb), :]
    vb = v_ref[pl.ds(start, nb), :]
    dn = (((1,), (1,)), ((), ()))
    s_loc = lax.dot_general(q, kb, dn, preferred_element_type=F32) * scale + bias_ref[...]
    s_ctx = lax.dot_general(q, kc_ref[...], dn, preferred_element_type=F32) * scale
    m = jnp.maximum(jnp.max(s_loc, axis=1, keepdims=True), jnp.max(s_ctx, axis=1, keepdims=True))
    p_loc = jnp.exp(s_loc - m)
    p_ctx = jnp.exp(s_ctx - m)
    l = jnp.sum(p_loc, axis=1, keepdims=True) + jnp.sum(p_ctx, axis=1, keepdims=True)
    o = (jnp.dot(p_loc.astype(BF16), vb, preferred_element_type=F32)
         + jnp.dot(p_ctx.astype(BF16), vc_ref[...], preferred_element_type=F32))
    o_ref[...] = (o / l).astype(o_ref.dtype)


def _neighbourhood_attention(p, bias, *, n_batch, n_lat, n_ctx, q_col, k_col, v_col, n_heads):
    rows = n_lat // GRID_W
    tq = NA_Q_ROWS * GRID_W
    nbk = NA_BAND_ROWS * GRID_W
    n_blocks = n_lat // tq
    ctx_blk0 = (n_batch * n_lat) // n_ctx

    def btype(i):
        return jnp.where(i == 0, 0, jnp.where(i == n_blocks - 1, 2, 1))

    return pl.pallas_call(
        functools.partial(_na_kernel, rows=rows, scale=HEAD_DIM ** -0.5),
        grid=(n_heads, n_batch, n_blocks),
        in_specs=[pl.BlockSpec((tq, HEAD_DIM), lambda h, b, i: (b * n_blocks + i, q_col + h)),
                  pl.BlockSpec((n_lat, HEAD_DIM), lambda h, b, i: (b, k_col + h)),
                  pl.BlockSpec((n_lat, HEAD_DIM), lambda h, b, i: (b, v_col + h)),
                  pl.BlockSpec((n_ctx, HEAD_DIM), lambda h, b, i: (ctx_blk0 + b, k_col + h)),
                  pl.BlockSpec((n_ctx, HEAD_DIM), lambda h, b, i: (ctx_blk0 + b, v_col + h)),
                  pl.BlockSpec((None, None, tq, nbk), lambda h, b, i: (h, btype(i), 0, 0))],
        out_specs=pl.BlockSpec((tq, HEAD_DIM), lambda h, b, i: (b * n_blocks + i, h)),
        out_shape=jax.ShapeDtypeStruct((n_batch * n_lat, n_heads * HEAD_DIM), BF16),
        compiler_params=_params("parallel", "parallel", "arbitrary"),
        name="neighbourhood_attention",
    )(p, p, p, p, p, bias)


def _rope_tables(n_lat, n_ctx):
    t = jnp.arange(n_lat, dtype=jnp.int32)
    pos = jnp.stack([t // GRID_W, t % GRID_W], axis=-1).astype(F32)
    n_freq = HEAD_DIM // 4
    inv = 1.0 / (ROPE_THETA ** (jnp.arange(n_freq, dtype=F32) / n_freq))
    ang = pos[:, :, None] * inv
    cos, sin = jnp.cos(ang), jnp.sin(ang)
    cos_t = jnp.concatenate([cos[:, 0], cos[:, 0], cos[:, 1], cos[:, 1]], axis=-1)
    sin_t = jnp.concatenate([-sin[:, 0], sin[:, 0], -sin[:, 1], sin[:, 1]], axis=-1)
    cos_t = jnp.concatenate([cos_t, jnp.ones((n_ctx, HEAD_DIM), F32)], axis=0)
    sin_t = jnp.concatenate([sin_t, jnp.zeros((n_ctx, HEAD_DIM), F32)], axis=0)
    return cos_t, sin_t


def _qk_prep_kernel(q_ref, k_ref, v_ref, cos_ref, sin_ref, gq_ref, gk_ref,
                    qo_ref, ko_ref, vo_ref, *, n_q, n_kv):
    cos = cos_ref[...]
    sin = sin_ref[...]
    quarter = HEAD_DIM // 4
    lane = lax.broadcasted_iota(jnp.int32, cos.shape, 1)
    first_half = (lane % (2 * quarter)) < quarter

    def prep(x, g):
        y = _rms(x.astype(F32), g)
        swapped = jnp.where(first_half,
                            pltpu.roll(y, HEAD_DIM - quarter, 1),
                            pltpu.roll(y, quarter, 1))
        return (y * cos + swapped * sin).astype(BF16)

    for h in range(n_q):
        cs = slice(h * HEAD_DIM, (h + 1) * HEAD_DIM)
        qo_ref[:, cs] = prep(q_ref[:, cs], gq_ref[...])
    for h in range(n_kv):
        cs = slice(h * HEAD_DIM, (h + 1) * HEAD_DIM)
        ko_ref[0, :, cs] = prep(k_ref[:, cs], gk_ref[...])
    vo_ref[0] = v_ref[...]


def _qk_prep(p, cos_t, sin_t, g_q, g_k, *, n_batch, n_lat, n_ctx, q_off, n_q, k_off, v_off, n_kv):
    t = p.shape[0]
    tm = n_ctx
    lat_tiles = n_lat // tm
    n_lat_tiles = n_batch * lat_tiles

    def pos_blk(i):
        return jnp.where(i < n_lat_tiles, i % lat_tiles, lat_tiles)

    def kv_blk(i):
        c = i - n_lat_tiles
        return (jnp.where(i < n_lat_tiles, i // lat_tiles, c),
                jnp.where(i < n_lat_tiles, i % lat_tiles, lat_tiles), 0)

    wq, wkv = n_q * HEAD_DIM, n_kv * HEAD_DIM
    return pl.pallas_call(
        functools.partial(_qk_prep_kernel, n_q=n_q, n_kv=n_kv),
        grid=(t // tm,),
        in_specs=[pl.BlockSpec((tm, wq), lambda i: (i, q_off)),
                  pl.BlockSpec((tm, wkv), lambda i: (i, k_off)),
                  pl.BlockSpec((tm, wkv), lambda i: (i, v_off)),
                  pl.BlockSpec((tm, HEAD_DIM), lambda i: (pos_blk(i), 0)),
                  pl.BlockSpec((tm, HEAD_DIM), lambda i: (pos_blk(i), 0)),
                  pl.BlockSpec((1, HEAD_DIM), lambda i: (0, 0)),
                  pl.BlockSpec((1, HEAD_DIM), lambda i: (0, 0))],
        out_specs=[pl.BlockSpec((tm, wq), lambda i: (i, 0)),
                   pl.BlockSpec((1, tm, wkv), kv_blk),
                   pl.BlockSpec((1, tm, wkv), kv_blk)],
        out_shape=[jax.ShapeDtypeStruct((t, wq), BF16),
                   jax.ShapeDtypeStruct((n_batch, n_lat + n_ctx, wkv), BF16),
                   jax.ShapeDtypeStruct((n_batch, n_lat + n_ctx, wkv), BF16)],
        compiler_params=_params("parallel"),
        name="qk_prep",
    )(p, p, p, cos_t, sin_t, g_q.reshape(1, HEAD_DIM), g_k.reshape(1, HEAD_DIM))


def _flash_kernel(q_ref, k_ref, v_ref, o_ref, m_sc, l_sc, acc_sc, *, group, tq, scale):
    kj = pl.program_id(3)

    @pl.when(kj == 0)
    def _():
        m_sc[...] = jnp.full_like(m_sc, -jnp.inf)
        l_sc[...] = jnp.zeros_like(l_sc)
        acc_sc[...] = jnp.zeros_like(acc_sc)

    if group == 1:
        q = q_ref[...]
    else:
        q = jnp.concatenate([q_ref[:, g * HEAD_DIM:(g + 1) * HEAD_DIM] for g in range(group)], axis=0)
    s = lax.dot_general(q, k_ref[0], (((1,), (1,)), ((), ())), preferred_element_type=F32) * scale
    m_prev = m_sc[...]
    m_new = jnp.maximum(m_prev, jnp.max(s, axis=1, keepdims=True))
    alpha = jnp.exp(m_prev - m_new)
    p = jnp.exp(s - m_new)
    l_sc[...] = alpha * l_sc[...] + jnp.sum(p, axis=1, keepdims=True)
    acc_sc[...] = alpha * acc_sc[...] + jnp.dot(p.astype(BF16), v_ref[0], preferred_element_type=F32)
    m_sc[...] = m_new

    @pl.when(kj == pl.num_programs(3) - 1)
    def _():
        o = acc_sc[...] / l_sc[...]
        for g in range(group):
            o_ref[:, g * HEAD_DIM:(g + 1) * HEAD_DIM] = o[g * tq:(g + 1) * tq].astype(o_ref.dtype)


def _pick_tile(n, candidates):
    for c in candidates:
        if n % c == 0:
            return c
    raise ValueError(f"no tile for {n}")


def _flash(q2d, k3d, v3d, *, n_batch, n_q_rows, q_row0, q_rows_per_batch, q_col, n_kv_heads,
           group, kv_row0, n_kv_rows, tq=256):
    tq = min(tq, n_q_rows)
    tk = _pick_tile(n_kv_rows, (1024, 768, 512, 256))
    qb0, qbb = q_row0 // tq, q_rows_per_batch // tq
    kb0 = kv_row0 // tk
    gw = group * HEAD_DIM
    return pl.pallas_call(
        functools.partial(_flash_kernel, group=group, tq=tq, scale=HEAD_DIM ** -0.5),
        grid=(n_batch, n_kv_heads, n_q_rows // tq, n_kv_rows // tk),
        in_specs=[pl.BlockSpec((tq, gw), lambda b, h, i, j: (qb0 + b * qbb + i, q_col + h)),
                  pl.BlockSpec((1, tk, HEAD_DIM), lambda b, h, i, j: (b, kb0 + j, h)),
                  pl.BlockSpec((1, tk, HEAD_DIM), lambda b, h, i, j: (b, kb0 + j, h))],
        out_specs=pl.BlockSpec((tq, gw), lambda b, h, i, j: (b * (n_q_rows // tq) + i, h)),
        out_shape=jax.ShapeDtypeStruct((n_batch * n_q_rows, n_kv_heads * gw), BF16),
        scratch_shapes=[pltpu.VMEM((group * tq, 1), F32),
                        pltpu.VMEM((group * tq, 1), F32),
                        pltpu.VMEM((group * tq, HEAD_DIM), F32)],
        compiler_params=_params("parallel", "parallel", "parallel", "arbitrary"),
        name="flash_attention",
    )(q2d, k3d, v3d)


def _merge_kernel(ya_ref, yb_ref, yc_ref, gb_ref, gc_ref, o_ref, *, wa, wb):
    o_ref[:, :wa] = ya_ref[...]
    o_ref[:, wa:wa + wb] = _rms(yb_ref[...].astype(F32), gb_ref[...]).astype(o_ref.dtype)
    o_ref[:, wa + wb:] = _rms(yc_ref[...].astype(F32), gc_ref[...]).astype(o_ref.dtype)


def _merge(ya, yb, yc, g_b, g_c, tm=256):
    t, wa = ya.shape
    wb, wc = yb.shape[1], yc.shape[1]
    return pl.pallas_call(
        functools.partial(_merge_kernel, wa=wa, wb=wb),
        grid=(t // tm,),
        in_specs=[pl.BlockSpec((tm, wa), lambda i: (i, 0)),
                  pl.BlockSpec((tm, wb), lambda i: (i, 0)),
                  pl.BlockSpec((tm, wc), lambda i: (i, 0)),
                  pl.BlockSpec((1, wb), lambda i: (0, 0)),
                  pl.BlockSpec((1, wc), lambda i: (0, 0))],
        out_specs=pl.BlockSpec((tm, wa + wb + wc), lambda i: (i, 0)),
        out_shape=jax.ShapeDtypeStruct((t, wa + wb + wc), BF16),
        compiler_params=_params("parallel"),
        name="merge_groups",
    )(ya, yb, yc, g_b.reshape(1, wb), g_c.reshape(1, wc))


def _router_kernel(h_ref, w_ref, b_ref, o_ref):
    logits = jnp.dot(h_ref[...], w_ref[...], preferred_element_type=F32) + b_ref[...]
    lane = lax.broadcasted_iota(jnp.int32, logits.shape, 1).astype(F32)
    m1 = jnp.max(logits, axis=1, keepdims=True)
    i1 = jnp.min(jnp.where(logits == m1, lane, float(ROUTER_LANES)), axis=1, keepdims=True)
    rest = jnp.where(lane == i1, -jnp.inf, logits)
    m2 = jnp.max(rest, axis=1, keepdims=True)
    i2 = jnp.min(jnp.where(rest == m2, lane, float(ROUTER_LANES)), axis=1, keepdims=True)
    e2 = jnp.exp(m2 - m1)
    g1 = 1.0 / (1.0 + e2)
    o_ref[...] = jnp.where(lane == i1, g1, 0.0) + jnp.where(lane == i2, e2 * g1, 0.0)


def _router(h, w_r, b_r, tm=512):
    t, d = h.shape
    e = w_r.shape[1]
    w_pad = jnp.zeros((d, ROUTER_LANES), BF16).at[:, :e].set(w_r.astype(BF16))
    b_pad = jnp.full((1, ROUTER_LANES), -jnp.inf, F32).at[0, :e].set(b_r)
    return pl.pallas_call(
        _router_kernel,
        grid=(t // tm,),
        in_specs=[pl.BlockSpec((tm, d), lambda i: (i, 0)),
                  pl.BlockSpec((d, ROUTER_LANES), lambda i: (0, 0)),
                  pl.BlockSpec((1, ROUTER_LANES), lambda i: (0, 0))],
        out_specs=pl.BlockSpec((tm, ROUTER_LANES), lambda i: (i, 0)),
        out_shape=jax.ShapeDtypeStruct((t, ROUTER_LANES), F32),
        compiler_params=_params("parallel"),
        name="router",
    )(h, w_pad, b_pad)


def kernel(x, c, ctx, c_ctx, ada_down, ada_up, ada_bias, norm1_g, norm2_g, w_in, sgu_norm_g, sgu_w,
           sgu_b, na_rpb, qk_norm_g, group_norm_g, w_out, ffn_w1, ffn_w3, ffn_w2, moe_router,
           moe_router_b, moe_w1, moe_w3, moe_w2, final_norm_g):
    n_batch, n_lat, d = x.shape
    n_ctx = ctx.shape[1]
    depth = w_in.shape[0]
    mix = w_out.shape[1]
    wa = wb = mix // 4
    wc = mix - wa - wb
    n_b_heads = wb // HEAD_DIM
    n_c_heads = wc // HEAD_DIM
    n_kv_heads = n_c_heads // 4
    wkv = n_kv_heads * HEAD_DIM
    group = n_c_heads // n_kv_heads
    n_experts = moe_router.shape[-1]
    dims = dict(n_lat=n_lat, n_batch=n_batch)
    lat_rows = n_batch * n_lat

    off_bq, off_bk, off_bv = 2 * wa, 2 * wa + wb, 2 * wa + 2 * wb
    src_cq = 2 * wa + 3 * wb
    src_ck = src_cq + wc
    off_ck = src_cq
    off_cv = off_ck + wkv
    off_cq = off_cv + wkv

    def reorder_in(w):
        return jnp.concatenate([w[:, :src_cq], w[:, src_ck:], w[:, src_cq:src_ck]], axis=1).astype(BF16)

    xa = jnp.concatenate([x.reshape(lat_rows, d), ctx.reshape(n_batch * n_ctx, d)], axis=0)
    cond = jnp.zeros((8, d), F32).at[:n_batch].set(c).at[n_batch].set(c_ctx)
    mods_all = _ada_mods(cond, ada_down, ada_up, ada_bias)
    cos_t, sin_t = _rope_tables(n_lat, n_ctx)

    for l in range(depth):
        mods = mods_all[l].reshape(8 * N_MOD, 1, d)
        h = _norm_mod(xa, norm1_g[l], mods, 0, 1, **dims)
        p = _matmul(h, reorder_in(w_in[l]), out_dtype=BF16)
        ya = _sgu(p, sgu_norm_g[l], sgu_w[l], sgu_b[l], group_norm_g[l, :wa], width=wa)
        bias = _na_bias_tables(na_rpb[l], n_lat // GRID_W)
        yb_lat = _neighbourhood_attention(
            p, bias, n_batch=n_batch, n_lat=n_lat, n_ctx=n_ctx, q_col=off_bq // HEAD_DIM,
            k_col=off_bk // HEAD_DIM, v_col=off_bv // HEAD_DIM, n_heads=n_b_heads)
        kb_ctx = p[lat_rows:, off_bk:off_bk + wb].reshape(n_batch, n_ctx, wb)
        vb_ctx = p[lat_rows:, off_bv:off_bv + wb].reshape(n_batch, n_ctx, wb)
        yb_ctx = _flash(p, kb_ctx, vb_ctx, n_batch=n_batch, n_q_rows=n_ctx, q_row0=lat_rows,
                        q_rows_per_batch=n_ctx, q_col=off_bq // HEAD_DIM, n_kv_heads=n_b_heads,
                        group=1, kv_row0=0, n_kv_rows=n_ctx)
        qc, kc, vc = _qk_prep(p, cos_t, sin_t, qk_norm_g[l, 0], qk_norm_g[l, 1], n_batch=n_batch,
                              n_lat=n_lat, n_ctx=n_ctx, q_off=off_cq // wc, n_q=n_c_heads,
                              k_off=off_ck // wkv, v_off=off_cv // wkv, n_kv=n_kv_heads)
        yc_lat = _flash(qc, kc, vc, n_batch=n_batch, n_q_rows=n_lat, q_row0=0,
                        q_rows_per_batch=n_lat, q_col=0, n_kv_heads=n_kv_heads, group=group,
                        kv_row0=0, n_kv_rows=n_lat + n_ctx)
        yc_ctx = _flash(qc, kc, vc, n_batch=n_batch, n_q_rows=n_ctx, q_row0=lat_rows,
                        q_rows_per_batch=n_ctx, q_col=0, n_kv_heads=n_kv_heads, group=group,
                        kv_row0=n_lat, n_kv_rows=n_ctx)
        yb = jnp.concatenate([yb_lat, yb_ctx], axis=0)
        yc = jnp.concatenate([yc_lat, yc_ctx], axis=0)
        merged = _merge(ya, yb, yc, group_norm_g[l, wa:wa + wb], group_norm_g[l, wa + wb:])
        xa = _matmul_residual(merged, w_out[l].astype(BF16), xa, mods, 2, **dims)
        h2 = _norm_mod(xa, norm2_g[l], mods, 3, 4, **dims)
        j = l // 2
        if l % 2 == 0:
            hid = _matmul_swiglu(h2, ffn_w1[j].astype(BF16), ffn_w3[j].astype(BF16))
            xa = _matmul_residual(hid, ffn_w2[j].astype(BF16), xa, mods, 5, **dims)
        else:
            gates = _router(h2, moe_router[j], moe_router_b[j])
            w1, w3, w2 = moe_w1[j].astype(BF16), moe_w3[j].astype(BF16), moe_w2[j].astype(BF16)
            for e in range(n_experts):
                hid = _matmul_swiglu(h2, w1, w3, expert=e)
                xa = _matmul_residual(hid, w2, xa, mods, 5, expert=e, row_gate=gates, **dims)

    out = _final_norm(xa, final_norm_g, lat_rows)
    return out.reshape(n_batch, n_lat, d)
```

```python
import functools
import math

import numpy as np
import jax
import jax.numpy as jnp
from jax import lax
from jax.experimental import pallas as pl
from jax.experimental.pallas import tpu as pltpu

F32 = jnp.float32
BF16 = jnp.bfloat16

HEAD_DIM = 128
GRID_W = 64
CHUNK = 128
NA_ROWS = 8
NA_COLS = 16
ROPE_THETA = 10000.0
EPS = 1e-6
N_MOD = 6
MASK_VALUE = -1e30
ROUTER_LANES = 128
MOE_TILE = 512
NA_Q_ROWS = 8
NA_BAND_ROWS = 16
LOG2E = 1.4426950408889634
QK_PRESCALE = HEAD_DIM ** -0.5 * LOG2E
V7X_VMEM_BYTES = 64 * 1024 * 1024
VMEM_LIMIT = V7X_VMEM_BYTES - 8 * 1024 * 1024


def _params(*sem):
    return pltpu.CompilerParams(dimension_semantics=sem, vmem_limit_bytes=VMEM_LIMIT)


def _rms(x, g):
    return x * lax.rsqrt(jnp.mean(x * x, axis=-1, keepdims=True) + EPS) * g


def _ada_down_kernel(c_ref, w_ref, o_ref):
    c = c_ref[...]
    a = (c * jax.nn.sigmoid(c)).astype(BF16)
    o_ref[0] = jnp.dot(a, w_ref[0].astype(BF16), preferred_element_type=F32)


def _ada_up_kernel(t_ref, w_ref, b_ref, o_ref):
    t = t_ref[0].astype(BF16)
    o_ref[0] = jnp.dot(t, w_ref[0].astype(BF16), preferred_element_type=F32) + b_ref[0]


def _ada_mods(cond, w_down, w_up, b_up):
    depth, d, r = w_down.shape
    n_out = w_up.shape[2]
    rows = cond.shape[0]
    tn1 = min(r, 512)
    t = pl.pallas_call(
        _ada_down_kernel,
        grid=(depth, r // tn1),
        in_specs=[pl.BlockSpec((rows, d), lambda l, j: (0, 0)),
                  pl.BlockSpec((1, d, tn1), lambda l, j: (l, 0, j))],
        out_specs=pl.BlockSpec((1, rows, tn1), lambda l, j: (l, 0, j)),
        out_shape=jax.ShapeDtypeStruct((depth, rows, r), F32),
        compiler_params=_params("parallel", "parallel"),
        name="ada_down",
    )(cond, w_down)
    tn2 = min(n_out, 2048)
    return pl.pallas_call(
        _ada_up_kernel,
        grid=(depth, n_out // tn2),
        in_specs=[pl.BlockSpec((1, rows, r), lambda l, j: (l, 0, 0)),
                  pl.BlockSpec((1, r, tn2), lambda l, j: (l, 0, j)),
                  pl.BlockSpec((1, 1, tn2), lambda l, j: (l, 0, j))],
        out_specs=pl.BlockSpec((1, rows, tn2), lambda l, j: (l, 0, j)),
        out_shape=jax.ShapeDtypeStruct((depth, rows, n_out), F32),
        compiler_params=_params("parallel", "parallel"),
        name="ada_up",
    )(t, w_up, b_up.reshape(depth, 1, n_out))


def _norm_mod_kernel(x_ref, g_ref, sh_ref, sc_ref, o_ref):
    y = _rms(x_ref[...], g_ref[...])
    o_ref[...] = (y * (1.0 + sc_ref[0]) + sh_ref[0]).astype(o_ref.dtype)


def _norm_kernel(x_ref, g_ref, o_ref):
    o_ref[...] = _rms(x_ref[...], g_ref[...]).astype(o_ref.dtype)


def _seg_fn(tm, n_lat, n_batch):
    return lambda i: jnp.minimum((i * tm) // n_lat, n_batch)


def _norm_mod(x, g, mods, k_shift, k_scale, *, n_lat, n_batch, out_dtype, tm=256):
    t, d = x.shape
    seg = _seg_fn(tm, n_lat, n_batch)
    return pl.pallas_call(
        _norm_mod_kernel,
        grid=(t // tm,),
        in_specs=[pl.BlockSpec((tm, d), lambda i: (i, 0)),
                  pl.BlockSpec((1, d), lambda i: (0, 0)),
                  pl.BlockSpec((1, 1, d), lambda i: (seg(i) * N_MOD + k_shift, 0, 0)),
                  pl.BlockSpec((1, 1, d), lambda i: (seg(i) * N_MOD + k_scale, 0, 0))],
        out_specs=pl.BlockSpec((tm, d), lambda i: (i, 0)),
        out_shape=jax.ShapeDtypeStruct((t, d), out_dtype),
        compiler_params=_params("parallel"),
        name="norm_mod",
    )(x, g.reshape(1, d), mods, mods)


def _final_norm(x, g, rows, tm=256):
    d = x.shape[1]
    return pl.pallas_call(
        _norm_kernel,
        grid=(rows // tm,),
        in_specs=[pl.BlockSpec((tm, d), lambda i: (i, 0)),
                  pl.BlockSpec((1, d), lambda i: (0, 0))],
        out_specs=pl.BlockSpec((tm, d), lambda i: (i, 0)),
        out_shape=jax.ShapeDtypeStruct((rows, d), F32),
        compiler_params=_params("parallel"),
        name="final_norm",
    )(x, g.reshape(1, d))


def _mm_kernel(a_ref, b_ref, o_ref):
    o_ref[...] = jnp.dot(a_ref[...], b_ref[...],
                         preferred_element_type=F32).astype(o_ref.dtype)


def _mm_swiglu_kernel(a_ref, w1_ref, w3_ref, o_ref):
    a = a_ref[...]
    h1 = jnp.dot(a, w1_ref[...], preferred_element_type=F32)
    h3 = jnp.dot(a, w3_ref[...], preferred_element_type=F32)
    o_ref[...] = (h1 * jax.nn.sigmoid(h1) * h3).astype(o_ref.dtype)


def _mm_res_kernel(a_ref, b_ref, x_ref, g_ref, o_ref):
    acc = jnp.dot(a_ref[...], b_ref[...], preferred_element_type=F32)
    o_ref[...] = x_ref[...] + g_ref[0] * acc


def _w_spec(w, tn):
    return pl.BlockSpec((w.shape[0], tn), lambda i, j: (0, j))


def _matmul(a, w, *, out_dtype, tm=512, tn=1024):
    m, k = a.shape
    n = w.shape[-1]
    tn = min(tn, n)
    return pl.pallas_call(
        _mm_kernel,
        grid=(m // tm, n // tn),
        in_specs=[pl.BlockSpec((tm, k), lambda i, j: (i, 0)), _w_spec(w, tn)],
        out_specs=pl.BlockSpec((tm, tn), lambda i, j: (i, j)),
        out_shape=jax.ShapeDtypeStruct((m, n), out_dtype),
        compiler_params=_params("parallel", "arbitrary"),
        name="matmul",
    )(a, w)


def _matmul_swiglu(a, w1, w3, *, tm=512, tn=512):
    m, k = a.shape
    n = w1.shape[-1]
    tn = min(tn, n)
    return pl.pallas_call(
        _mm_swiglu_kernel,
        grid=(m // tm, n // tn),
        in_specs=[pl.BlockSpec((tm, k), lambda i, j: (i, 0)),
                  _w_spec(w1, tn), _w_spec(w3, tn)],
        out_specs=pl.BlockSpec((tm, tn), lambda i, j: (i, j)),
        out_shape=jax.ShapeDtypeStruct((m, n), BF16),
        compiler_params=_params("parallel", "arbitrary"),
        name="matmul_swiglu",
    )(a, w1, w3)


def _matmul_residual(a, w, x, mods, k_gate, *, n_lat, n_batch, tm=512, tn=512):
    m, k = a.shape
    n = w.shape[-1]
    tn = min(tn, n)
    seg = _seg_fn(tm, n_lat, n_batch)
    return pl.pallas_call(
        _mm_res_kernel,
        grid=(m // tm, n // tn),
        in_specs=[pl.BlockSpec((tm, k), lambda i, j: (i, 0)), _w_spec(w, tn),
                  pl.BlockSpec((tm, tn), lambda i, j: (i, j)),
                  pl.BlockSpec((1, 1, tn), lambda i, j: (seg(i) * N_MOD + k_gate, 0, j))],
        out_specs=pl.BlockSpec((tm, tn), lambda i, j: (i, j)),
        out_shape=jax.ShapeDtypeStruct((m, n), F32),
        compiler_params=_params("parallel", "arbitrary"),
        name="matmul_residual",
    )(a, w, x, mods)


def _sgu_kernel(u_ref, v_ref, gv_ref, ws_ref, bs_ref, ga_ref, o_ref, *, n_chunks, n_groups):
    for c in range(n_chunks):
        rows = slice(c * CHUNK, (c + 1) * CHUNK)
        u = jax.nn.gelu(u_ref[rows, :].astype(F32), approximate=True)
        v = jax.nn.gelu(v_ref[rows, :].astype(F32), approximate=True)
        vb = _rms(v, gv_ref[...]).astype(BF16)
        parts = [jnp.dot(ws_ref[g], vb[:, g * HEAD_DIM:(g + 1) * HEAD_DIM],
                         preferred_element_type=F32) for g in range(n_groups)]
        y = u * (jnp.concatenate(parts, axis=1) + bs_ref[...])
        o_ref[rows, :] = _rms(y, ga_ref[...]).astype(o_ref.dtype)


def _sgu(p, g_v, w_s, b_s, g_a, *, width, ta=256):
    t = p.shape[0]
    n_groups = width // HEAD_DIM
    bias =jnp.repeat(b_s.T, HEAD_DIM, axis=1)
    return pl.pallas_call(
        functools.partial(_sgu_kernel, n_chunks=ta // CHUNK, n_groups=n_groups),
        grid=(t // ta,),
        in_specs=[pl.BlockSpec((ta, width), lambda i: (i, 0)),
                  pl.BlockSpec((ta, width), lambda i: (i, 1)),
                  pl.BlockSpec((1, width), lambda i: (0, 0)),
                  pl.BlockSpec((n_groups, CHUNK, CHUNK), lambda i: (0, 0, 0)),
                  pl.BlockSpec((CHUNK, width), lambda i: (0, 0)),
                  pl.BlockSpec((1, width), lambda i: (0, 0))],
        out_specs=pl.BlockSpec((ta, width), lambda i: (i, 0)),
        out_shape=jax.ShapeDtypeStruct((t, width), BF16),
        compiler_params=_params("parallel"),
        name="sgu",
    )(p, p, g_v.reshape(1, width), w_s.astype(BF16), bias, g_a.reshape(1, width))


def _na_bias_tables(rpb, rows):
    n_blocks = rows // NA_Q_ROWS
    cols = np.arange(GRID_W)
    col_start = np.clip(cols - NA_COLS // 2, 0, GRID_W - NA_COLS)
    dc = cols[None, :] - cols[:, None]
    col_ok = (cols[None, :] >= col_start[:, None]) & (cols[None, :] < col_start[:, None] + NA_COLS)
    col_idx = np.where(col_ok, dc + NA_COLS - 1, 0)
    n_dr = 2 * NA_ROWS - 1
    t = jnp.where(col_ok[None, None], rpb[:, :, col_idx], MASK_VALUE)
    t = jnp.concatenate([t, jnp.full((rpb.shape[0], 1, GRID_W, GRID_W), MASK_VALUE, F32)], axis=1)
    slab = np.full((3, NA_Q_ROWS, NA_BAND_ROWS), n_dr, np.int32)
    rep = [0, min(1, n_blocks - 1), n_blocks - 1]
    for ty, blk in enumerate(rep):
        band0 = int(np.clip(blk * NA_Q_ROWS - NA_ROWS // 2, 0, rows - NA_BAND_ROWS))
        for rq in range(NA_Q_ROWS):
            r = blk * NA_Q_ROWS + rq
            start = int(np.clip(r - NA_ROWS // 2, 0, rows - NA_ROWS))
            for kr in range(NA_BAND_ROWS):
                ar = band0 + kr
                if start <= ar < start + NA_ROWS:
                    slab[ty, rq, kr] = ar - r + NA_ROWS - 1
    full = t[:, slab]
    full = full.transpose(0, 1, 2, 4, 3, 5)
    return full.reshape(rpb.shape[0], 3, NA_Q_ROWS * GRID_W, NA_BAND_ROWS * GRID_W)


def _na_kernel(q_ref, k_ref, v_ref, kc_ref, vc_ref, bias_ref, o_ref, *, rows, scale):
    i = pl.program_id(2)
    band0 = jnp.clip(i * NA_Q_ROWS - NA_ROWS // 2, 0, rows - NA_BAND_ROWS)
    start = pl.multiple_of(band0 * GRID_W, GRID_W)
    nb = NA_BAND_ROWS * GRID_W
    q = q_ref[...]
    kb = k_ref[pl.ds(start, nb), :]
    vb = v_ref[pl.ds(start, nb), :]
    dn = (((1,), (1,)), ((), ()))
    s_loc = lax.dot_general(q, kb, dn, preferred_element_type=F32) * scale + bias_ref[...]
    s_ctx = lax.dot_general(q, kc_ref[...], dn, preferred_element_type=F32) * scale
    m = jnp.maximum(jnp.max(s_loc, axis=1, keepdims=True), jnp.max(s_ctx, axis=1, keepdims=True))
    p_loc = jnp.exp(s_loc - m)
    p_ctx = jnp.exp(s_ctx - m)
    l = jnp.sum(p_loc, axis=1, keepdims=True) + jnp.sum(p_ctx, axis=1, keepdims=True)
    o = (jnp.dot(p_loc.astype(BF16), vb, preferred_element_type=F32)
         + jnp.dot(p_ctx.astype(BF16), vc_ref[...], preferred_element_type=F32))
    o_ref[...] = (o / l).astype(o_ref.dtype)


def _neighbourhood_attention(p, bias, *, n_batch, n_lat, n_ctx, q_col, k_col, v_col, n_heads):
    rows = n_lat // GRID_W
    tq = NA_Q_ROWS * GRID_W
    nbk = NA_BAND_ROWS * GRID_W
    n_blocks = n_lat // tq
    ctx_blk0 = (n_batch * n_lat) // n_ctx

    def btype(i):
        return jnp.where(i == 0, 0, jnp.where(i == n_blocks - 1, 2, 1))

    return pl.pallas_call(
        functools.partial(_na_kernel, rows=rows, scale=HEAD_DIM ** -0.5),
        grid=(n_heads, n_batch, n_blocks),
        in_specs=[pl.BlockSpec((tq, HEAD_DIM), lambda h, b, i: (b * n_blocks + i, q_col + h)),
                  pl.BlockSpec((n_lat, HEAD_DIM), lambda h, b, i: (b, k_col + h)),
                  pl.BlockSpec((n_lat, HEAD_DIM), lambda h, b, i: (b, v_col + h)),
                  pl.BlockSpec((n_ctx, HEAD_DIM), lambda h, b, i: (ctx_blk0 + b, k_col + h)),
                  pl.BlockSpec((n_ctx, HEAD_DIM), lambda h, b, i: (ctx_blk0 + b, v_col + h)),
                  pl.BlockSpec((None, None, tq, nbk), lambda h, b, i: (h, btype(i), 0, 0))],
        out_specs=pl.BlockSpec((tq, HEAD_DIM), lambda h, b, i: (b * n_blocks + i, h)),
        out_shape=jax.ShapeDtypeStruct((n_batch * n_lat, n_heads * HEAD_DIM), BF16),
        compiler_params=_params("parallel", "parallel", "arbitrary"),
        name="neighbourhood_attention",
    )(p, p, p, p, p, bias)


def _rope_tables(n_lat, n_ctx):
    t = jnp.arange(n_lat, dtype=jnp.int32)
    pos = jnp.stack([t // GRID_W, t % GRID_W], axis=-1).astype(F32)
    n_freq = HEAD_DIM // 4
    inv = 1.0 / (ROPE_THETA ** (jnp.arange(n_freq, dtype=F32) / n_freq))
    ang = pos[:, :, None] * inv
    cos, sin = jnp.cos(ang), jnp.sin(ang)
    cos_t = jnp.concatenate([cos[:, 0], cos[:, 0], cos[:, 1], cos[:, 1]], axis=-1)
    sin_t = jnp.concatenate([-sin[:, 0], sin[:, 0], -sin[:, 1], sin[:, 1]], axis=-1)
    cos_t = jnp.concatenate([cos_t, jnp.ones((n_ctx, HEAD_DIM), F32)], axis=0)
    sin_t = jnp.concatenate([sin_t, jnp.zeros((n_ctx, HEAD_DIM), F32)], axis=0)
    return cos_t, sin_t


def _qk_prep_kernel(q_ref, k_ref, v_ref, cos_ref, sin_ref, gq_ref, gk_ref,
                    qo_ref, ko_ref, vo_ref, *, n_q, n_kv):
    cos = cos_ref[...]
    sin = sin_ref[...]
    quarter = HEAD_DIM // 4
    lane = lax.broadcasted_iota(jnp.int32, cos.shape, 1)
    first_half = (lane % (2 * quarter)) < quarter

    def prep(x, g, out_scale):
        y = _rms(x.astype(F32), g)
        swapped = jnp.where(first_half,
                            pltpu.roll(y, HEAD_DIM - quarter, 1),
                            pltpu.roll(y, quarter, 1))
        return ((y * cos + swapped * sin) * out_scale).astype(BF16)

    for h in range(n_q):
        cs = slice(h * HEAD_DIM, (h + 1) * HEAD_DIM)
        qo_ref[:, cs] = prep(q_ref[:, cs], gq_ref[...], QK_PRESCALE)
    ones = jnp.ones((v_ref.shape[0], HEAD_DIM), BF16)
    for h in range(n_kv):
        cs = slice(h * HEAD_DIM, (h + 1) * HEAD_DIM)
        ko_ref[0, :, cs] = prep(k_ref[:, cs], gk_ref[...], 1.0)
        vo_ref[0, :, 2 * h * HEAD_DIM:(2 * h + 1) * HEAD_DIM] = v_ref[:, cs]
        vo_ref[0, :, (2 * h + 1) * HEAD_DIM:(2 * h + 2) * HEAD_DIM] = ones


def _qk_prep(p, cos_t, sin_t, g_q, g_k, *, n_batch, n_lat, n_ctx, q_off, n_q, k_off, v_off, n_kv):
    t = p.shape[0]
    tm = n_ctx
    lat_tiles = n_lat // tm
    n_lat_tiles = n_batch * lat_tiles

    def pos_blk(i):
        return jnp.where(i < n_lat_tiles, i % lat_tiles, lat_tiles)

    def kv_blk(i):
        c = i - n_lat_tiles
        return (jnp.where(i < n_lat_tiles, i // lat_tiles, c),
                jnp.where(i < n_lat_tiles, i % lat_tiles, lat_tiles), 0)

    wq, wkv = n_q * HEAD_DIM, n_kv * HEAD_DIM
    return pl.pallas_call(
        functools.partial(_qk_prep_kernel, n_q=n_q, n_kv=n_kv),
        grid=(t // tm,),
        in_specs=[pl.BlockSpec((tm, wq), lambda i: (i, q_off)),
                  pl.BlockSpec((tm, wkv), lambda i: (i, k_off)),
                  pl.BlockSpec((tm, wkv), lambda i: (i, v_off)),
                  pl.BlockSpec((tm, HEAD_DIM), lambda i: (pos_blk(i), 0)),
                  pl.BlockSpec((tm, HEAD_DIM), lambda i: (pos_blk(i), 0)),
                  pl.BlockSpec((1, HEAD_DIM), lambda i: (0, 0)),
                  pl.BlockSpec((1, HEAD_DIM), lambda i: (0, 0))],
        out_specs=[pl.BlockSpec((tm, wq), lambda i: (i, 0)),
                   pl.BlockSpec((1, tm, wkv), kv_blk),
                   pl.BlockSpec((1, tm, 2 * wkv), kv_blk)],
        out_shape=[jax.ShapeDtypeStruct((t, wq), BF16),
                   jax.ShapeDtypeStruct((n_batch, n_lat + n_ctx, wkv), BF16),
                   jax.ShapeDtypeStruct((n_batch, n_lat + n_ctx, 2 * wkv), BF16)],
        compiler_params=_params("parallel"),
        name="qk_prep",
    )(p, p, p, cos_t, sin_t, g_q.reshape(1, HEAD_DIM), g_k.reshape(1, HEAD_DIM))


def _flash_kernel(q_ref, k_ref, v_ref, o_ref, m_sc, l_sc, acc_sc, *, group, tq, scale):
    kj = pl.program_id(3)

    @pl.when(kj == 0)
    def _():
        m_sc[...] = jnp.full_like(m_sc, -jnp.inf)
        l_sc[...] = jnp.zeros_like(l_sc)
        acc_sc[...] = jnp.zeros_like(acc_sc)

    if group == 1:
        q = q_ref[...]
    else:
        q = jnp.concatenate([q_ref[:, g * HEAD_DIM:(g + 1) * HEAD_DIM] for g in range(group)], axis=0)
    s = lax.dot_general(q, k_ref[0], (((1,), (1,)), ((), ())), preferred_element_type=F32) * scale
    m_prev = m_sc[...]
    m_new = jnp.maximum(m_prev, jnp.max(s, axis=1, keepdims=True))
    alpha = jnp.exp(m_prev - m_new)
    p = jnp.exp(s - m_new)
    l_sc[...] = alpha * l_sc[...] + jnp.sum(p, axis=1, keepdims=True)
    acc_sc[...] = alpha * acc_sc[...] + jnp.dot(p.astype(BF16), v_ref[0], preferred_element_type=F32)
    m_sc[...] = m_new

    @pl.when(kj == pl.num_programs(3) - 1)
    def _():
        o = acc_sc[...] / l_sc[...]
        for g in range(group):
            o_ref[:, g * HEAD_DIM:(g + 1) * HEAD_DIM] = o[g * tq:(g + 1) * tq].astype(o_ref.dtype)


def _pick_tile(n, candidates):
    for c in candidates:
        if n % c == 0:
            return c
    raise ValueError(f"no tile for {n}")


def _flash(q2d, k3d, v3d, *, n_batch, n_q_rows, q_row0, q_rows_per_batch, q_col, n_kv_heads,
           group, kv_row0, n_kv_rows, tq=256):
    tq = min(tq, n_q_rows)
    tk = _pick_tile(n_kv_rows, (1024, 768, 512, 256))
    qb0, qbb = q_row0 // tq, q_rows_per_batch // tq
    kb0 = kv_row0 // tk
    gw = group * HEAD_DIM
    return pl.pallas_call(
        functools.partial(_flash_kernel, group=group, tq=tq, scale=HEAD_DIM ** -0.5),
        grid=(n_batch, n_kv_heads, n_q_rows // tq, n_kv_rows // tk),
        in_specs=[pl.BlockSpec((tq, gw), lambda b, h, i, j: (qb0 + b * qbb + i, q_col + h)),
                  pl.BlockSpec((1, tk, HEAD_DIM), lambda b, h, i, j: (b, kb0 + j, h)),
                  pl.BlockSpec((1, tk, HEAD_DIM), lambda b, h, i, j: (b, kb0 + j, h))],
        out_specs=pl.BlockSpec((tq, gw), lambda b, h, i, j: (b * (n_q_rows // tq) + i, h)),
        out_shape=jax.ShapeDtypeStruct((n_batch * n_q_rows, n_kv_heads * gw), BF16),
        scratch_shapes=[pltpu.VMEM((group * tq, 1), F32),
                        pltpu.VMEM((group * tq, 1), F32),
                        pltpu.VMEM((group * tq, HEAD_DIM), F32)],
        compiler_params=_params("parallel", "parallel", "parallel", "arbitrary"),
        name="flash_attention",
    )(q2d, k3d, v3d)


def _flash_gqa_kernel(q_ref, k_ref, v_ref, o_ref, m_sc, acc_sc, *, group, tq, tk, rc):
    kj = pl.program_id(3)

    @pl.when(kj == 0)
    def _():
        m_sc[...] = jnp.full_like(m_sc, -jnp.inf)
        acc_sc[...] = jnp.zeros_like(acc_sc)

    n_cols = tk // HEAD_DIM
    k = k_ref[0]
    v = v_ref[0]
    chunks = [(g, r0) for g in range(group) for r0 in range(0, tq, rc)]

    def scores(chunk):
        g, r0 = chunk
        q = q_ref[r0:r0 + rc, g * HEAD_DIM:(g + 1) * HEAD_DIM]
        return lax.dot_general(q, k, (((1,), (1,)), ((), ())), preferred_element_type=F32)

    s_next = scores(chunks[0])
    for ci, (g, r0) in enumerate(chunks):
        s = s_next
        if ci + 1 < len(chunks):
            s_next = scores(chunks[ci + 1])
        rows = slice(g * tq + r0, g * tq + r0 + rc)
        cols = [s[:, i * HEAD_DIM:(i + 1) * HEAD_DIM] for i in range(n_cols)]
        col_max = functools.reduce(jnp.maximum, cols)
        m_prev = m_sc[rows, :]
        m_new = jnp.maximum(m_prev, jnp.max(col_max, axis=1, keepdims=True))
        alpha = jnp.exp2(m_prev - m_new)
        p = jnp.concatenate([jnp.exp2(c - m_new).astype(BF16) for c in cols], axis=1)
        pv = jnp.dot(p, v, preferred_element_type=F32)
        acc_sc[rows, :] = jnp.concatenate([alpha, alpha], axis=1) * acc_sc[rows, :] + pv
        m_sc[rows, :] = m_new

    @pl.when(kj == pl.num_programs(3) - 1)
    def _():
        o = acc_sc[:, :HEAD_DIM] / acc_sc[:, HEAD_DIM:]
        for g in range(group):
            o_ref[:, g * HEAD_DIM:(g + 1) * HEAD_DIM] = o[g * tq:(g + 1) * tq].astype(o_ref.dtype)


def _flash_gqa(q2d, k3d, v3d, *, n_batch, n_q_rows, q_row0, q_rows_per_batch, n_kv_heads, group,
               kv_row0, n_kv_rows, tq=512, rc=256):
    tq = min(tq, n_q_rows)
    rc = min(rc, tq)
    tk = _pick_tile(n_kv_rows, (1408, 1024, 768, 512, 256))
    qb0, qbb = q_row0 // tq, q_rows_per_batch // tq
    kb0 = kv_row0 // tk
    gw = group * HEAD_DIM
    return pl.pallas_call(
        functools.partial(_flash_gqa_kernel, group=group, tq=tq, tk=tk, rc=rc),
        grid=(n_batch, n_kv_heads, n_q_rows // tq, n_kv_rows // tk),
        in_specs=[pl.BlockSpec((tq, gw), lambda b, h, i, j: (qb0 + b * qbb + i, h)),
                  pl.BlockSpec((1, tk, HEAD_DIM), lambda b, h, i, j: (b, kb0 + j, h)),
                  pl.BlockSpec((1, tk, 2 * HEAD_DIM), lambda b, h, i, j: (b, kb0 + j, h))],
        out_specs=pl.BlockSpec((tq, gw), lambda b, h, i, j: (b * (n_q_rows // tq) + i, h)),
        out_shape=jax.ShapeDtypeStruct((n_batch * n_q_rows, n_kv_heads * gw), BF16),
        scratch_shapes=[pltpu.VMEM((group * tq, HEAD_DIM), F32),
                        pltpu.VMEM((group * tq, 2 * HEAD_DIM), F32)],
        compiler_params=_params("parallel", "parallel", "parallel", "arbitrary"),
        name="flash_gqa",
    )(q2d, k3d, v3d)


def _merge_kernel(ya_ref, yb_ref, yc_ref, gb_ref, gc_ref, o_ref, *, wa, wb):
    o_ref[:, :wa] = ya_ref[...]
    o_ref[:, wa:wa + wb] = _rms(yb_ref[...].astype(F32), gb_ref[...]).astype(o_ref.dtype)
    o_ref[:, wa + wb:] = _rms(yc_ref[...].astype(F32), gc_ref[...]).astype(o_ref.dtype)


def _merge(ya, yb, yc, g_b, g_c, tm=256):
    t, wa = ya.shape
    wb, wc = yb.shape[1], yc.shape[1]
    return pl.pallas_call(
        functools.partial(_merge_kernel, wa=wa, wb=wb),
        grid=(t // tm,),
        in_specs=[pl.BlockSpec((tm, wa), lambda i: (i, 0)),
                  pl.BlockSpec((tm, wb), lambda i: (i, 0)),
                  pl.BlockSpec((tm, wc), lambda i: (i, 0)),
                  pl.BlockSpec((1, wb), lambda i: (0, 0)),
                  pl.BlockSpec((1, wc), lambda i: (0, 0))],
        out_specs=pl.BlockSpec((tm, wa + wb + wc), lambda i: (i, 0)),
        out_shape=jax.ShapeDtypeStruct((t, wa + wb + wc), BF16),
        compiler_params=_params("parallel"),
        name="merge_groups",
    )(ya, yb, yc, g_b.reshape(1, wb), g_c.reshape(1, wc))


def _router_kernel(h_ref, w_ref, b_ref, o_ref):
    logits = jnp.dot(h_ref[...].astype(BF16), w_ref[...], preferred_element_type=F32) + b_ref[...]
    lane = lax.broadcasted_iota(jnp.int32, logits.shape, 1).astype(F32)
    m1 = jnp.max(logits, axis=1, keepdims=True)
    i1 = jnp.min(jnp.where(logits == m1, lane, float(ROUTER_LANES)), axis=1, keepdims=True)
    rest = jnp.where(lane == i1, -jnp.inf, logits)
    m2 = jnp.max(rest, axis=1, keepdims=True)
    i2 = jnp.min(jnp.where(rest == m2, lane, float(ROUTER_LANES)), axis=1, keepdims=True)
    e2 = jnp.exp(m2 - m1)
    g1 = 1.0 / (1.0 + e2)
    o_ref[...] = jnp.where(lane == 0.0, i1,
                           jnp.where(lane == 1.0, i2,
                                     jnp.where(lane == 2.0, g1,
                                               jnp.where(lane == 3.0, e2 * g1, 0.0))))


def _route_plan(routes, n_experts, tm):
    t = routes.shape[0]
    a = 2 * t
    flat_e = routes[:, :2].astype(jnp.int32).reshape(a)
    onehot = (flat_e[:, None] == jnp.arange(n_experts, dtype=jnp.int32)[None, :]).astype(jnp.int32)
    csum = jnp.cumsum(onehot, axis=0)
    counts = csum[-1]
    rank = jnp.take_along_axis(csum, flat_e[:, None], axis=1)[:, 0] - 1
    padded = ((counts + tm - 1) // tm) * tm
    ends = jnp.cumsum(padded)
    starts = ends - padded
    dest = starts[flat_e] + rank
    ustarts = jnp.cumsum(counts) - counts
    order = jnp.argsort(flat_e, stable=True).astype(jnp.int32)
    n_rows = a + n_experts * tm
    row = jnp.arange(n_rows, dtype=jnp.int32)
    row_e = jnp.minimum(jnp.searchsorted(ends, row, side="right"), n_experts - 1).astype(jnp.int32)
    k = row - starts[row_e]
    src_a = order[jnp.clip(ustarts[row_e] + k, 0, a - 1)]
    src = jnp.where(k < counts[row_e], src_a // 2, 0).astype(jnp.int32)
    tile_start = jnp.arange(n_rows // tm, dtype=jnp.int32) * tm
    tile_valid = (tile_start < ends[-1]).astype(jnp.int32)
    tile_expert = row_e[jnp.minimum(tile_start, ends[-1] - 1)]
    return src, dest.astype(jnp.int32), tile_expert, tile_valid


def _gather_rows_kernel(src_ref, h_hbm, o_hbm, sem, *, window):
    n = o_hbm.shape[0]

    def copy(i):
        return pltpu.make_async_copy(h_hbm.at[pl.ds(src_ref[i], 1)], o_hbm.at[pl.ds(i, 1)], sem)

    for i in range(window):
        copy(i).start()

    def body(i, carry):
        copy(i - window).wait()
        copy(i).start()
        return carry

    lax.fori_loop(window, n, body, 0)
    for i in range(window):
        copy(n - window + i).wait()


def _gather_rows(h, src, window=16):
    n = src.shape[0]
    return pl.pallas_call(
        functools.partial(_gather_rows_kernel, window=window),
        grid_spec=pltpu.PrefetchScalarGridSpec(
            num_scalar_prefetch=1, grid=(1,),
            in_specs=[pl.BlockSpec(memory_space=pl.ANY)],
            out_specs=pl.BlockSpec(memory_space=pl.ANY),
            scratch_shapes=[pltpu.SemaphoreType.DMA]),
        out_shape=jax.ShapeDtypeStruct((n, h.shape[1]), h.dtype),
        compiler_params=_params("arbitrary"),
        name="moe_gather",
    )(src, h)


def _moe_up_kernel(te_ref, tv_ref, x_ref, w1_ref, w3_ref, o_ref):
    del te_ref
    valid = tv_ref[pl.program_id(0)] == 1

    @pl.when(valid)
    def _():
        a = x_ref[...].astype(BF16)
        h1 = jnp.dot(a, w1_ref[...], preferred_element_type=F32)
        h3 = jnp.dot(a, w3_ref[...], preferred_element_type=F32)
        o_ref[...] = (h1 * jax.nn.sigmoid(h1) * h3).astype(o_ref.dtype)

    @pl.when(jnp.logical_not(valid))
    def _():
        o_ref[...] = jnp.zeros_like(o_ref)


def _moe_down_kernel(te_ref, tv_ref, a_ref, w_ref, o_ref):
    del te_ref
    valid = tv_ref[pl.program_id(0)] == 1

    @pl.when(valid)
    def _():
        o_ref[...] = jnp.dot(a_ref[...], w_ref[...], preferred_element_type=F32)

    @pl.when(jnp.logical_not(valid))
    def _():
        o_ref[...] = jnp.zeros_like(o_ref)


def _moe_experts(xs, w1, w3, w2, tile_expert, tile_valid, *, tm, tn_up=512, tn_down=1024):
    rows, d = xs.shape
    f = w1.shape[-1]
    tn_up, tn_down = min(tn_up, f), min(tn_down, d)
    n_tiles = rows // tm
    hid = pl.pallas_call(
        _moe_up_kernel,
        grid_spec=pltpu.PrefetchScalarGridSpec(
            num_scalar_prefetch=2, grid=(n_tiles, f // tn_up),
            in_specs=[pl.BlockSpec((tm, d), lambda i, j, te, tv: (i, 0)),
                      pl.BlockSpec((None, d, tn_up), lambda i, j, te, tv: (te[i], 0, j)),
                      pl.BlockSpec((None, d, tn_up), lambda i, j, te, tv: (te[i], 0, j))],
            out_specs=pl.BlockSpec((tm, tn_up), lambda i, j, te, tv: (i, j))),
        out_shape=jax.ShapeDtypeStruct((rows, f), BF16),
        compiler_params=_params("parallel", "arbitrary"),
        name="moe_up",
    )(tile_expert, tile_valid, xs, w1, w3)
    return pl.pallas_call(
        _moe_down_kernel,
        grid_spec=pltpu.PrefetchScalarGridSpec(
            num_scalar_prefetch=2, grid=(n_tiles, d // tn_down),
            in_specs=[pl.BlockSpec((tm, f), lambda i, j, te, tv: (i, 0)),
                      pl.BlockSpec((None, f, tn_down), lambda i, j, te, tv: (te[i], 0, j))],
            out_specs=pl.BlockSpec((tm, tn_down), lambda i, j, te, tv: (i, j))),
        out_shape=jax.ShapeDtypeStruct((rows, d), F32),
        compiler_params=_params("parallel", "arbitrary"),
        name="moe_down",
    )(tile_expert, tile_valid, hid, w2)


def _moe_combine_kernel(dest_ref, x_ref, r_ref, g_ref, y_hbm, o_ref, ybuf, sem, *, tc):
    base = pl.program_id(0) * (2 * tc)

    def copy(j, slot):
        return pltpu.make_async_copy(y_hbm.at[pl.ds(dest_ref[base + 2 * j + slot], 1)],
                                     ybuf.at[slot, pl.ds(j, 1)], sem)

    def issue(j, carry):
        copy(j, 0).start()
        copy(j, 1).start()
        return carry

    def drain(j, carry):
        copy(j, 0).wait()
        copy(j, 1).wait()
        return carry

    lax.fori_loop(0, tc, issue, 0)
    lax.fori_loop(0, tc, drain, 0)
    r = r_ref[...]
    y = r[:, 2:3] * ybuf[0] + r[:, 3:4] * ybuf[1]
    o_ref[...] = x_ref[...] + g_ref[0] * y


def _moe_combine(x, y, routes, dest, mods, k_gate, *, n_lat, n_batch, tc=256):
    t, d = x.shape
    seg = _seg_fn(tc, n_lat, n_batch)
    return pl.pallas_call(
        functools.partial(_moe_combine_kernel, tc=tc),
        grid_spec=pltpu.PrefetchScalarGridSpec(
            num_scalar_prefetch=1, grid=(t // tc,),
            in_specs=[pl.BlockSpec((tc, d), lambda i, ds: (i, 0)),
                      pl.BlockSpec((tc, ROUTER_LANES), lambda i, ds: (i, 0)),
                      pl.BlockSpec((1, 1, d), lambda i, ds: (seg(i) * N_MOD + k_gate, 0, 0)),
                      pl.BlockSpec(memory_space=pl.ANY)],
            out_specs=pl.BlockSpec((tc, d), lambda i, ds: (i, 0)),
            scratch_shapes=[pltpu.VMEM((2, tc, d), F32), pltpu.SemaphoreType.DMA]),
        out_shape=jax.ShapeDtypeStruct((t, d), F32),
        compiler_params=_params("arbitrary"),
        name="moe_combine",
    )(dest, x, routes, mods, y)


def _router(h, w_r, b_r, tm=512):
    t, d = h.shape
    e = w_r.shape[1]
    w_pad = jnp.zeros((d, ROUTER_LANES), BF16).at[:, :e].set(w_r.astype(BF16))
    b_pad = jnp.full((1, ROUTER_LANES), -jnp.inf, F32).at[0, :e].set(b_r)
    return pl.pallas_call(
        _router_kernel,
        grid=(t // tm,),
        in_specs=[pl.BlockSpec((tm, d), lambda i: (i, 0)),
                  pl.BlockSpec((d, ROUTER_LANES), lambda i: (0, 0)),
                  pl.BlockSpec((1, ROUTER_LANES), lambda i: (0, 0))],
        out_specs=pl.BlockSpec((tm, ROUTER_LANES), lambda i: (i, 0)),
        out_shape=jax.ShapeDtypeStruct((t, ROUTER_LANES), F32),
        compiler_params=_params("parallel"),
        name="router",
    )(h, w_pad, b_pad)


def kernel(x, c, ctx, c_ctx, ada_down, ada_up, ada_bias, norm1_g, norm2_g, w_in, sgu_norm_g, sgu_w,
           sgu_b, na_rpb, qk_norm_g, group_norm_g, w_out, ffn_w1, ffn_w3, ffn_w2, moe_router,
           moe_router_b, moe_w1, moe_w3, moe_w2, final_norm_g):
    n_batch, n_lat, d = x.shape
    n_ctx = ctx.shape[1]
    depth = w_in.shape[0]
    mix = w_out.shape[1]
    wa = wb = mix // 4
    wc = mix - wa - wb
    n_b_heads = wb // HEAD_DIM
    n_c_heads = wc // HEAD_DIM
    n_kv_heads = n_c_heads // 4
    wkv = n_kv_heads * HEAD_DIM
    group = n_c_heads // n_kv_heads
    n_experts = moe_router.shape[-1]
    dims = dict(n_lat=n_lat, n_batch=n_batch)
    lat_rows = n_batch * n_lat

    off_bq, off_bk, off_bv = 2 * wa, 2 * wa + wb, 2 * wa + 2 * wb
    src_cq = 2 * wa + 3 * wb
    src_ck = src_cq + wc
    off_ck = src_cq
    off_cv = off_ck + wkv
    off_cq = off_cv + wkv

    def reorder_in(w):
        return jnp.concatenate([w[:, :src_cq], w[:, src_ck:], w[:, src_cq:src_ck]], axis=1).astype(BF16)

    xa = jnp.concatenate([x.reshape(lat_rows, d), ctx.reshape(n_batch * n_ctx, d)], axis=0)
    cond = jnp.zeros((8, d), F32).at[:n_batch].set(c).at[n_batch].set(c_ctx)
    mods_all = _ada_mods(cond, ada_down, ada_up, ada_bias)
    cos_t, sin_t = _rope_tables(n_lat, n_ctx)

    for l in range(depth):
        mods = mods_all[l].reshape(8 * N_MOD, 1, d)
        h = _norm_mod(xa, norm1_g[l], mods, 0, 1, out_dtype=BF16, **dims)
        p = _matmul(h, reorder_in(w_in[l]), out_dtype=BF16)
        ya = _sgu(p, sgu_norm_g[l], sgu_w[l], sgu_b[l], group_norm_g[l, :wa], width=wa)
        bias = _na_bias_tables(na_rpb[l], n_lat // GRID_W)
        yb_lat = _neighbourhood_attention(
            p, bias, n_batch=n_batch, n_lat=n_lat, n_ctx=n_ctx, q_col=off_bq // HEAD_DIM,
            k_col=off_bk // HEAD_DIM, v_col=off_bv // HEAD_DIM, n_heads=n_b_heads)
        kb_ctx = p[lat_rows:, off_bk:off_bk + wb].reshape(n_batch, n_ctx, wb)
        vb_ctx = p[lat_rows:, off_bv:off_bv + wb].reshape(n_batch, n_ctx, wb)
        yb_ctx = _flash(p, kb_ctx, vb_ctx, n_batch=n_batch, n_q_rows=n_ctx, q_row0=lat_rows,
                        q_rows_per_batch=n_ctx, q_col=off_bq // HEAD_DIM, n_kv_heads=n_b_heads,
                        group=1, kv_row0=0, n_kv_rows=n_ctx)
        qc, kc, vc = _qk_prep(p, cos_t, sin_t, qk_norm_g[l, 0], qk_norm_g[l, 1], n_batch=n_batch,
                              n_lat=n_lat, n_ctx=n_ctx, q_off=off_cq // wc, n_q=n_c_heads,
                              k_off=off_ck // wkv, v_off=off_cv // wkv, n_kv=n_kv_heads)
        yc_lat = _flash_gqa(qc, kc, vc, n_batch=n_batch, n_q_rows=n_lat, q_row0=0,
                            q_rows_per_batch=n_lat, n_kv_heads=n_kv_heads, group=group,
                            kv_row0=0, n_kv_rows=n_lat + n_ctx)
        yc_ctx = _flash_gqa(qc, kc, vc, n_batch=n_batch, n_q_rows=n_ctx, q_row0=lat_rows,
                            q_rows_per_batch=n_ctx, n_kv_heads=n_kv_heads, group=group,
                            kv_row0=n_lat, n_kv_rows=n_ctx)
        yb = jnp.concatenate([yb_lat, yb_ctx], axis=0)
        yc = jnp.concatenate([yc_lat, yc_ctx], axis=0)
        merged = _merge(ya, yb, yc, group_norm_g[l, wa:wa + wb], group_norm_g[l, wa + wb:])
        xa = _matmul_residual(merged, w_out[l].astype(BF16), xa, mods, 2, **dims)
        j = l // 2
        if l % 2 == 0:
            h2 = _norm_mod(xa, norm2_g[l], mods, 3, 4, out_dtype=BF16, **dims)
            hid = _matmul_swiglu(h2, ffn_w1[j].astype(BF16), ffn_w3[j].astype(BF16))
            xa = _matmul_residual(hid, ffn_w2[j].astype(BF16), xa, mods, 5, **dims)
        else:
            h2 = _norm_mod(xa, norm2_g[l], mods, 3, 4, out_dtype=F32, **dims)
            routes = _router(h2, moe_router[j], moe_router_b[j])
            src, dest, tile_expert, tile_valid = _route_plan(routes, n_experts, MOE_TILE)
            xs = _gather_rows(h2, src)
            ys = _moe_experts(xs, moe_w1[j].astype(BF16), moe_w3[j].astype(BF16),
                              moe_w2[j].astype(BF16), tile_expert, tile_valid, tm=MOE_TILE)
            xa = _moe_combine(xa, ys, routes, dest, mods, 5, **dims)

    out = _final_norm(xa, final_norm_g, lat_rows)
    return out.reshape(n_batch, n_lat, d)
```

```python
import functools
import math

import numpy as np
import jax
import jax.numpy as jnp
from jax import lax
from jax.experimental import pallas as pl
from jax.experimental.pallas import tpu as pltpu

F32 = jnp.float32
BF16 = jnp.bfloat16

HEAD_DIM = 128
GRID_W = 64
CHUNK = 128
NA_ROWS = 8
NA_COLS = 16
ROPE_THETA = 10000.0
EPS = 1e-6
N_MOD = 6
MASK_VALUE = -1e30
ROUTER_LANES = 128
MOE_TILE = 512
FLASH_LOOKAHEAD = 3
NA_Q_ROWS = 8
NA_BAND_ROWS = 16
LOG2E = 1.4426950408889634
QK_PRESCALE = HEAD_DIM ** -0.5 * LOG2E
V7X_VMEM_BYTES = 64 * 1024 * 1024
VMEM_LIMIT = V7X_VMEM_BYTES - 8 * 1024 * 1024


def _params(*sem):
    return pltpu.CompilerParams(dimension_semantics=sem, vmem_limit_bytes=VMEM_LIMIT)


def _rms(x, g):
    return x * lax.rsqrt(jnp.mean(x * x, axis=-1, keepdims=True) + EPS) * g


def _ada_down_kernel(c_ref, w_ref, o_ref):
    c = c_ref[...]
    a = (c * jax.nn.sigmoid(c)).astype(BF16)
    o_ref[0] = jnp.dot(a, w_ref[0].astype(BF16), preferred_element_type=F32)


def _ada_up_kernel(t_ref, w_ref, b_ref, o_ref):
    t = t_ref[0].astype(BF16)
    o_ref[0] = jnp.dot(t, w_ref[0].astype(BF16), preferred_element_type=F32) + b_ref[0]


def _ada_mods(cond, w_down, w_up, b_up):
    depth, d, r = w_down.shape
    n_out = w_up.shape[2]
    rows = cond.shape[0]
    tn1 = min(r, 512)
    t = pl.pallas_call(
        _ada_down_kernel,
        grid=(depth, r // tn1),
        in_specs=[pl.BlockSpec((rows, d), lambda l, j: (0, 0)),
                  pl.BlockSpec((1, d, tn1), lambda l, j: (l, 0, j))],
        out_specs=pl.BlockSpec((1, rows, tn1), lambda l, j: (l, 0, j)),
        out_shape=jax.ShapeDtypeStruct((depth, rows, r), F32),
        compiler_params=_params("parallel", "parallel"),
        name="ada_down",
    )(cond, w_down)
    tn2 = min(n_out, 2048)
    return pl.pallas_call(
        _ada_up_kernel,
        grid=(depth, n_out // tn2),
        in_specs=[pl.BlockSpec((1, rows, r), lambda l, j: (l, 0, 0)),
                  pl.BlockSpec((1, r, tn2), lambda l, j: (l, 0, j)),
                  pl.BlockSpec((1, 1, tn2), lambda l, j: (l, 0, j))],
        out_specs=pl.BlockSpec((1, rows, tn2), lambda l, j: (l, 0, j)),
        out_shape=jax.ShapeDtypeStruct((depth, rows, n_out), F32),
        compiler_params=_params("parallel", "parallel"),
        name="ada_up",
    )(t, w_up, b_up.reshape(depth, 1, n_out))


def _norm_mod_kernel(x_ref, g_ref, sh_ref, sc_ref, o_ref):
    y = _rms(x_ref[...], g_ref[...])
    o_ref[...] = (y * (1.0 + sc_ref[0]) + sh_ref[0]).astype(o_ref.dtype)


def _norm_kernel(x_ref, g_ref, o_ref):
    o_ref[...] = _rms(x_ref[...], g_ref[...]).astype(o_ref.dtype)


def _seg_fn(tm, n_lat, n_batch):
    return lambda i: jnp.minimum((i * tm) // n_lat, n_batch)


def _norm_mod(x, g, mods, k_shift, k_scale, *, n_lat, n_batch, out_dtype, tm=256):
    t, d = x.shape
    seg = _seg_fn(tm, n_lat, n_batch)
    return pl.pallas_call(
        _norm_mod_kernel,
        grid=(t // tm,),
        in_specs=[pl.BlockSpec((tm, d), lambda i: (i, 0)),
                  pl.BlockSpec((1, d), lambda i: (0, 0)),
                  pl.BlockSpec((1, 1, d), lambda i: (seg(i) * N_MOD + k_shift, 0, 0)),
                  pl.BlockSpec((1, 1, d), lambda i: (seg(i) * N_MOD + k_scale, 0, 0))],
        out_specs=pl.BlockSpec((tm, d), lambda i: (i, 0)),
        out_shape=jax.ShapeDtypeStruct((t, d), out_dtype),
        compiler_params=_params("parallel"),
        name="norm_mod",
    )(x, g.reshape(1, d), mods, mods)


def _final_norm(x, g, rows, tm=256):
    d = x.shape[1]
    return pl.pallas_call(
        _norm_kernel,
        grid=(rows // tm,),
        in_specs=[pl.BlockSpec((tm, d), lambda i: (i, 0)),
                  pl.BlockSpec((1, d), lambda i: (0, 0))],
        out_specs=pl.BlockSpec((tm, d), lambda i: (i, 0)),
        out_shape=jax.ShapeDtypeStruct((rows, d), F32),
        compiler_params=_params("parallel"),
        name="final_norm",
    )(x, g.reshape(1, d))


def _mm_kernel(a_ref, b_ref, o_ref):
    o_ref[...] = jnp.dot(a_ref[...], b_ref[...],
                         preferred_element_type=F32).astype(o_ref.dtype)


def _mm_swiglu_kernel(a_ref, w1_ref, w3_ref, o_ref):
    a = a_ref[...]
    h1 = jnp.dot(a, w1_ref[...], preferred_element_type=F32)
    h3 = jnp.dot(a, w3_ref[...], preferred_element_type=F32)
    o_ref[...] = (h1 * jax.nn.sigmoid(h1) * h3).astype(o_ref.dtype)


def _mm_res_kernel(a_ref, b_ref, x_ref, g_ref, o_ref):
    acc = jnp.dot(a_ref[...], b_ref[...], preferred_element_type=F32)
    o_ref[...] = x_ref[...] + g_ref[0] * acc


def _w_spec(w, tn):
    return pl.BlockSpec((w.shape[0], tn), lambda i, j: (0, j))


def _matmul(a, w, *, out_dtype, tm=512, tn=1024):
    m, k = a.shape
    n = w.shape[-1]
    tn = min(tn, n)
    return pl.pallas_call(
        _mm_kernel,
        grid=(m // tm, n // tn),
        in_specs=[pl.BlockSpec((tm, k), lambda i, j: (i, 0)), _w_spec(w, tn)],
        out_specs=pl.BlockSpec((tm, tn), lambda i, j: (i, j)),
        out_shape=jax.ShapeDtypeStruct((m, n), out_dtype),
        compiler_params=_params("parallel", "arbitrary"),
        name="matmul",
    )(a, w)


def _matmul_swiglu(a, w1, w3, *, tm=512, tn=512):
    m, k = a.shape
    n = w1.shape[-1]
    tn = min(tn, n)
    return pl.pallas_call(
        _mm_swiglu_kernel,
        grid=(m // tm, n // tn),
        in_specs=[pl.BlockSpec((tm, k), lambda i, j: (i, 0)),
                  _w_spec(w1, tn), _w_spec(w3, tn)],
        out_specs=pl.BlockSpec((tm, tn), lambda i, j: (i, j)),
        out_shape=jax.ShapeDtypeStruct((m, n), BF16),
        compiler_params=_params("parallel", "arbitrary"),
        name="matmul_swiglu",
    )(a, w1, w3)


def _matmul_residual(a, w, x, mods, k_gate, *, n_lat, n_batch, tm=512, tn=512):
    m, k = a.shape
    n = w.shape[-1]
    tn = min(tn, n)
    seg = _seg_fn(tm, n_lat, n_batch)
    return pl.pallas_call(
        _mm_res_kernel,
        grid=(m // tm, n // tn),
        in_specs=[pl.BlockSpec((tm, k), lambda i, j: (i, 0)), _w_spec(w, tn),
                  pl.BlockSpec((tm, tn), lambda i, j: (i, j)),
                  pl.BlockSpec((1, 1, tn), lambda i, j: (seg(i) * N_MOD + k_gate, 0, j))],
        out_specs=pl.BlockSpec((tm, tn), lambda i, j: (i, j)),
        out_shape=jax.ShapeDtypeStruct((m, n), F32),
        compiler_params=_params("parallel", "arbitrary"),
        name="matmul_residual",
    )(a, w, x, mods)


def _sgu_kernel(u_ref, v_ref, gv_ref, ws_ref, bs_ref, ga_ref, o_ref, *, n_chunks, n_groups):
    for c in range(n_chunks):
        rows = slice(c * CHUNK, (c + 1) * CHUNK)
        u = jax.nn.gelu(u_ref[rows, :].astype(F32), approximate=True)
        v = jax.nn.gelu(v_ref[rows, :].astype(F32), approximate=True)
        vb = _rms(v, gv_ref[...]).astype(BF16)
        parts = [jnp.dot(ws_ref[g], vb[:, g * HEAD_DIM:(g + 1) * HEAD_DIM],
                         preferred_element_type=F32) for g in range(n_groups)]
        y = u * (jnp.concatenate(parts, axis=1) + bs_ref[...])
        o_ref[rows, :] = _rms(y, ga_ref[...]).astype(o_ref.dtype)


def _sgu(p, g_v, w_s, b_s, g_a, *, width, ta=256):
    t = p.shape[0]
    n_groups = width // HEAD_DIM
    bias =jnp.repeat(b_s.T, HEAD_DIM, axis=1)
    return pl.pallas_call(
        functools.partial(_sgu_kernel, n_chunks=ta // CHUNK, n_groups=n_groups),
        grid=(t // ta,),
        in_specs=[pl.BlockSpec((ta, width), lambda i: (i, 0)),
                  pl.BlockSpec((ta, width), lambda i: (i, 1)),
                  pl.BlockSpec((1, width), lambda i: (0, 0)),
                  pl.BlockSpec((n_groups, CHUNK, CHUNK), lambda i: (0, 0, 0)),
                  pl.BlockSpec((CHUNK, width), lambda i: (0, 0)),
                  pl.BlockSpec((1, width), lambda i: (0, 0))],
        out_specs=pl.BlockSpec((ta, width), lambda i: (i, 0)),
        out_shape=jax.ShapeDtypeStruct((t, width), BF16),
        compiler_params=_params("parallel"),
        name="sgu",
    )(p, p, g_v.reshape(1, width), w_s.astype(BF16), bias, g_a.reshape(1, width))


def _na_bias_tables(rpb, rows):
    n_blocks = rows // NA_Q_ROWS
    cols = np.arange(GRID_W)
    col_start = np.clip(cols - NA_COLS // 2, 0, GRID_W - NA_COLS)
    dc = cols[None, :] - cols[:, None]
    col_ok = (cols[None, :] >= col_start[:, None]) & (cols[None, :] < col_start[:, None] + NA_COLS)
    col_idx = np.where(col_ok, dc + NA_COLS - 1, 0)
    n_dr = 2 * NA_ROWS - 1
    t = jnp.where(col_ok[None, None], rpb[:, :, col_idx], MASK_VALUE)
    t = jnp.concatenate([t, jnp.full((rpb.shape[0], 1, GRID_W, GRID_W), MASK_VALUE, F32)], axis=1)
    slab = np.full((3, NA_Q_ROWS, NA_BAND_ROWS), n_dr, np.int32)
    rep = [0, min(1, n_blocks - 1), n_blocks - 1]
    for ty, blk in enumerate(rep):
        band0 = int(np.clip(blk * NA_Q_ROWS - NA_ROWS // 2, 0, rows - NA_BAND_ROWS))
        for rq in range(NA_Q_ROWS):
            r = blk * NA_Q_ROWS + rq
            start = int(np.clip(r - NA_ROWS // 2, 0, rows - NA_ROWS))
            for kr in range(NA_BAND_ROWS):
                ar = band0 + kr
                if start <= ar < start + NA_ROWS:
                    slab[ty, rq, kr] = ar - r + NA_ROWS - 1
    full = t[:, slab]
    full = full.transpose(0, 1, 2, 4, 3, 5)
    return full.reshape(rpb.shape[0], 3, NA_Q_ROWS * GRID_W, NA_BAND_ROWS * GRID_W)


def _na_kernel(q_ref, k_ref, v_ref, kc_ref, vc_ref, bias_ref, o_ref, *, rows, scale):
    i = pl.program_id(2)
    band0 = jnp.clip(i * NA_Q_ROWS - NA_ROWS // 2, 0, rows - NA_BAND_ROWS)
    start = pl.multiple_of(band0 * GRID_W, GRID_W)
    nb = NA_BAND_ROWS * GRID_W
    q = q_ref[...]
    kb = k_ref[pl.ds(start, nb), :]
    vb = v_ref[pl.ds(start, nb), :]
    dn = (((1,), (1,)), ((), ()))
    s_loc = lax.dot_general(q, kb, dn, preferred_element_type=F32) * scale + bias_ref[...]
    s_ctx = lax.dot_general(q, kc_ref[...], dn, preferred_element_type=F32) * scale
    m = jnp.maximum(jnp.max(s_loc, axis=1, keepdims=True), jnp.max(s_ctx, axis=1, keepdims=True))
    p_loc = jnp.exp(s_loc - m)
    p_ctx = jnp.exp(s_ctx - m)
    l = jnp.sum(p_loc, axis=1, keepdims=True) + jnp.sum(p_ctx, axis=1, keepdims=True)
    o = (jnp.dot(p_loc.astype(BF16), vb, preferred_element_type=F32)
         + jnp.dot(p_ctx.astype(BF16), vc_ref[...], preferred_element_type=F32))
    o_ref[...] = (o / l).astype(o_ref.dtype)


def _neighbourhood_attention(p, bias, *, n_batch, n_lat, n_ctx, q_col, k_col, v_col, n_heads):
    rows = n_lat // GRID_W
    tq = NA_Q_ROWS * GRID_W
    nbk = NA_BAND_ROWS * GRID_W
    n_blocks = n_lat // tq
    ctx_blk0 = (n_batch * n_lat) // n_ctx

    def btype(i):
        return jnp.where(i == 0, 0, jnp.where(i == n_blocks - 1, 2, 1))

    return pl.pallas_call(
        functools.partial(_na_kernel, rows=rows, scale=HEAD_DIM ** -0.5),
        grid=(n_heads, n_batch, n_blocks),
        in_specs=[pl.BlockSpec((tq, HEAD_DIM), lambda h, b, i: (b * n_blocks + i, q_col + h)),
                  pl.BlockSpec((n_lat, HEAD_DIM), lambda h, b, i: (b, k_col + h)),
                  pl.BlockSpec((n_lat, HEAD_DIM), lambda h, b, i: (b, v_col + h)),
                  pl.BlockSpec((n_ctx, HEAD_DIM), lambda h, b, i: (ctx_blk0 + b, k_col + h)),
                  pl.BlockSpec((n_ctx, HEAD_DIM), lambda h, b, i: (ctx_blk0 + b, v_col + h)),
                  pl.BlockSpec((None, None, tq, nbk), lambda h, b, i: (h, btype(i), 0, 0))],
        out_specs=pl.BlockSpec((tq, HEAD_DIM), lambda h, b, i: (b * n_blocks + i, h)),
        out_shape=jax.ShapeDtypeStruct((n_batch * n_lat, n_heads * HEAD_DIM), BF16),
        compiler_params=_params("parallel", "parallel", "arbitrary"),
        name="neighbourhood_attention",
    )(p, p, p, p, p, bias)


def _rope_tables(n_lat, n_ctx):
    t = jnp.arange(n_lat, dtype=jnp.int32)
    pos = jnp.stack([t // GRID_W, t % GRID_W], axis=-1).astype(F32)
    n_freq = HEAD_DIM // 4
    inv = 1.0 / (ROPE_THETA ** (jnp.arange(n_freq, dtype=F32) / n_freq))
    ang = pos[:, :, None] * inv
    cos, sin = jnp.cos(ang), jnp.sin(ang)
    cos_t = jnp.concatenate([cos[:, 0], cos[:, 0], cos[:, 1], cos[:, 1]], axis=-1)
    sin_t = jnp.concatenate([-sin[:, 0], sin[:, 0], -sin[:, 1], sin[:, 1]], axis=-1)
    cos_t = jnp.concatenate([cos_t, jnp.ones((n_ctx, HEAD_DIM), F32)], axis=0)
    sin_t = jnp.concatenate([sin_t, jnp.zeros((n_ctx, HEAD_DIM), F32)], axis=0)
    return cos_t, sin_t


def _qk_prep_kernel(q_ref, k_ref, v_ref, cos_ref, sin_ref, gq_ref, gk_ref,
                    qo_ref, ko_ref, vo_ref, *, n_q, n_kv):
    cos = cos_ref[...]
    sin = sin_ref[...]
    quarter = HEAD_DIM // 4
    lane = lax.broadcasted_iota(jnp.int32, cos.shape, 1)
    first_half = (lane % (2 * quarter)) < quarter

    def prep(x, g, out_scale):
        y = _rms(x.astype(F32), g)
        swapped = jnp.where(first_half,
                            pltpu.roll(y, HEAD_DIM - quarter, 1),
                            pltpu.roll(y, quarter, 1))
        return ((y * cos + swapped * sin) * out_scale).astype(BF16)

    for h in range(n_q):
        cs = slice(h * HEAD_DIM, (h + 1) * HEAD_DIM)
        qo_ref[:, cs] = prep(q_ref[:, cs], gq_ref[...], QK_PRESCALE)
    ones = jnp.ones((v_ref.shape[0], HEAD_DIM), BF16)
    for h in range(n_kv):
        cs = slice(h * HEAD_DIM, (h + 1) * HEAD_DIM)
        ko_ref[0, :, cs] = prep(k_ref[:, cs], gk_ref[...], 1.0)
        vo_ref[0, :, 2 * h * HEAD_DIM:(2 * h + 1) * HEAD_DIM] = v_ref[:, cs]
        vo_ref[0, :, (2 * h + 1) * HEAD_DIM:(2 * h + 2) * HEAD_DIM] = ones


def _qk_prep(p, cos_t, sin_t, g_q, g_k, *, n_batch, n_lat, n_ctx, q_off, n_q, k_off, v_off, n_kv):
    t = p.shape[0]
    tm = n_ctx
    lat_tiles = n_lat // tm
    n_lat_tiles = n_batch * lat_tiles

    def pos_blk(i):
        return jnp.where(i < n_lat_tiles, i % lat_tiles, lat_tiles)

    def kv_blk(i):
        c = i - n_lat_tiles
        return (jnp.where(i < n_lat_tiles, i // lat_tiles, c),
                jnp.where(i < n_lat_tiles, i % lat_tiles, lat_tiles), 0)

    wq, wkv = n_q * HEAD_DIM, n_kv * HEAD_DIM
    return pl.pallas_call(
        functools.partial(_qk_prep_kernel, n_q=n_q, n_kv=n_kv),
        grid=(t // tm,),
        in_specs=[pl.BlockSpec((tm, wq), lambda i: (i, q_off)),
                  pl.BlockSpec((tm, wkv), lambda i: (i, k_off)),
                  pl.BlockSpec((tm, wkv), lambda i: (i, v_off)),
                  pl.BlockSpec((tm, HEAD_DIM), lambda i: (pos_blk(i), 0)),
                  pl.BlockSpec((tm, HEAD_DIM), lambda i: (pos_blk(i), 0)),
                  pl.BlockSpec((1, HEAD_DIM), lambda i: (0, 0)),
                  pl.BlockSpec((1, HEAD_DIM), lambda i: (0, 0))],
        out_specs=[pl.BlockSpec((tm, wq), lambda i: (i, 0)),
                   pl.BlockSpec((1, tm, wkv), kv_blk),
                   pl.BlockSpec((1, tm, 2 * wkv), kv_blk)],
        out_shape=[jax.ShapeDtypeStruct((t, wq), BF16),
                   jax.ShapeDtypeStruct((n_batch, n_lat + n_ctx, wkv), BF16),
                   jax.ShapeDtypeStruct((n_batch, n_lat + n_ctx, 2 * wkv), BF16)],
        compiler_params=_params("parallel"),
        name="qk_prep",
    )(p, p, p, cos_t, sin_t, g_q.reshape(1, HEAD_DIM), g_k.reshape(1, HEAD_DIM))


def _flash_kernel(q_ref, k_ref, v_ref, o_ref, m_sc, l_sc, acc_sc, *, group, tq, scale):
    kj = pl.program_id(3)

    @pl.when(kj == 0)
    def _():
        m_sc[...] = jnp.full_like(m_sc, -jnp.inf)
        l_sc[...] = jnp.zeros_like(l_sc)
        acc_sc[...] = jnp.zeros_like(acc_sc)

    if group == 1:
        q = q_ref[...]
    else:
        q = jnp.concatenate([q_ref[:, g * HEAD_DIM:(g + 1) * HEAD_DIM] for g in range(group)], axis=0)
    s = lax.dot_general(q, k_ref[0], (((1,), (1,)), ((), ())), preferred_element_type=F32) * scale
    m_prev = m_sc[...]
    m_new = jnp.maximum(m_prev, jnp.max(s, axis=1, keepdims=True))
    alpha = jnp.exp(m_prev - m_new)
    p = jnp.exp(s - m_new)
    l_sc[...] = alpha * l_sc[...] + jnp.sum(p, axis=1, keepdims=True)
    acc_sc[...] = alpha * acc_sc[...] + jnp.dot(p.astype(BF16), v_ref[0], preferred_element_type=F32)
    m_sc[...] = m_new

    @pl.when(kj == pl.num_programs(3) - 1)
    def _():
        o = acc_sc[...] / l_sc[...]
        for g in range(group):
            o_ref[:, g * HEAD_DIM:(g + 1) * HEAD_DIM] = o[g * tq:(g + 1) * tq].astype(o_ref.dtype)


def _pick_tile(n, candidates):
    for c in candidates:
        if n % c == 0:
            return c
    raise ValueError(f"no tile for {n}")


def _flash(q2d, k3d, v3d, *, n_batch, n_q_rows, q_row0, q_rows_per_batch, q_col, n_kv_heads,
           group, kv_row0, n_kv_rows, tq=256):
    tq = min(tq, n_q_rows)
    tk = _pick_tile(n_kv_rows, (1024, 768, 512, 256))
    qb0, qbb = q_row0 // tq, q_rows_per_batch // tq
    kb0 = kv_row0 // tk
    gw = group * HEAD_DIM
    return pl.pallas_call(
        functools.partial(_flash_kernel, group=group, tq=tq, scale=HEAD_DIM ** -0.5),
        grid=(n_batch, n_kv_heads, n_q_rows // tq, n_kv_rows // tk),
        in_specs=[pl.BlockSpec((tq, gw), lambda b, h, i, j: (qb0 + b * qbb + i, q_col + h)),
                  pl.BlockSpec((1, tk, HEAD_DIM), lambda b, h, i, j: (b, kb0 + j, h)),
                  pl.BlockSpec((1, tk, HEAD_DIM), lambda b, h, i, j: (b, kb0 + j, h))],
        out_specs=pl.BlockSpec((tq, gw), lambda b, h, i, j: (b * (n_q_rows // tq) + i, h)),
        out_shape=jax.ShapeDtypeStruct((n_batch * n_q_rows, n_kv_heads * gw), BF16),
        scratch_shapes=[pltpu.VMEM((group * tq, 1), F32),
                        pltpu.VMEM((group * tq, 1), F32),
                        pltpu.VMEM((group * tq, HEAD_DIM), F32)],
        compiler_params=_params("parallel", "parallel", "parallel", "arbitrary"),
        name="flash_attention",
    )(q2d, k3d, v3d)


def _flash_gqa_kernel(q_ref, k_ref, v_ref, o_ref, m_sc, acc_sc, *, group, tq, tk, rc):
    kj = pl.program_id(3)

    @pl.when(kj == 0)
    def _():
        m_sc[...] = jnp.full_like(m_sc, -jnp.inf)
        acc_sc[...] = jnp.zeros_like(acc_sc)

    n_cols = tk // HEAD_DIM
    k = k_ref[0]
    v = v_ref[0]
    chunks = [(g, r0) for g in range(group) for r0 in range(0, tq, rc)]

    def scores(chunk):
        g, r0 = chunk
        q = q_ref[r0:r0 + rc, g * HEAD_DIM:(g + 1) * HEAD_DIM]
        return lax.dot_general(q, k, (((1,), (1,)), ((), ())), preferred_element_type=F32)

    pending = [scores(ch) for ch in chunks[:FLASH_LOOKAHEAD]]
    for ci, (g, r0) in enumerate(chunks):
        if ci + FLASH_LOOKAHEAD < len(chunks):
            pending.append(scores(chunks[ci + FLASH_LOOKAHEAD]))
        s = pending.pop(0)
        rows = slice(g * tq + r0, g * tq + r0 + rc)
        cols = [s[:, i * HEAD_DIM:(i + 1) * HEAD_DIM] for i in range(n_cols)]
        col_max = functools.reduce(jnp.maximum, cols)
        m_prev = m_sc[rows, :]
        m_new = jnp.maximum(m_prev, jnp.max(col_max, axis=1, keepdims=True))
        alpha = jnp.exp2(m_prev - m_new)
        p = jnp.concatenate([jnp.exp2(c - m_new).astype(BF16) for c in cols], axis=1)
        pv = jnp.dot(p, v, preferred_element_type=F32)
        acc_sc[rows, :] = jnp.concatenate([alpha, alpha], axis=1) * acc_sc[rows, :] + pv
        m_sc[rows, :] = m_new

    @pl.when(kj == pl.num_programs(3) - 1)
    def _():
        o = acc_sc[:, :HEAD_DIM] / acc_sc[:, HEAD_DIM:]
        for g in range(group):
            o_ref[:, g * HEAD_DIM:(g + 1) * HEAD_DIM] = o[g * tq:(g + 1) * tq].astype(o_ref.dtype)


def _flash_gqa(q2d, k3d, v3d, *, n_batch, n_q_rows, q_row0, q_rows_per_batch, n_kv_heads, group,
               kv_row0, n_kv_rows, tq=512, rc=256):
    tq = min(tq, n_q_rows)
    rc = min(rc, tq)
    tk = _pick_tile(n_kv_rows, (1408, 1024, 768, 512, 256))
    qb0, qbb = q_row0 // tq, q_rows_per_batch // tq
    kb0 = kv_row0 // tk
    gw = group * HEAD_DIM
    return pl.pallas_call(
        functools.partial(_flash_gqa_kernel, group=group, tq=tq, tk=tk, rc=rc),
        grid=(n_batch, n_kv_heads, n_q_rows // tq, n_kv_rows // tk),
        in_specs=[pl.BlockSpec((tq, gw), lambda b, h, i, j: (qb0 + b * qbb + i, h)),
                  pl.BlockSpec((1, tk, HEAD_DIM), lambda b, h, i, j: (b, kb0 + j, h)),
                  pl.BlockSpec((1, tk, 2 * HEAD_DIM), lambda b, h, i, j: (b, kb0 + j, h))],
        out_specs=pl.BlockSpec((tq, gw), lambda b, h, i, j: (b * (n_q_rows // tq) + i, h)),
        out_shape=jax.ShapeDtypeStruct((n_batch * n_q_rows, n_kv_heads * gw), BF16),
        scratch_shapes=[pltpu.VMEM((group * tq, HEAD_DIM), F32),
                        pltpu.VMEM((group * tq, 2 * HEAD_DIM), F32)],
        compiler_params=_params("parallel", "parallel", "parallel", "arbitrary"),
        name="flash_gqa",
    )(q2d, k3d, v3d)


def _merge_kernel(ya_ref, yb_ref, yc_ref, gb_ref, gc_ref, o_ref, *, wa, wb):
    o_ref[:, :wa] = ya_ref[...]
    o_ref[:, wa:wa + wb] = _rms(yb_ref[...].astype(F32), gb_ref[...]).astype(o_ref.dtype)
    o_ref[:, wa + wb:] = _rms(yc_ref[...].astype(F32), gc_ref[...]).astype(o_ref.dtype)


def _merge(ya, yb, yc, g_b, g_c, tm=256):
    t, wa = ya.shape
    wb, wc = yb.shape[1], yc.shape[1]
    return pl.pallas_call(
        functools.partial(_merge_kernel, wa=wa, wb=wb),
        grid=(t // tm,),
        in_specs=[pl.BlockSpec((tm, wa), lambda i: (i, 0)),
                  pl.BlockSpec((tm, wb), lambda i: (i, 0)),
                  pl.BlockSpec((tm, wc), lambda i: (i, 0)),
                  pl.BlockSpec((1, wb), lambda i: (0, 0)),
                  pl.BlockSpec((1, wc), lambda i: (0, 0))],
        out_specs=pl.BlockSpec((tm, wa + wb + wc), lambda i: (i, 0)),
        out_shape=jax.ShapeDtypeStruct((t, wa + wb + wc), BF16),
        compiler_params=_params("parallel"),
        name="merge_groups",
    )(ya, yb, yc, g_b.reshape(1, wb), g_c.reshape(1, wc))


def _router_kernel(h_ref, w_ref, b_ref, o_ref):
    logits = jnp.dot(h_ref[...].astype(BF16), w_ref[...], preferred_element_type=F32) + b_ref[...]
    lane = lax.broadcasted_iota(jnp.int32, logits.shape, 1).astype(F32)
    m1 = jnp.max(logits, axis=1, keepdims=True)
    i1 = jnp.min(jnp.where(logits == m1, lane, float(ROUTER_LANES)), axis=1, keepdims=True)
    rest = jnp.where(lane == i1, -jnp.inf, logits)
    m2 = jnp.max(rest, axis=1, keepdims=True)
    i2 = jnp.min(jnp.where(rest == m2, lane, float(ROUTER_LANES)), axis=1, keepdims=True)
    e2 = jnp.exp(m2 - m1)
    g1 = 1.0 / (1.0 + e2)
    o_ref[...] = jnp.where(lane == 0.0, i1,
                           jnp.where(lane == 1.0, i2,
                                     jnp.where(lane == 2.0, g1,
                                               jnp.where(lane == 3.0, e2 * g1, 0.0))))


def _route_plan(routes, n_experts, tm):
    t = routes.shape[0]
    a = 2 * t
    flat_e = routes[:, :2].astype(jnp.int32).reshape(a)
    onehot = (flat_e[:, None] == jnp.arange(n_experts, dtype=jnp.int32)[None, :]).astype(jnp.int32)
    csum = jnp.cumsum(onehot, axis=0)
    counts = csum[-1]
    rank = jnp.take_along_axis(csum, flat_e[:, None], axis=1)[:, 0] - 1
    padded = ((counts + tm - 1) // tm) * tm
    ends = jnp.cumsum(padded)
    starts = ends - padded
    dest = starts[flat_e] + rank
    ustarts = jnp.cumsum(counts) - counts
    order = jnp.argsort(flat_e, stable=True).astype(jnp.int32)
    n_rows = a + n_experts * tm
    row = jnp.arange(n_rows, dtype=jnp.int32)
    row_e = jnp.minimum(jnp.searchsorted(ends, row, side="right"), n_experts - 1).astype(jnp.int32)
    k = row - starts[row_e]
    src_a = order[jnp.clip(ustarts[row_e] + k, 0, a - 1)]
    src = jnp.where(k < counts[row_e], src_a // 2, 0).astype(jnp.int32)
    tile_start = jnp.arange(n_rows // tm, dtype=jnp.int32) * tm
    tile_valid = (tile_start < ends[-1]).astype(jnp.int32)
    tile_expert = row_e[jnp.minimum(tile_start, ends[-1] - 1)]
    return src, dest.astype(jnp.int32), tile_expert, tile_valid


def _gather_rows_kernel(src_ref, h_hbm, o_ref, buf, sem, *, tg):
    base = pl.program_id(0) * tg

    def copy(j):
        return pltpu.make_async_copy(h_hbm.at[pl.ds(src_ref[base + j], 1)], buf.at[pl.ds(j, 1)], sem)

    def issue(j, carry):
        copy(j).start()
        return carry

    def drain(j, carry):
        copy(j).wait()
        return carry

    lax.fori_loop(0, tg, issue, 0)
    lax.fori_loop(0, tg, drain, 0)
    o_ref[...] = buf[...].astype(o_ref.dtype)


def _gather_rows(h, src, *, out_dtype, tg=512):
    n = src.shape[0]
    d = h.shape[1]
    return pl.pallas_call(
        functools.partial(_gather_rows_kernel, tg=tg),
        grid_spec=pltpu.PrefetchScalarGridSpec(
            num_scalar_prefetch=1, grid=(n // tg,),
            in_specs=[pl.BlockSpec(memory_space=pl.ANY)],
            out_specs=pl.BlockSpec((tg, d), lambda i, s: (i, 0)),
            scratch_shapes=[pltpu.VMEM((tg, d), h.dtype), pltpu.SemaphoreType.DMA]),
        out_shape=jax.ShapeDtypeStruct((n, d), out_dtype),
        compiler_params=_params("arbitrary"),
        name="moe_gather",
    )(src, h)


def _moe_up_kernel(te_ref, tv_ref, x_ref, w1_ref, w3_ref, o_ref):
    del te_ref
    valid = tv_ref[pl.program_id(0)] == 1

    @pl.when(valid)
    def _():
        a = x_ref[...]
        h1 = jnp.dot(a, w1_ref[...], preferred_element_type=F32)
        h3 = jnp.dot(a, w3_ref[...], preferred_element_type=F32)
        o_ref[...] = (h1 * jax.nn.sigmoid(h1) * h3).astype(o_ref.dtype)

    @pl.when(jnp.logical_not(valid))
    def _():
        o_ref[...] = jnp.zeros_like(o_ref)


def _moe_down_kernel(te_ref, tv_ref, a_ref, w_ref, o_ref):
    del te_ref
    valid = tv_ref[pl.program_id(0)] == 1

    @pl.when(valid)
    def _():
        o_ref[...] = jnp.dot(a_ref[...], w_ref[...], preferred_element_type=F32)

    @pl.when(jnp.logical_not(valid))
    def _():
        o_ref[...] = jnp.zeros_like(o_ref)


def _moe_experts(xs, w1, w3, w2, tile_expert, tile_valid, *, tm, tn_up=512, tn_down=1024):
    rows, d = xs.shape
    f = w1.shape[-1]
    tn_up, tn_down = min(tn_up, f), min(tn_down, d)
    n_tiles = rows // tm
    hid = pl.pallas_call(
        _moe_up_kernel,
        grid_spec=pltpu.PrefetchScalarGridSpec(
            num_scalar_prefetch=2, grid=(n_tiles, f // tn_up),
            in_specs=[pl.BlockSpec((tm, d), lambda i, j, te, tv: (i, 0)),
                      pl.BlockSpec((None, d, tn_up), lambda i, j, te, tv: (te[i], 0, j)),
                      pl.BlockSpec((None, d, tn_up), lambda i, j, te, tv: (te[i], 0, j))],
            out_specs=pl.BlockSpec((tm, tn_up), lambda i, j, te, tv: (i, j))),
        out_shape=jax.ShapeDtypeStruct((rows, f), BF16),
        compiler_params=_params("parallel", "arbitrary"),
        name="moe_up",
    )(tile_expert, tile_valid, xs, w1, w3)
    return pl.pallas_call(
        _moe_down_kernel,
        grid_spec=pltpu.PrefetchScalarGridSpec(
            num_scalar_prefetch=2, grid=(n_tiles, d // tn_down),
            in_specs=[pl.BlockSpec((tm, f), lambda i, j, te, tv: (i, 0)),
                      pl.BlockSpec((None, f, tn_down), lambda i, j, te, tv: (te[i], 0, j))],
            out_specs=pl.BlockSpec((tm, tn_down), lambda i, j, te, tv: (i, j))),
        out_shape=jax.ShapeDtypeStruct((rows, d), F32),
        compiler_params=_params("parallel", "arbitrary"),
        name="moe_down",
    )(tile_expert, tile_valid, hid, w2)


def _moe_combine_kernel(dest_ref, x_ref, r_ref, g_ref, y_hbm, o_ref, ybuf, sem, *, tc):
    base = pl.program_id(0) * (2 * tc)

    def copy(j, slot):
        return pltpu.make_async_copy(y_hbm.at[pl.ds(dest_ref[base + 2 * j + slot], 1)],
                                     ybuf.at[slot, pl.ds(j, 1)], sem)

    def issue(j, carry):
        copy(j, 0).start()
        copy(j, 1).start()
        return carry

    def drain(j, carry):
        copy(j, 0).wait()
        copy(j, 1).wait()
        return carry

    lax.fori_loop(0, tc, issue, 0)
    lax.fori_loop(0, tc, drain, 0)
    r = r_ref[...]
    y = r[:, 2:3] * ybuf[0] + r[:, 3:4] * ybuf[1]
    o_ref[...] = x_ref[...] + g_ref[0] * y


def _moe_combine(x, y, routes, dest, mods, k_gate, *, n_lat, n_batch, tc=256):
    t, d = x.shape
    seg = _seg_fn(tc, n_lat, n_batch)
    return pl.pallas_call(
        functools.partial(_moe_combine_kernel, tc=tc),
        grid_spec=pltpu.PrefetchScalarGridSpec(
            num_scalar_prefetch=1, grid=(t // tc,),
            in_specs=[pl.BlockSpec((tc, d), lambda i, ds: (i, 0)),
                      pl.BlockSpec((tc, ROUTER_LANES), lambda i, ds: (i, 0)),
                      pl.BlockSpec((1, 1, d), lambda i, ds: (seg(i) * N_MOD + k_gate, 0, 0)),
                      pl.BlockSpec(memory_space=pl.ANY)],
            out_specs=pl.BlockSpec((tc, d), lambda i, ds: (i, 0)),
            scratch_shapes=[pltpu.VMEM((2, tc, d), F32), pltpu.SemaphoreType.DMA]),
        out_shape=jax.ShapeDtypeStruct((t, d), F32),
        compiler_params=_params("arbitrary"),
        name="moe_combine",
    )(dest, x, routes, mods, y)


def _router(h, w_r, b_r, tm=512):
    t, d = h.shape
    e = w_r.shape[1]
    w_pad = jnp.zeros((d, ROUTER_LANES), BF16).at[:, :e].set(w_r.astype(BF16))
    b_pad = jnp.full((1, ROUTER_LANES), -jnp.inf, F32).at[0, :e].set(b_r)
    return pl.pallas_call(
        _router_kernel,
        grid=(t // tm,),
        in_specs=[pl.BlockSpec((tm, d), lambda i: (i, 0)),
                  pl.BlockSpec((d, ROUTER_LANES), lambda i: (0, 0)),
                  pl.BlockSpec((1, ROUTER_LANES), lambda i: (0, 0))],
        out_specs=pl.BlockSpec((tm, ROUTER_LANES), lambda i: (i, 0)),
        out_shape=jax.ShapeDtypeStruct((t, ROUTER_LANES), F32),
        compiler_params=_params("parallel"),
        name="router",
    )(h, w_pad, b_pad)


def kernel(x, c, ctx, c_ctx, ada_down, ada_up, ada_bias, norm1_g, norm2_g, w_in, sgu_norm_g, sgu_w,
           sgu_b, na_rpb, qk_norm_g, group_norm_g, w_out, ffn_w1, ffn_w3, ffn_w2, moe_router,
           moe_router_b, moe_w1, moe_w3, moe_w2, final_norm_g):
    n_batch, n_lat, d = x.shape
    n_ctx = ctx.shape[1]
    depth = w_in.shape[0]
    mix = w_out.shape[1]
    wa = wb = mix // 4
    wc = mix - wa - wb
    n_b_heads = wb // HEAD_DIM
    n_c_heads = wc // HEAD_DIM
    n_kv_heads = n_c_heads // 4
    wkv = n_kv_heads * HEAD_DIM
    group = n_c_heads // n_kv_heads
    n_experts = moe_router.shape[-1]
    dims = dict(n_lat=n_lat, n_batch=n_batch)
    lat_rows = n_batch * n_lat

    off_bq, off_bk, off_bv = 2 * wa, 2 * wa + wb, 2 * wa + 2 * wb
    src_cq = 2 * wa + 3 * wb
    src_ck = src_cq + wc
    off_ck = src_cq
    off_cv = off_ck + wkv
    off_cq = off_cv + wkv

    def reorder_in(w):
        return jnp.concatenate([w[:, :src_cq], w[:, src_ck:], w[:, src_cq:src_ck]], axis=1).astype(BF16)

    xa = jnp.concatenate([x.reshape(lat_rows, d), ctx.reshape(n_batch * n_ctx, d)], axis=0)
    cond = jnp.zeros((8, d), F32).at[:n_batch].set(c).at[n_batch].set(c_ctx)
    mods_all = _ada_mods(cond, ada_down, ada_up, ada_bias)
    cos_t, sin_t = _rope_tables(n_lat, n_ctx)

    for l in range(depth):
        mods = mods_all[l].reshape(8 * N_MOD, 1, d)
        h = _norm_mod(xa, norm1_g[l], mods, 0, 1, out_dtype=BF16, **dims)
        p = _matmul(h, reorder_in(w_in[l]), out_dtype=BF16)
        ya = _sgu(p, sgu_norm_g[l], sgu_w[l], sgu_b[l], group_norm_g[l, :wa], width=wa)
        bias = _na_bias_tables(na_rpb[l], n_lat // GRID_W)
        yb_lat = _neighbourhood_attention(
            p, bias, n_batch=n_batch, n_lat=n_lat, n_ctx=n_ctx, q_col=off_bq // HEAD_DIM,
            k_col=off_bk // HEAD_DIM, v_col=off_bv // HEAD_DIM, n_heads=n_b_heads)
        kb_ctx = p[lat_rows:, off_bk:off_bk + wb].reshape(n_batch, n_ctx, wb)
        vb_ctx = p[lat_rows:, off_bv:off_bv + wb].reshape(n_batch, n_ctx, wb)
        yb_ctx = _flash(p, kb_ctx, vb_ctx, n_batch=n_batch, n_q_rows=n_ctx, q_row0=lat_rows,
                        q_rows_per_batch=n_ctx, q_col=off_bq // HEAD_DIM, n_kv_heads=n_b_heads,
                        group=1, kv_row0=0, n_kv_rows=n_ctx)
        qc, kc, vc = _qk_prep(p, cos_t, sin_t, qk_norm_g[l, 0], qk_norm_g[l, 1], n_batch=n_batch,
                              n_lat=n_lat, n_ctx=n_ctx, q_off=off_cq // wc, n_q=n_c_heads,
                              k_off=off_ck // wkv, v_off=off_cv // wkv, n_kv=n_kv_heads)
        yc_lat = _flash_gqa(qc, kc, vc, n_batch=n_batch, n_q_rows=n_lat, q_row0=0,
                            q_rows_per_batch=n_lat, n_kv_heads=n_kv_heads, group=group,
                            kv_row0=0, n_kv_rows=n_lat + n_ctx)
        yc_ctx = _flash_gqa(qc, kc, vc, n_batch=n_batch, n_q_rows=n_ctx, q_row0=lat_rows,
                            q_rows_per_batch=n_ctx, n_kv_heads=n_kv_heads, group=group,
                            kv_row0=n_lat, n_kv_rows=n_ctx)
        yb = jnp.concatenate([yb_lat, yb_ctx], axis=0)
        yc = jnp.concatenate([yc_lat, yc_ctx], axis=0)
        merged = _merge(ya, yb, yc, group_norm_g[l, wa:wa + wb], group_norm_g[l, wa + wb:])
        xa = _matmul_residual(merged, w_out[l].astype(BF16), xa, mods, 2, tn=1024, **dims)
        j = l // 2
        if l % 2 == 0:
            h2 = _norm_mod(xa, norm2_g[l], mods, 3, 4, out_dtype=BF16, **dims)
            hid = _matmul_swiglu(h2, ffn_w1[j].astype(BF16), ffn_w3[j].astype(BF16))
            xa = _matmul_residual(hid, ffn_w2[j].astype(BF16), xa, mods, 5, **dims)
        else:
            h2 = _norm_mod(xa, norm2_g[l], mods, 3, 4, out_dtype=F32, **dims)
            routes = _router(h2, moe_router[j], moe_router_b[j])
            src, dest, tile_expert, tile_valid = _route_plan(routes, n_experts, MOE_TILE)
            xs = _gather_rows(h2, src, out_dtype=BF16)
            ys = _moe_experts(xs, moe_w1[j].astype(BF16), moe_w3[j].astype(BF16),
                              moe_w2[j].astype(BF16), tile_expert, tile_valid, tm=MOE_TILE)
            xa = _moe_combine(xa, ys, routes, dest, mods, 5, **dims)

    out = _final_norm(xa, final_norm_g, lat_rows)
    return out.reshape(n_batch, n_lat, d)
```

```python
import functools
import math

import numpy as np
import jax
import jax.numpy as jnp
from jax import lax
from jax.experimental import pallas as pl
from jax.experimental.pallas import tpu as pltpu

F32 = jnp.float32
BF16 = jnp.bfloat16

HEAD_DIM = 128
GRID_W = 64
CHUNK = 128
NA_ROWS = 8
NA_COLS = 16
ROPE_THETA = 10000.0
EPS = 1e-6
N_MOD = 6
MASK_VALUE = -1e30
ROUTER_LANES = 128
MOE_TILE = 512
FLASH_LOOKAHEAD = 3
NA_Q_ROWS = 8
NA_BAND_ROWS = 16
LOG2E = 1.4426950408889634
QK_PRESCALE = HEAD_DIM ** -0.5 * LOG2E
V7X_VMEM_BYTES = 64 * 1024 * 1024
VMEM_LIMIT = V7X_VMEM_BYTES - 8 * 1024 * 1024


def _params(*sem):
    return pltpu.CompilerParams(dimension_semantics=sem, vmem_limit_bytes=VMEM_LIMIT)


def _rms(x, g):
    return x * lax.rsqrt(jnp.mean(x * x, axis=-1, keepdims=True) + EPS) * g


def _ada_down_kernel(c_ref, w_ref, o_ref):
    c = c_ref[...]
    a = (c * jax.nn.sigmoid(c)).astype(BF16)
    o_ref[0] = jnp.dot(a, w_ref[0].astype(BF16), preferred_element_type=F32)


def _ada_up_kernel(t_ref, w_ref, b_ref, o_ref):
    t = t_ref[0].astype(BF16)
    o_ref[0] = jnp.dot(t, w_ref[0].astype(BF16), preferred_element_type=F32) + b_ref[0]


def _ada_mods(cond, w_down, w_up, b_up):
    depth, d, r = w_down.shape
    n_out = w_up.shape[2]
    rows = cond.shape[0]
    tn1 = min(r, 512)
    t = pl.pallas_call(
        _ada_down_kernel,
        grid=(depth, r // tn1),
        in_specs=[pl.BlockSpec((rows, d), lambda l, j: (0, 0)),
                  pl.BlockSpec((1, d, tn1), lambda l, j: (l, 0, j))],
        out_specs=pl.BlockSpec((1, rows, tn1), lambda l, j: (l, 0, j)),
        out_shape=jax.ShapeDtypeStruct((depth, rows, r), F32),
        compiler_params=_params("parallel", "parallel"),
        name="ada_down",
    )(cond, w_down)
    tn2 = min(n_out, 2048)
    return pl.pallas_call(
        _ada_up_kernel,
        grid=(depth, n_out // tn2),
        in_specs=[pl.BlockSpec((1, rows, r), lambda l, j: (l, 0, 0)),
                  pl.BlockSpec((1, r, tn2), lambda l, j: (l, 0, j)),
                  pl.BlockSpec((1, 1, tn2), lambda l, j: (l, 0, j))],
        out_specs=pl.BlockSpec((1, rows, tn2), lambda l, j: (l, 0, j)),
        out_shape=jax.ShapeDtypeStruct((depth, rows, n_out), F32),
        compiler_params=_params("parallel", "parallel"),
        name="ada_up",
    )(t, w_up, b_up.reshape(depth, 1, n_out))


def _norm_mod_kernel(x_ref, g_ref, sh_ref, sc_ref, o_ref):
    y = _rms(x_ref[...], g_ref[...])
    o_ref[...] = (y * (1.0 + sc_ref[0]) + sh_ref[0]).astype(o_ref.dtype)


def _norm_kernel(x_ref, g_ref, o_ref):
    o_ref[...] = _rms(x_ref[...], g_ref[...]).astype(o_ref.dtype)


def _seg_fn(tm, n_lat, n_batch):
    return lambda i: jnp.minimum((i * tm) // n_lat, n_batch)


def _norm_mod(x, g, mods, k_shift, k_scale, *, n_lat, n_batch, out_dtype, tm=256):
    t, d = x.shape
    seg = _seg_fn(tm, n_lat, n_batch)
    return pl.pallas_call(
        _norm_mod_kernel,
        grid=(t // tm,),
        in_specs=[pl.BlockSpec((tm, d), lambda i: (i, 0)),
                  pl.BlockSpec((1, d), lambda i: (0, 0)),
                  pl.BlockSpec((1, 1, d), lambda i: (seg(i) * N_MOD + k_shift, 0, 0)),
                  pl.BlockSpec((1, 1, d), lambda i: (seg(i) * N_MOD + k_scale, 0, 0))],
        out_specs=pl.BlockSpec((tm, d), lambda i: (i, 0)),
        out_shape=jax.ShapeDtypeStruct((t, d), out_dtype),
        compiler_params=_params("parallel"),
        name="norm_mod",
    )(x, g.reshape(1, d), mods, mods)


def _final_norm(x, g, rows, tm=256):
    d = x.shape[1]
    return pl.pallas_call(
        _norm_kernel,
        grid=(rows // tm,),
        in_specs=[pl.BlockSpec((tm, d), lambda i: (i, 0)),
                  pl.BlockSpec((1, d), lambda i: (0, 0))],
        out_specs=pl.BlockSpec((tm, d), lambda i: (i, 0)),
        out_shape=jax.ShapeDtypeStruct((rows, d), F32),
        compiler_params=_params("parallel"),
        name="final_norm",
    )(x, g.reshape(1, d))


def _mm_kernel(a_ref, b_ref, o_ref):
    o_ref[...] = jnp.dot(a_ref[...], b_ref[...],
                         preferred_element_type=F32).astype(o_ref.dtype)


def _mm_swiglu_kernel(a_ref, w1_ref, w3_ref, o_ref):
    a = a_ref[...]
    h1 = jnp.dot(a, w1_ref[...], preferred_element_type=F32)
    h3 = jnp.dot(a, w3_ref[...], preferred_element_type=F32)
    o_ref[...] = (h1 * jax.nn.sigmoid(h1) * h3).astype(o_ref.dtype)


def _mm_res_kernel(a_ref, b_ref, x_ref, g_ref, o_ref):
    acc = jnp.dot(a_ref[...], b_ref[...], preferred_element_type=F32)
    o_ref[...] = x_ref[...] + g_ref[0] * acc


def _w_spec(w, tn):
    return pl.BlockSpec((w.shape[0], tn), lambda i, j: (0, j))


def _matmul(a, w, *, out_dtype, tm=512, tn=1024):
    m, k = a.shape
    n = w.shape[-1]
    tn = min(tn, n)
    return pl.pallas_call(
        _mm_kernel,
        grid=(m // tm, n // tn),
        in_specs=[pl.BlockSpec((tm, k), lambda i, j: (i, 0)), _w_spec(w, tn)],
        out_specs=pl.BlockSpec((tm, tn), lambda i, j: (i, j)),
        out_shape=jax.ShapeDtypeStruct((m, n), out_dtype),
        compiler_params=_params("parallel", "arbitrary"),
        name="matmul",
    )(a, w)


def _matmul_swiglu(a, w1, w3, *, tm=512, tn=512):
    m, k = a.shape
    n = w1.shape[-1]
    tn = min(tn, n)
    return pl.pallas_call(
        _mm_swiglu_kernel,
        grid=(m // tm, n // tn),
        in_specs=[pl.BlockSpec((tm, k), lambda i, j: (i, 0)),
                  _w_spec(w1, tn), _w_spec(w3, tn)],
        out_specs=pl.BlockSpec((tm, tn), lambda i, j: (i, j)),
        out_shape=jax.ShapeDtypeStruct((m, n), BF16),
        compiler_params=_params("parallel", "arbitrary"),
        name="matmul_swiglu",
    )(a, w1, w3)


def _matmul_residual(a, w, x, mods, k_gate, *, n_lat, n_batch, tm=512, tn=512):
    m, k = a.shape
    n = w.shape[-1]
    tn = min(tn, n)
    seg = _seg_fn(tm, n_lat, n_batch)
    return pl.pallas_call(
        _mm_res_kernel,
        grid=(m // tm, n // tn),
        in_specs=[pl.BlockSpec((tm, k), lambda i, j: (i, 0)), _w_spec(w, tn),
                  pl.BlockSpec((tm, tn), lambda i, j: (i, j)),
                  pl.BlockSpec((1, 1, tn), lambda i, j: (seg(i) * N_MOD + k_gate, 0, j))],
        out_specs=pl.BlockSpec((tm, tn), lambda i, j: (i, j)),
        out_shape=jax.ShapeDtypeStruct((m, n), F32),
        compiler_params=_params("parallel", "arbitrary"),
        name="matmul_residual",
    )(a, w, x, mods)


def _sgu_kernel(u_ref, v_ref, gv_ref, ws_ref, bs_ref, ga_ref, o_ref, *, n_chunks, n_groups):
    for c in range(n_chunks):
        rows = slice(c * CHUNK, (c + 1) * CHUNK)
        u = jax.nn.gelu(u_ref[rows, :].astype(F32), approximate=True)
        v = jax.nn.gelu(v_ref[rows, :].astype(F32), approximate=True)
        vb = _rms(v, gv_ref[...]).astype(BF16)
        parts = [jnp.dot(ws_ref[g], vb[:, g * HEAD_DIM:(g + 1) * HEAD_DIM],
                         preferred_element_type=F32) for g in range(n_groups)]
        y = u * (jnp.concatenate(parts, axis=1) + bs_ref[...])
        o_ref[rows, :] = _rms(y, ga_ref[...]).astype(o_ref.dtype)


def _sgu(p, g_v, w_s, b_s, g_a, *, width, ta=256):
    t = p.shape[0]
    n_groups = width // HEAD_DIM
    bias =jnp.repeat(b_s.T, HEAD_DIM, axis=1)
    return pl.pallas_call(
        functools.partial(_sgu_kernel, n_chunks=ta // CHUNK, n_groups=n_groups),
        grid=(t // ta,),
        in_specs=[pl.BlockSpec((ta, width), lambda i: (i, 0)),
                  pl.BlockSpec((ta, width), lambda i: (i, 1)),
                  pl.BlockSpec((1, width), lambda i: (0, 0)),
                  pl.BlockSpec((n_groups, CHUNK, CHUNK), lambda i: (0, 0, 0)),
                  pl.BlockSpec((CHUNK, width), lambda i: (0, 0)),
                  pl.BlockSpec((1, width), lambda i: (0, 0))],
        out_specs=pl.BlockSpec((ta, width), lambda i: (i, 0)),
        out_shape=jax.ShapeDtypeStruct((t, width), BF16),
        compiler_params=_params("parallel"),
        name="sgu",
    )(p, p, g_v.reshape(1, width), w_s.astype(BF16), bias, g_a.reshape(1, width))


def _na_bias_tables(rpb, rows):
    n_blocks = rows // NA_Q_ROWS
    cols = np.arange(GRID_W)
    col_start = np.clip(cols - NA_COLS // 2, 0, GRID_W - NA_COLS)
    dc = cols[None, :] - cols[:, None]
    col_ok = (cols[None, :] >= col_start[:, None]) & (cols[None, :] < col_start[:, None] + NA_COLS)
    col_idx = np.where(col_ok, dc + NA_COLS - 1, 0)
    n_dr = 2 * NA_ROWS - 1
    t = jnp.where(col_ok[None, None], rpb[:, :, col_idx] * LOG2E, MASK_VALUE)
    t = jnp.concatenate([t, jnp.full((rpb.shape[0], 1, GRID_W, GRID_W), MASK_VALUE, F32)], axis=1)
    slab = np.full((3, NA_Q_ROWS, NA_BAND_ROWS), n_dr, np.int32)
    rep = [0, min(1, n_blocks - 1), n_blocks - 1]
    for ty, blk in enumerate(rep):
        band0 = int(np.clip(blk * NA_Q_ROWS - NA_ROWS // 2, 0, rows - NA_BAND_ROWS))
        for rq in range(NA_Q_ROWS):
            r = blk * NA_Q_ROWS + rq
            start = int(np.clip(r - NA_ROWS // 2, 0, rows - NA_ROWS))
            for kr in range(NA_BAND_ROWS):
                ar = band0 + kr
                if start <= ar < start + NA_ROWS:
                    slab[ty, rq, kr] = ar - r + NA_ROWS - 1
    full = t[:, slab]
    full = full.transpose(0, 1, 2, 4, 3, 5)
    return full.reshape(rpb.shape[0], 3, NA_Q_ROWS * GRID_W, NA_BAND_ROWS * GRID_W)


def _na_kernel(q_ref, k_ref, v_ref, kc_ref, vc_ref, bias_ref, o_ref, *, rows, rc):
    i = pl.program_id(2)
    band0 = jnp.clip(i * NA_Q_ROWS - NA_ROWS // 2, 0, rows - NA_BAND_ROWS)
    start = pl.multiple_of(band0 * GRID_W, GRID_W)
    nb = NA_BAND_ROWS * GRID_W
    kb = k_ref[pl.ds(start, nb), :]
    vb = v_ref[pl.ds(start, nb), :]
    kc = kc_ref[...]
    vc = vc_ref[...]
    dn = (((1,), (1,)), ((), ()))
    n_chunks = q_ref.shape[0] // rc
    n_loc = nb // HEAD_DIM

    def scores(c):
        q = q_ref[c * rc:(c + 1) * rc, :]
        return (lax.dot_general(q, kb, dn, preferred_element_type=F32),
                lax.dot_general(q, kc, dn, preferred_element_type=F32))

    nxt = scores(0)
    for c in range(n_chunks):
        s_loc, s_ctx = nxt
        if c + 1 < n_chunks:
            nxt = scores(c + 1)
        s_loc = s_loc + bias_ref[c * rc:(c + 1) * rc, :]
        cols = ([s_loc[:, j * HEAD_DIM:(j + 1) * HEAD_DIM] for j in range(n_loc)]
                + [s_ctx[:, j * HEAD_DIM:(j + 1) * HEAD_DIM] for j in range(s_ctx.shape[1] // HEAD_DIM)])
        m = jnp.max(functools.reduce(jnp.maximum, cols), axis=1, keepdims=True)
        ps = [jnp.exp2(x - m) for x in cols]
        l = jnp.sum(functools.reduce(jnp.add, ps), axis=1, keepdims=True)
        p_loc = jnp.concatenate([x.astype(BF16) for x in ps[:n_loc]], axis=1)
        p_ctx = jnp.concatenate([x.astype(BF16) for x in ps[n_loc:]], axis=1)
        o = (jnp.dot(p_loc, vb, preferred_element_type=F32)
             + jnp.dot(p_ctx, vc, preferred_element_type=F32))
        o_ref[c * rc:(c + 1) * rc, :] = (o / l).astype(o_ref.dtype)


def _neighbourhood_attention(p, bias, *, n_batch, n_lat, n_ctx, q_col, k_col, v_col, n_heads):
    rows = n_lat // GRID_W
    tq = NA_Q_ROWS * GRID_W
    nbk = NA_BAND_ROWS * GRID_W
    n_blocks = n_lat // tq
    ctx_blk0 = (n_batch * n_lat) // n_ctx

    def btype(i):
        return jnp.where(i == 0, 0, jnp.where(i == n_blocks - 1, 2, 1))

    return pl.pallas_call(
        functools.partial(_na_kernel, rows=rows, rc=min(256, tq)),
        grid=(n_heads, n_batch, n_blocks),
        in_specs=[pl.BlockSpec((tq, HEAD_DIM), lambda h, b, i: (b * n_blocks + i, q_col + h)),
                  pl.BlockSpec((n_lat, HEAD_DIM), lambda h, b, i: (b, k_col + h)),
                  pl.BlockSpec((n_lat, HEAD_DIM), lambda h, b, i: (b, v_col + h)),
                  pl.BlockSpec((n_ctx, HEAD_DIM), lambda h, b, i: (ctx_blk0 + b, k_col + h)),
                  pl.BlockSpec((n_ctx, HEAD_DIM), lambda h, b, i: (ctx_blk0 + b, v_col + h)),
                  pl.BlockSpec((None, None, tq, nbk), lambda h, b, i: (h, btype(i), 0, 0))],
        out_specs=pl.BlockSpec((tq, HEAD_DIM), lambda h, b, i: (b * n_blocks + i, h)),
        out_shape=jax.ShapeDtypeStruct((n_batch * n_lat, n_heads * HEAD_DIM), BF16),
        compiler_params=_params("parallel", "parallel", "arbitrary"),
        name="neighbourhood_attention",
    )(p, p, p, p, p, bias)


def _rope_tables(n_lat, n_ctx):
    t = jnp.arange(n_lat, dtype=jnp.int32)
    pos = jnp.stack([t // GRID_W, t % GRID_W], axis=-1).astype(F32)
    n_freq = HEAD_DIM // 4
    inv = 1.0 / (ROPE_THETA ** (jnp.arange(n_freq, dtype=F32) / n_freq))
    ang = pos[:, :, None] * inv
    cos, sin = jnp.cos(ang), jnp.sin(ang)
    cos_t = jnp.concatenate([cos[:, 0], cos[:, 0], cos[:, 1], cos[:, 1]], axis=-1)
    sin_t = jnp.concatenate([-sin[:, 0], sin[:, 0], -sin[:, 1], sin[:, 1]], axis=-1)
    cos_t = jnp.concatenate([cos_t, jnp.ones((n_ctx, HEAD_DIM), F32)], axis=0)
    sin_t = jnp.concatenate([sin_t, jnp.zeros((n_ctx, HEAD_DIM), F32)], axis=0)
    return cos_t, sin_t


def _qk_prep_kernel(q_ref, k_ref, v_ref, cos_ref, sin_ref, gq_ref, gk_ref,
                    qo_ref, ko_ref, vo_ref, *, n_q, n_kv):
    cos = cos_ref[...]
    sin = sin_ref[...]
    quarter = HEAD_DIM // 4
    lane = lax.broadcasted_iota(jnp.int32, cos.shape, 1)
    first_half = (lane % (2 * quarter)) < quarter

    def prep(x, g, out_scale):
        y = _rms(x.astype(F32), g)
        swapped = jnp.where(first_half,
                            pltpu.roll(y, HEAD_DIM - quarter, 1),
                            pltpu.roll(y, quarter, 1))
        return ((y * cos + swapped * sin) * out_scale).astype(BF16)

    for h in range(n_q):
        cs = slice(h * HEAD_DIM, (h + 1) * HEAD_DIM)
        qo_ref[:, cs] = prep(q_ref[:, cs], gq_ref[...], QK_PRESCALE)
    ones = jnp.ones((v_ref.shape[0], HEAD_DIM), BF16)
    for h in range(n_kv):
        cs = slice(h * HEAD_DIM, (h + 1) * HEAD_DIM)
        ko_ref[0, :, cs] = prep(k_ref[:, cs], gk_ref[...], 1.0)
        vo_ref[0, :, 2 * h * HEAD_DIM:(2 * h + 1) * HEAD_DIM] = v_ref[:, cs]
        vo_ref[0, :, (2 * h + 1) * HEAD_DIM:(2 * h + 2) * HEAD_DIM] = ones


def _qk_prep(p, cos_t, sin_t, g_q, g_k, *, n_batch, n_lat, n_ctx, q_off, n_q, k_off, v_off, n_kv):
    t = p.shape[0]
    tm = n_ctx
    lat_tiles = n_lat // tm
    n_lat_tiles = n_batch * lat_tiles

    def pos_blk(i):
        return jnp.where(i < n_lat_tiles, i % lat_tiles, lat_tiles)

    def kv_blk(i):
        c = i - n_lat_tiles
        return (jnp.where(i < n_lat_tiles, i // lat_tiles, c),
                jnp.where(i < n_lat_tiles, i % lat_tiles, lat_tiles), 0)

    wq, wkv = n_q * HEAD_DIM, n_kv * HEAD_DIM
    return pl.pallas_call(
        functools.partial(_qk_prep_kernel, n_q=n_q, n_kv=n_kv),
        grid=(t // tm,),
        in_specs=[pl.BlockSpec((tm, wq), lambda i: (i, q_off)),
                  pl.BlockSpec((tm, wkv), lambda i: (i, k_off)),
                  pl.BlockSpec((tm, wkv), lambda i: (i, v_off)),
                  pl.BlockSpec((tm, HEAD_DIM), lambda i: (pos_blk(i), 0)),
                  pl.BlockSpec((tm, HEAD_DIM), lambda i: (pos_blk(i), 0)),
                  pl.BlockSpec((1, HEAD_DIM), lambda i: (0, 0)),
                  pl.BlockSpec((1, HEAD_DIM), lambda i: (0, 0))],
        out_specs=[pl.BlockSpec((tm, wq), lambda i: (i, 0)),
                   pl.BlockSpec((1, tm, wkv), kv_blk),
                   pl.BlockSpec((1, tm, 2 * wkv), kv_blk)],
        out_shape=[jax.ShapeDtypeStruct((t, wq), BF16),
                   jax.ShapeDtypeStruct((n_batch, n_lat + n_ctx, wkv), BF16),
                   jax.ShapeDtypeStruct((n_batch, n_lat + n_ctx, 2 * wkv), BF16)],
        compiler_params=_params("parallel"),
        name="qk_prep",
    )(p, p, p, cos_t, sin_t, g_q.reshape(1, HEAD_DIM), g_k.reshape(1, HEAD_DIM))


def _flash_kernel(q_ref, k_ref, v_ref, o_ref, m_sc, l_sc, acc_sc, *, group, tq):
    kj = pl.program_id(3)

    @pl.when(kj == 0)
    def _():
        m_sc[...] = jnp.full_like(m_sc, -jnp.inf)
        l_sc[...] = jnp.zeros_like(l_sc)
        acc_sc[...] = jnp.zeros_like(acc_sc)

    if group == 1:
        q = q_ref[...]
    else:
        q = jnp.concatenate([q_ref[:, g * HEAD_DIM:(g + 1) * HEAD_DIM] for g in range(group)], axis=0)
    s = lax.dot_general(q, k_ref[0], (((1,), (1,)), ((), ())), preferred_element_type=F32)
    m_prev = m_sc[...]
    m_new = jnp.maximum(m_prev, jnp.max(s, axis=1, keepdims=True))
    alpha = jnp.exp2(m_prev - m_new)
    p = jnp.exp2(s - m_new)
    l_sc[...] = alpha * l_sc[...] + jnp.sum(p, axis=1, keepdims=True)
    acc_sc[...] = alpha * acc_sc[...] + jnp.dot(p.astype(BF16), v_ref[0], preferred_element_type=F32)
    m_sc[...] = m_new

    @pl.when(kj == pl.num_programs(3) - 1)
    def _():
        o = acc_sc[...] / l_sc[...]
        for g in range(group):
            o_ref[:, g * HEAD_DIM:(g + 1) * HEAD_DIM] = o[g * tq:(g + 1) * tq].astype(o_ref.dtype)


def _pick_tile(n, candidates):
    for c in candidates:
        if n % c == 0:
            return c
    raise ValueError(f"no tile for {n}")


def _flash(q2d, k3d, v3d, *, n_batch, n_q_rows, q_row0, q_rows_per_batch, q_col, n_kv_heads,
           group, kv_row0, n_kv_rows, tq=256):
    tq = min(tq, n_q_rows)
    tk = _pick_tile(n_kv_rows, (1024, 768, 512, 256))
    qb0, qbb = q_row0 // tq, q_rows_per_batch // tq
    kb0 = kv_row0 // tk
    gw = group * HEAD_DIM
    return pl.pallas_call(
        functools.partial(_flash_kernel, group=group, tq=tq),
        grid=(n_batch, n_kv_heads, n_q_rows // tq, n_kv_rows // tk),
        in_specs=[pl.BlockSpec((tq, gw), lambda b, h, i, j: (qb0 + b * qbb + i, q_col + h)),
                  pl.BlockSpec((1, tk, HEAD_DIM), lambda b, h, i, j: (b, kb0 + j, h)),
                  pl.BlockSpec((1, tk, HEAD_DIM), lambda b, h, i, j: (b, kb0 + j, h))],
        out_specs=pl.BlockSpec((tq, gw), lambda b, h, i, j: (b * (n_q_rows // tq) + i, h)),
        out_shape=jax.ShapeDtypeStruct((n_batch * n_q_rows, n_kv_heads * gw), BF16),
        scratch_shapes=[pltpu.VMEM((group * tq, 1), F32),
                        pltpu.VMEM((group * tq, 1), F32),
                        pltpu.VMEM((group * tq, HEAD_DIM), F32)],
        compiler_params=_params("parallel", "parallel", "parallel", "arbitrary"),
        name="flash_attention",
    )(q2d, k3d, v3d)


def _flash_gqa_kernel(q_ref, k_ref, v_ref, o_ref, m_sc, acc_sc, *, group, tq, tk, rc):
    kj = pl.program_id(3)

    @pl.when(kj == 0)
    def _():
        m_sc[...] = jnp.full_like(m_sc, -jnp.inf)
        acc_sc[...] = jnp.zeros_like(acc_sc)

    n_cols = tk // HEAD_DIM
    k = k_ref[0]
    v = v_ref[0]
    chunks = [(g, r0) for g in range(group) for r0 in range(0, tq, rc)]

    def scores(chunk):
        g, r0 = chunk
        q = q_ref[r0:r0 + rc, g * HEAD_DIM:(g + 1) * HEAD_DIM]
        return lax.dot_general(q, k, (((1,), (1,)), ((), ())), preferred_element_type=F32)

    pending = [scores(ch) for ch in chunks[:FLASH_LOOKAHEAD]]
    for ci, (g, r0) in enumerate(chunks):
        if ci + FLASH_LOOKAHEAD < len(chunks):
            pending.append(scores(chunks[ci + FLASH_LOOKAHEAD]))
        s = pending.pop(0)
        rows = slice(g * tq + r0, g * tq + r0 + rc)
        cols = [s[:, i * HEAD_DIM:(i + 1) * HEAD_DIM] for i in range(n_cols)]
        col_max = functools.reduce(jnp.maximum, cols)
        m_prev = m_sc[rows, :]
        m_new = jnp.maximum(m_prev, jnp.max(col_max, axis=1, keepdims=True))
        alpha = jnp.exp2(m_prev - m_new)
        p = jnp.concatenate([jnp.exp2(c - m_new).astype(BF16) for c in cols], axis=1)
        pv = jnp.dot(p, v, preferred_element_type=F32)
        acc_sc[rows, :] = jnp.concatenate([alpha, alpha], axis=1) * acc_sc[rows, :] + pv
        m_sc[rows, :] = m_new

    @pl.when(kj == pl.num_programs(3) - 1)
    def _():
        o = acc_sc[:, :HEAD_DIM] / acc_sc[:, HEAD_DIM:]
        for g in range(group):
            o_ref[:, g * HEAD_DIM:(g + 1) * HEAD_DIM] = o[g * tq:(g + 1) * tq].astype(o_ref.dtype)


def _flash_gqa(q2d, k3d, v3d, *, n_batch, n_q_rows, q_row0, q_rows_per_batch, n_kv_heads, group,
               kv_row0, n_kv_rows, tq=512, rc=256):
    tq = min(tq, n_q_rows)
    rc = min(rc, tq)
    tk = _pick_tile(n_kv_rows, (1408, 1024, 768, 512, 256))
    qb0, qbb = q_row0 // tq, q_rows_per_batch // tq
    kb0 = kv_row0 // tk
    gw = group * HEAD_DIM
    return pl.pallas_call(
        functools.partial(_flash_gqa_kernel, group=group, tq=tq, tk=tk, rc=rc),
        grid=(n_batch, n_kv_heads, n_q_rows // tq, n_kv_rows // tk),
        in_specs=[pl.BlockSpec((tq, gw), lambda b, h, i, j: (qb0 + b * qbb + i, h)),
                  pl.BlockSpec((1, tk, HEAD_DIM), lambda b, h, i, j: (b, kb0 + j, h)),
                  pl.BlockSpec((1, tk, 2 * HEAD_DIM), lambda b, h, i, j: (b, kb0 + j, h))],
        out_specs=pl.BlockSpec((tq, gw), lambda b, h, i, j: (b * (n_q_rows // tq) + i, h)),
        out_shape=jax.ShapeDtypeStruct((n_batch * n_q_rows, n_kv_heads * gw), BF16),
        scratch_shapes=[pltpu.VMEM((group * tq, HEAD_DIM), F32),
                        pltpu.VMEM((group * tq, 2 * HEAD_DIM), F32)],
        compiler_params=_params("parallel", "parallel", "parallel", "arbitrary"),
        name="flash_gqa",
    )(q2d, k3d, v3d)


def _merge_kernel(ya_ref, yb_lat_ref, yb_ctx_ref, yc_lat_ref, yc_ctx_ref, gb_ref, gc_ref, o_ref,
                  *, wa, wb, n_lat_tiles):
    o_ref[:, :wa] = ya_ref[...]

    def emit(yb_ref, yc_ref):
        o_ref[:, wa:wa + wb] = _rms(yb_ref[...].astype(F32), gb_ref[...]).astype(o_ref.dtype)
        o_ref[:, wa + wb:] = _rms(yc_ref[...].astype(F32), gc_ref[...]).astype(o_ref.dtype)

    is_lat = pl.program_id(0) < n_lat_tiles
    pl.when(is_lat)(lambda: emit(yb_lat_ref, yc_lat_ref))
    pl.when(jnp.logical_not(is_lat))(lambda: emit(yb_ctx_ref, yc_ctx_ref))


def _merge(ya, yb_lat, yb_ctx, yc_lat, yc_ctx, g_b, g_c, tm=256):
    t, wa = ya.shape
    wb, wc = yb_lat.shape[1], yc_lat.shape[1]
    n_lat_tiles = yb_lat.shape[0] // tm

    def lat(i):
        return (jnp.minimum(i, n_lat_tiles - 1), 0)

    def ctx(i):
        return (jnp.maximum(i - n_lat_tiles, 0), 0)

    return pl.pallas_call(
        functools.partial(_merge_kernel, wa=wa, wb=wb, n_lat_tiles=n_lat_tiles),
        grid=(t // tm,),
        in_specs=[pl.BlockSpec((tm, wa), lambda i: (i, 0)),
                  pl.BlockSpec((tm, wb), lat), pl.BlockSpec((tm, wb), ctx),
                  pl.BlockSpec((tm, wc), lat), pl.BlockSpec((tm, wc), ctx),
                  pl.BlockSpec((1, wb), lambda i: (0, 0)),
                  pl.BlockSpec((1, wc), lambda i: (0, 0))],
        out_specs=pl.BlockSpec((tm, wa + wb + wc), lambda i: (i, 0)),
        out_shape=jax.ShapeDtypeStruct((t, wa + wb + wc), BF16),
        compiler_params=_params("parallel"),
        name="merge_groups",
    )(ya, yb_lat, yb_ctx, yc_lat, yc_ctx, g_b.reshape(1, wb), g_c.reshape(1, wc))


def _router_kernel(h_ref, w_ref, b_ref, o_ref):
    logits = jnp.dot(h_ref[...].astype(BF16), w_ref[...], preferred_element_type=F32) + b_ref[...]
    lane = lax.broadcasted_iota(jnp.int32, logits.shape, 1).astype(F32)
    m1 = jnp.max(logits, axis=1, keepdims=True)
    i1 = jnp.min(jnp.where(logits == m1, lane, float(ROUTER_LANES)), axis=1, keepdims=True)
    rest = jnp.where(lane == i1, -jnp.inf, logits)
    m2 = jnp.max(rest, axis=1, keepdims=True)
    i2 = jnp.min(jnp.where(rest == m2, lane, float(ROUTER_LANES)), axis=1, keepdims=True)
    e2 = jnp.exp(m2 - m1)
    g1 = 1.0 / (1.0 + e2)
    o_ref[...] = jnp.where(lane == 0.0, i1,
                           jnp.where(lane == 1.0, i2,
                                     jnp.where(lane == 2.0, g1,
                                               jnp.where(lane == 3.0, e2 * g1, 0.0))))


def _route_plan(routes, n_experts, tm):
    t = routes.shape[0]
    a = 2 * t
    flat_e = routes[:, :2].astype(jnp.int32).reshape(a)
    onehot = (flat_e[:, None] == jnp.arange(n_experts, dtype=jnp.int32)[None, :]).astype(jnp.int32)
    csum = jnp.cumsum(onehot, axis=0)
    counts = csum[-1]
    rank = jnp.take_along_axis(csum, flat_e[:, None], axis=1)[:, 0] - 1
    padded = ((counts + tm - 1) // tm) * tm
    ends = jnp.cumsum(padded)
    starts = ends - padded
    dest = starts[flat_e] + rank
    ustarts = jnp.cumsum(counts) - counts
    order = jnp.argsort(flat_e, stable=True).astype(jnp.int32)
    n_rows = a + n_experts * tm
    row = jnp.arange(n_rows, dtype=jnp.int32)
    row_e = jnp.minimum(jnp.searchsorted(ends, row, side="right"), n_experts - 1).astype(jnp.int32)
    k = row - starts[row_e]
    src_a = order[jnp.clip(ustarts[row_e] + k, 0, a - 1)]
    src = jnp.where(k < counts[row_e], src_a // 2, 0).astype(jnp.int32)
    tile_start = jnp.arange(n_rows // tm, dtype=jnp.int32) * tm
    tile_valid = (tile_start < ends[-1]).astype(jnp.int32)
    tile_expert = row_e[jnp.minimum(tile_start, ends[-1] - 1)]
    return src, dest.astype(jnp.int32), tile_expert, tile_valid


def _gather_rows_kernel(src_ref, h_hbm, o_ref, buf, sem, *, tg):
    i = pl.program_id(0)

    def copy(tile, j):
        slot = tile % 2
        return pltpu.make_async_copy(h_hbm.at[pl.ds(src_ref[tile * tg + j], 1)],
                                     buf.at[slot, pl.ds(j, 1)], sem.at[slot])

    def issue(tile):
        def body(j, carry):
            copy(tile, j).start()
            return carry
        lax.fori_loop(0, tg, body, 0, unroll=8)

    pl.when(i == 0)(lambda: issue(0))
    pl.when(i + 1 < pl.num_programs(0))(lambda: issue(i + 1))

    slot = i % 2
    pltpu.make_async_copy(h_hbm.at[pl.ds(0, tg)], buf.at[slot], sem.at[slot]).wait()
    o_ref[...] = buf[slot].astype(o_ref.dtype)


def _gather_rows(h, src, *, out_dtype, tg=512):
    n = src.shape[0]
    d = h.shape[1]
    return pl.pallas_call(
        functools.partial(_gather_rows_kernel, tg=tg),
        grid_spec=pltpu.PrefetchScalarGridSpec(
            num_scalar_prefetch=1, grid=(n // tg,),
            in_specs=[pl.BlockSpec(memory_space=pl.ANY)],
            out_specs=pl.BlockSpec((tg, d), lambda i, s: (i, 0)),
            scratch_shapes=[pltpu.VMEM((2, tg, d), h.dtype), pltpu.SemaphoreType.DMA((2,))]),
        out_shape=jax.ShapeDtypeStruct((n, d), out_dtype),
        compiler_params=_params("arbitrary"),
        name="moe_gather",
    )(src, h)


def _moe_up_kernel(te_ref, tv_ref, x_ref, w1_ref, w3_ref, o_ref):
    del te_ref
    valid = tv_ref[pl.program_id(0)] == 1

    @pl.when(valid)
    def _():
        a = x_ref[...]
        h1 = jnp.dot(a, w1_ref[...], preferred_element_type=F32)
        h3 = jnp.dot(a, w3_ref[...], preferred_element_type=F32)
        o_ref[...] = (h1 * jax.nn.sigmoid(h1) * h3).astype(o_ref.dtype)

    @pl.when(jnp.logical_not(valid))
    def _():
        o_ref[...] = jnp.zeros_like(o_ref)


def _moe_down_kernel(te_ref, tv_ref, a_ref, w_ref, o_ref):
    del te_ref
    valid = tv_ref[pl.program_id(0)] == 1

    @pl.when(valid)
    def _():
        o_ref[...] = jnp.dot(a_ref[...], w_ref[...], preferred_element_type=F32)

    @pl.when(jnp.logical_not(valid))
    def _():
        o_ref[...] = jnp.zeros_like(o_ref)


def _moe_experts(xs, w1, w3, w2, tile_expert, tile_valid, *, tm, tn_up=512, tn_down=1024):
    rows, d = xs.shape
    f = w1.shape[-1]
    tn_up, tn_down = min(tn_up, f), min(tn_down, d)
    n_tiles = rows // tm
    hid = pl.pallas_call(
        _moe_up_kernel,
        grid_spec=pltpu.PrefetchScalarGridSpec(
            num_scalar_prefetch=2, grid=(n_tiles, f // tn_up),
            in_specs=[pl.BlockSpec((tm, d), lambda i, j, te, tv: (i, 0)),
                      pl.BlockSpec((None, d, tn_up), lambda i, j, te, tv: (te[i], 0, j)),
                      pl.BlockSpec((None, d, tn_up), lambda i, j, te, tv: (te[i], 0, j))],
            out_specs=pl.BlockSpec((tm, tn_up), lambda i, j, te, tv: (i, j))),
        out_shape=jax.ShapeDtypeStruct((rows, f), BF16),
        compiler_params=_params("parallel", "arbitrary"),
        name="moe_up",
    )(tile_expert, tile_valid, xs, w1, w3)
    return pl.pallas_call(
        _moe_down_kernel,
        grid_spec=pltpu.PrefetchScalarGridSpec(
            num_scalar_prefetch=2, grid=(n_tiles, d // tn_down),
            in_specs=[pl.BlockSpec((tm, f), lambda i, j, te, tv: (i, 0)),
                      pl.BlockSpec((None, f, tn_down), lambda i, j, te, tv: (te[i], 0, j))],
            out_specs=pl.BlockSpec((tm, tn_down), lambda i, j, te, tv: (i, j))),
        out_shape=jax.ShapeDtypeStruct((rows, d), F32),
        compiler_params=_params("parallel", "arbitrary"),
        name="moe_down",
    )(tile_expert, tile_valid, hid, w2)


def _moe_combine_kernel(dest_ref, x_ref, r_ref, g_ref, y_hbm, o_ref, ybuf, sem, *, tc):
    i = pl.program_id(0)

    def copy(tile, j, k):
        slot = tile % 2
        return pltpu.make_async_copy(y_hbm.at[pl.ds(dest_ref[(tile * tc + j) * 2 + k], 1)],
                                     ybuf.at[slot, k, pl.ds(j, 1)], sem.at[slot])

    def issue(tile):
        def body(j, carry):
            copy(tile, j, 0).start()
            copy(tile, j, 1).start()
            return carry
        lax.fori_loop(0, tc, body, 0, unroll=4)

    pl.when(i == 0)(lambda: issue(0))
    pl.when(i + 1 < pl.num_programs(0))(lambda: issue(i + 1))

    slot = i % 2
    for k in range(2):
        pltpu.make_async_copy(y_hbm.at[pl.ds(0, tc)], ybuf.at[slot, k], sem.at[slot]).wait()
    gate = g_ref[0]
    rows = 32

    def row_block(c, carry):
        rs = pl.ds(pl.multiple_of(c * rows, rows), rows)
        r = r_ref[rs, :]
        y = r[:, 2:3] * ybuf[slot, 0, rs, :] + r[:, 3:4] * ybuf[slot, 1, rs, :]
        o_ref[rs, :] = x_ref[rs, :] + gate * y
        return carry

    lax.fori_loop(0, tc // rows, row_block, 0, unroll=2)


def _moe_combine(x, y, routes, dest, mods, k_gate, *, n_lat, n_batch, tc=256):
    t, d = x.shape
    seg = _seg_fn(tc, n_lat, n_batch)
    return pl.pallas_call(
        functools.partial(_moe_combine_kernel, tc=tc),
        grid_spec=pltpu.PrefetchScalarGridSpec(
            num_scalar_prefetch=1, grid=(t // tc,),
            in_specs=[pl.BlockSpec((tc, d), lambda i, ds: (i, 0)),
                      pl.BlockSpec((tc, ROUTER_LANES), lambda i, ds: (i, 0)),
                      pl.BlockSpec((1, 1, d), lambda i, ds: (seg(i) * N_MOD + k_gate, 0, 0)),
                      pl.BlockSpec(memory_space=pl.ANY)],
            out_specs=pl.BlockSpec((tc, d), lambda i, ds: (i, 0)),
            scratch_shapes=[pltpu.VMEM((2, 2, tc, d), F32), pltpu.SemaphoreType.DMA((2,))]),
        out_shape=jax.ShapeDtypeStruct((t, d), F32),
        compiler_params=_params("arbitrary"),
        name="moe_combine",
    )(dest, x, routes, mods, y)


def _router(h, w_r, b_r, tm=512):
    t, d = h.shape
    e = w_r.shape[1]
    w_pad = jnp.zeros((d, ROUTER_LANES), BF16).at[:, :e].set(w_r.astype(BF16))
    b_pad = jnp.full((1, ROUTER_LANES), -jnp.inf, F32).at[0, :e].set(b_r)
    return pl.pallas_call(
        _router_kernel,
        grid=(t // tm,),
        in_specs=[pl.BlockSpec((tm, d), lambda i: (i, 0)),
                  pl.BlockSpec((d, ROUTER_LANES), lambda i: (0, 0)),
                  pl.BlockSpec((1, ROUTER_LANES), lambda i: (0, 0))],
        out_specs=pl.BlockSpec((tm, ROUTER_LANES), lambda i: (i, 0)),
        out_shape=jax.ShapeDtypeStruct((t, ROUTER_LANES), F32),
        compiler_params=_params("parallel"),
        name="router",
    )(h, w_pad, b_pad)


def kernel(x, c, ctx, c_ctx, ada_down, ada_up, ada_bias, norm1_g, norm2_g, w_in, sgu_norm_g, sgu_w,
           sgu_b, na_rpb, qk_norm_g, group_norm_g, w_out, ffn_w1, ffn_w3, ffn_w2, moe_router,
           moe_router_b, moe_w1, moe_w3, moe_w2, final_norm_g):
    n_batch, n_lat, d = x.shape
    n_ctx = ctx.shape[1]
    depth = w_in.shape[0]
    mix = w_out.shape[1]
    wa = wb = mix // 4
    wc = mix - wa - wb
    n_b_heads = wb // HEAD_DIM
    n_c_heads = wc // HEAD_DIM
    n_kv_heads = n_c_heads // 4
    wkv = n_kv_heads * HEAD_DIM
    group = n_c_heads // n_kv_heads
    n_experts = moe_router.shape[-1]
    dims = dict(n_lat=n_lat, n_batch=n_batch)
    lat_rows = n_batch * n_lat

    off_bq, off_bk, off_bv = 2 * wa, 2 * wa + wb, 2 * wa + 2 * wb
    src_cq = 2 * wa + 3 * wb
    src_ck = src_cq + wc
    off_ck = src_cq
    off_cv = off_ck + wkv
    off_cq = off_cv + wkv

    def reorder_in(w):
        return jnp.concatenate([w[:, :off_bq], w[:, off_bq:off_bk] * QK_PRESCALE, w[:, off_bk:src_cq],
                                w[:, src_ck:], w[:, src_cq:src_ck]], axis=1).astype(BF16)

    xa = jnp.concatenate([x.reshape(lat_rows, d), ctx.reshape(n_batch * n_ctx, d)], axis=0)
    cond = jnp.zeros((8, d), F32).at[:n_batch].set(c).at[n_batch].set(c_ctx)
    mods_all = _ada_mods(cond, ada_down, ada_up, ada_bias)
    cos_t, sin_t = _rope_tables(n_lat, n_ctx)

    for l in range(depth):
        mods = mods_all[l].reshape(8 * N_MOD, 1, d)
        h = _norm_mod(xa, norm1_g[l], mods, 0, 1, out_dtype=BF16, **dims)
        p = _matmul(h, reorder_in(w_in[l]), out_dtype=BF16)
        ya = _sgu(p, sgu_norm_g[l], sgu_w[l], sgu_b[l], group_norm_g[l, :wa], width=wa)
        bias = _na_bias_tables(na_rpb[l], n_lat // GRID_W)
        yb_lat = _neighbourhood_attention(
            p, bias, n_batch=n_batch, n_lat=n_lat, n_ctx=n_ctx, q_col=off_bq // HEAD_DIM,
            k_col=off_bk // HEAD_DIM, v_col=off_bv // HEAD_DIM, n_heads=n_b_heads)
        kb_ctx = p[lat_rows:, off_bk:off_bk + wb].reshape(n_batch, n_ctx, wb)
        vb_ctx = p[lat_rows:, off_bv:off_bv + wb].reshape(n_batch, n_ctx, wb)
        yb_ctx = _flash(p, kb_ctx, vb_ctx, n_batch=n_batch, n_q_rows=n_ctx, q_row0=lat_rows,
                        q_rows_per_batch=n_ctx, q_col=off_bq // HEAD_DIM, n_kv_heads=n_b_heads,
                        group=1, kv_row0=0, n_kv_rows=n_ctx)
        qc, kc, vc = _qk_prep(p, cos_t, sin_t, qk_norm_g[l, 0], qk_norm_g[l, 1], n_batch=n_batch,
                              n_lat=n_lat, n_ctx=n_ctx, q_off=off_cq // wc, n_q=n_c_heads,
                              k_off=off_ck // wkv, v_off=off_cv // wkv, n_kv=n_kv_heads)
        yc_lat = _flash_gqa(qc, kc, vc, n_batch=n_batch, n_q_rows=n_lat, q_row0=0,
                            q_rows_per_batch=n_lat, n_kv_heads=n_kv_heads, group=group,
                            kv_row0=0, n_kv_rows=n_lat + n_ctx)
        yc_ctx = _flash_gqa(qc, kc, vc, n_batch=n_batch, n_q_rows=n_ctx, q_row0=lat_rows,
                            q_rows_per_batch=n_ctx, n_kv_heads=n_kv_heads, group=group,
                            kv_row0=n_lat, n_kv_rows=n_ctx)
        merged = _merge(ya, yb_lat, yb_ctx, yc_lat, yc_ctx,
                        group_norm_g[l, wa:wa + wb], group_norm_g[l, wa + wb:])
        xa = _matmul_residual(merged, w_out[l].astype(BF16), xa, mods, 2, tn=1024, **dims)
        j = l // 2
        if l % 2 == 0:
            h2 = _norm_mod(xa, norm2_g[l], mods, 3, 4, out_dtype=BF16, **dims)
            hid = _matmul_swiglu(h2, ffn_w1[j].astype(BF16), ffn_w3[j].astype(BF16))
            xa = _matmul_residual(hid, ffn_w2[j].astype(BF16), xa, mods, 5, **dims)
        else:
            h2 = _norm_mod(xa, norm2_g[l], mods, 3, 4, out_dtype=F32, **dims)
            routes = _router(h2, moe_router[j], moe_router_b[j])
            src, dest, tile_expert, tile_valid = _route_plan(routes, n_experts, MOE_TILE)
            xs = _gather_rows(h2, src, out_dtype=BF16)
            ys = _moe_experts(xs, moe_w1[j].astype(BF16), moe_w3[j].astype(BF16),
                              moe_w2[j].astype(BF16), tile_expert, tile_valid, tm=MOE_TILE)
            xa = _moe_combine(xa, ys, routes, dest, mods, 5, **dims)

    out = _final_norm(xa, final_norm_g, lat_rows)
    return out.reshape(n_batch, n_lat, d)
```

```python
import functools
import math

import numpy as np
import jax
import jax.numpy as jnp
from jax import lax
from jax.experimental import pallas as pl
from jax.experimental.pallas import tpu as pltpu

F32 = jnp.float32
BF16 = jnp.bfloat16

HEAD_DIM = 128
GRID_W = 64
CHUNK = 128
NA_ROWS = 8
NA_COLS = 16
ROPE_THETA = 10000.0
EPS = 1e-6
N_MOD = 6
MASK_VALUE = -1e30
ROUTER_LANES = 128
MOE_TILE = 512
FLASH_LOOKAHEAD = 3
ROW_TILES = (1536, 1024, 512, 256)
NA_Q_ROWS = 8
NA_BAND_ROWS = 16
LOG2E = 1.4426950408889634
QK_PRESCALE = HEAD_DIM ** -0.5 * LOG2E
V7X_VMEM_BYTES = 64 * 1024 * 1024
VMEM_LIMIT = V7X_VMEM_BYTES - 8 * 1024 * 1024


def _params(*sem):
    return pltpu.CompilerParams(dimension_semantics=sem, vmem_limit_bytes=VMEM_LIMIT)


def _rms(x, g):
    return x * lax.rsqrt(jnp.mean(x * x, axis=-1, keepdims=True) + EPS) * g


def _ada_down_kernel(c_ref, w_ref, o_ref):
    c = c_ref[...]
    a = (c * jax.nn.sigmoid(c)).astype(BF16)
    o_ref[0] = jnp.dot(a, w_ref[0].astype(BF16), preferred_element_type=F32)


def _ada_up_kernel(t_ref, w_ref, b_ref, o_ref):
    t = t_ref[0].astype(BF16)
    o_ref[0] = jnp.dot(t, w_ref[0].astype(BF16), preferred_element_type=F32) + b_ref[0]


def _ada_mods(cond, w_down, w_up, b_up):
    depth, d, r = w_down.shape
    n_out = w_up.shape[2]
    rows = cond.shape[0]
    tn1 = min(r, 512)
    t = pl.pallas_call(
        _ada_down_kernel,
        grid=(depth, r // tn1),
        in_specs=[pl.BlockSpec((rows, d), lambda l, j: (0, 0)),
                  pl.BlockSpec((1, d, tn1), lambda l, j: (l, 0, j))],
        out_specs=pl.BlockSpec((1, rows, tn1), lambda l, j: (l, 0, j)),
        out_shape=jax.ShapeDtypeStruct((depth, rows, r), F32),
        compiler_params=_params("parallel", "parallel"),
        name="ada_down",
    )(cond, w_down)
    tn2 = min(n_out, 2048)
    return pl.pallas_call(
        _ada_up_kernel,
        grid=(depth, n_out // tn2),
        in_specs=[pl.BlockSpec((1, rows, r), lambda l, j: (l, 0, 0)),
                  pl.BlockSpec((1, r, tn2), lambda l, j: (l, 0, j)),
                  pl.BlockSpec((1, 1, tn2), lambda l, j: (l, 0, j))],
        out_specs=pl.BlockSpec((1, rows, tn2), lambda l, j: (l, 0, j)),
        out_shape=jax.ShapeDtypeStruct((depth, rows, n_out), F32),
        compiler_params=_params("parallel", "parallel"),
        name="ada_up",
    )(t, w_up, b_up.reshape(depth, 1, n_out))


def _norm_mod_kernel(x_ref, g_ref, sh_ref, sc_ref, o_ref):
    y = _rms(x_ref[...], g_ref[...])
    o_ref[...] = (y * (1.0 + sc_ref[0]) + sh_ref[0]).astype(o_ref.dtype)


def _norm_kernel(x_ref, g_ref, o_ref):
    o_ref[...] = _rms(x_ref[...], g_ref[...]).astype(o_ref.dtype)


def _seg_fn(tm, n_lat, n_batch):
    return lambda i: jnp.minimum((i * tm) // n_lat, n_batch)


def _norm_mod(x, g, mods, k_shift, k_scale, *, n_lat, n_batch, out_dtype, tm=256):
    t, d = x.shape
    seg = _seg_fn(tm, n_lat, n_batch)
    return pl.pallas_call(
        _norm_mod_kernel,
        grid=(t // tm,),
        in_specs=[pl.BlockSpec((tm, d), lambda i: (i, 0)),
                  pl.BlockSpec((1, d), lambda i: (0, 0)),
                  pl.BlockSpec((1, 1, d), lambda i: (seg(i) * N_MOD + k_shift, 0, 0)),
                  pl.BlockSpec((1, 1, d), lambda i: (seg(i) * N_MOD + k_scale, 0, 0))],
        out_specs=pl.BlockSpec((tm, d), lambda i: (i, 0)),
        out_shape=jax.ShapeDtypeStruct((t, d), out_dtype),
        compiler_params=_params("parallel"),
        name="norm_mod",
    )(x, g.reshape(1, d), mods, mods)


def _final_norm(x, g, rows, tm=256):
    d = x.shape[1]
    return pl.pallas_call(
        _norm_kernel,
        grid=(rows // tm,),
        in_specs=[pl.BlockSpec((tm, d), lambda i: (i, 0)),
                  pl.BlockSpec((1, d), lambda i: (0, 0))],
        out_specs=pl.BlockSpec((tm, d), lambda i: (i, 0)),
        out_shape=jax.ShapeDtypeStruct((rows, d), F32),
        compiler_params=_params("parallel"),
        name="final_norm",
    )(x, g.reshape(1, d))


def _mmw_kernel(a_ref, w_ref, cs_ref, o_ref, wb_sc):
    @pl.when(pl.program_id(1) == 0)
    def _():
        wb_sc[...] = (w_ref[...] * cs_ref[...]).astype(BF16)

    o_ref[...] = jnp.dot(a_ref[...], wb_sc[...], preferred_element_type=F32).astype(o_ref.dtype)


def _mmw_swiglu_kernel(a_ref, w1_ref, w3_ref, o_ref, w1_sc, w3_sc):
    @pl.when(pl.program_id(1) == 0)
    def _():
        w1_sc[...] = w1_ref[...].astype(BF16)
        w3_sc[...] = w3_ref[...].astype(BF16)

    a = a_ref[...]
    h1 = jnp.dot(a, w1_sc[...], preferred_element_type=F32)
    h3 = jnp.dot(a, w3_sc[...], preferred_element_type=F32)
    o_ref[...] = (h1 * jax.nn.sigmoid(h1) * h3).astype(o_ref.dtype)


def _mmw_res_kernel(a_ref, w_ref, x_ref, glo_ref, ghi_ref, o_ref, wb_sc, *, tm, n_lat, n_batch):
    i = pl.program_id(1)

    @pl.when(i == 0)
    def _():
        wb_sc[...] = w_ref[...].astype(BF16)

    acc = jnp.dot(a_ref[...], wb_sc[...], preferred_element_type=F32)
    row0 = i * tm
    seg_hi = jnp.minimum((row0 + tm - 1) // n_lat, n_batch)
    boundary = seg_hi * n_lat - row0
    rows = lax.broadcasted_iota(jnp.int32, (tm, 1), 0)
    gate = jnp.where(rows < boundary, glo_ref[0], ghi_ref[0])
    o_ref[...] = x_ref[...] + gate * acc


def _mm_res_kernel(a_ref, b_ref, x_ref, g_ref, o_ref):
    acc = jnp.dot(a_ref[...], b_ref[...], preferred_element_type=F32)
    o_ref[...] = x_ref[...] + g_ref[0] * acc


def _w_spec(w, tn):
    return pl.BlockSpec((w.shape[0], tn), lambda i, j: (0, j))


def _pick_tile(n, candidates):
    for c in candidates:
        if n % c == 0:
            return c
    raise ValueError(f"no tile for {n}")


def _matmul_w32(a, w, col_scale, col_block, *, out_dtype, tn):
    m, k = a.shape
    n = w.shape[1]
    tm = _pick_tile(m, ROW_TILES)
    return pl.pallas_call(
        _mmw_kernel,
        grid=(n // tn, m // tm),
        in_specs=[pl.BlockSpec((tm, k), lambda j, i: (i, 0)),
                  pl.BlockSpec((k, tn), lambda j, i: (0, col_block(j))),
                  pl.BlockSpec((1, tn), lambda j, i: (0, col_block(j)))],
        out_specs=pl.BlockSpec((tm, tn), lambda j, i: (i, j)),
        out_shape=jax.ShapeDtypeStruct((m, n), out_dtype),
        scratch_shapes=[pltpu.VMEM((k, tn), BF16)],
        compiler_params=_params("parallel", "arbitrary"),
        name="matmul_w32",
    )(a, w, col_scale)


def _matmul_swiglu_w32(a, w1, w3, *, tn=256):
    m, k = a.shape
    n = w1.shape[1]
    tn = min(tn, n)
    tm = _pick_tile(m, ROW_TILES)
    return pl.pallas_call(
        _mmw_swiglu_kernel,
        grid=(n // tn, m // tm),
        in_specs=[pl.BlockSpec((tm, k), lambda j, i: (i, 0)),
                  pl.BlockSpec((k, tn), lambda j, i: (0, j)),
                  pl.BlockSpec((k, tn), lambda j, i: (0, j))],
        out_specs=pl.BlockSpec((tm, tn), lambda j, i: (i, j)),
        out_shape=jax.ShapeDtypeStruct((m, n), BF16),
        scratch_shapes=[pltpu.VMEM((k, tn), BF16), pltpu.VMEM((k, tn), BF16)],
        compiler_params=_params("parallel", "arbitrary"),
        name="matmul_swiglu_w32",
    )(a, w1, w3)


def _matmul_residual_w32(a, w, x, mods, k_gate, *, n_lat, n_batch, tn=256):
    m, k = a.shape
    n = w.shape[1]
    tn = min(tn, n)
    tm = _pick_tile(m, ROW_TILES)
    assert tm <= n_lat

    def seg_lo(i):
        return jnp.minimum((i * tm) // n_lat, n_batch)

    def seg_hi(i):
        return jnp.minimum((i * tm + tm - 1) // n_lat, n_batch)

    return pl.pallas_call(
        functools.partial(_mmw_res_kernel, tm=tm, n_lat=n_lat, n_batch=n_batch),
        grid=(n // tn, m // tm),
        in_specs=[pl.BlockSpec((tm, k), lambda j, i: (i, 0)),
                  pl.BlockSpec((k, tn), lambda j, i: (0, j)),
                  pl.BlockSpec((tm, tn), lambda j, i: (i, j)),
                  pl.BlockSpec((1, 1, tn), lambda j, i: (seg_lo(i) * N_MOD + k_gate, 0, j)),
                  pl.BlockSpec((1, 1, tn), lambda j, i: (seg_hi(i) * N_MOD + k_gate, 0, j))],
        out_specs=pl.BlockSpec((tm, tn), lambda j, i: (i, j)),
        out_shape=jax.ShapeDtypeStruct((m, n), F32),
        scratch_shapes=[pltpu.VMEM((k, tn), BF16)],
        compiler_params=_params("parallel", "arbitrary"),
        name="matmul_residual_w32",
    )(a, w, x, mods, mods)


def _matmul_residual(a, w, x, mods, k_gate, *, n_lat, n_batch, tm=512, tn=512):
    m, k = a.shape
    n = w.shape[-1]
    tn = min(tn, n)
    seg = _seg_fn(tm, n_lat, n_batch)
    return pl.pallas_call(
        _mm_res_kernel,
        grid=(m // tm, n // tn),
        in_specs=[pl.BlockSpec((tm, k), lambda i, j: (i, 0)), _w_spec(w, tn),
                  pl.BlockSpec((tm, tn), lambda i, j: (i, j)),
                  pl.BlockSpec((1, 1, tn), lambda i, j: (seg(i) * N_MOD + k_gate, 0, j))],
        out_specs=pl.BlockSpec((tm, tn), lambda i, j: (i, j)),
        out_shape=jax.ShapeDtypeStruct((m, n), F32),
        compiler_params=_params("parallel", "arbitrary"),
        name="matmul_residual",
    )(a, w, x, mods)


def _sgu_kernel(u_ref, v_ref, gv_ref, ws_ref, bs_ref, ga_ref, o_ref, *, n_chunks, n_groups):
    for c in range(n_chunks):
        rows = slice(c * CHUNK, (c + 1) * CHUNK)
        u = jax.nn.gelu(u_ref[rows, :].astype(F32), approximate=True)
        v = jax.nn.gelu(v_ref[rows, :].astype(F32), approximate=True)
        vb = _rms(v, gv_ref[...]).astype(BF16)
        parts = [jnp.dot(ws_ref[g], vb[:, g * HEAD_DIM:(g + 1) * HEAD_DIM],
                         preferred_element_type=F32) for g in range(n_groups)]
        y = u * (jnp.concatenate(parts, axis=1) + bs_ref[...])
        o_ref[rows, :] = _rms(y, ga_ref[...]).astype(o_ref.dtype)


def _sgu(p, g_v, w_s, b_s, g_a, *, width, ta=256):
    t = p.shape[0]
    n_groups = width // HEAD_DIM
    bias =jnp.repeat(b_s.T, HEAD_DIM, axis=1)
    return pl.pallas_call(
        functools.partial(_sgu_kernel, n_chunks=ta // CHUNK, n_groups=n_groups),
        grid=(t // ta,),
        in_specs=[pl.BlockSpec((ta, width), lambda i: (i, 0)),
                  pl.BlockSpec((ta, width), lambda i: (i, 1)),
                  pl.BlockSpec((1, width), lambda i: (0, 0)),
                  pl.BlockSpec((n_groups, CHUNK, CHUNK), lambda i: (0, 0, 0)),
                  pl.BlockSpec((CHUNK, width), lambda i: (0, 0)),
                  pl.BlockSpec((1, width), lambda i: (0, 0))],
        out_specs=pl.BlockSpec((ta, width), lambda i: (i, 0)),
        out_shape=jax.ShapeDtypeStruct((t, width), BF16),
        compiler_params=_params("parallel"),
        name="sgu",
    )(p, p, g_v.reshape(1, width), w_s.astype(BF16), bias, g_a.reshape(1, width))


def _na_bias_tables(rpb, rows):
    n_blocks = rows // NA_Q_ROWS
    cols = np.arange(GRID_W)
    col_start = np.clip(cols - NA_COLS // 2, 0, GRID_W - NA_COLS)
    dc = cols[None, :] - cols[:, None]
    col_ok = (cols[None, :] >= col_start[:, None]) & (cols[None, :] < col_start[:, None] + NA_COLS)
    col_idx = np.where(col_ok, dc + NA_COLS - 1, 0)
    n_dr = 2 * NA_ROWS - 1
    t = jnp.where(col_ok[None, None], rpb[:, :, col_idx] * LOG2E, MASK_VALUE)
    t = jnp.concatenate([t, jnp.full((rpb.shape[0], 1, GRID_W, GRID_W), MASK_VALUE, F32)], axis=1)
    slab = np.full((3, NA_Q_ROWS, NA_BAND_ROWS), n_dr, np.int32)
    rep = [0, min(1, n_blocks - 1), n_blocks - 1]
    for ty, blk in enumerate(rep):
        band0 = int(np.clip(blk * NA_Q_ROWS - NA_ROWS // 2, 0, rows - NA_BAND_ROWS))
        for rq in range(NA_Q_ROWS):
            r = blk * NA_Q_ROWS + rq
            start = int(np.clip(r - NA_ROWS // 2, 0, rows - NA_ROWS))
            for kr in range(NA_BAND_ROWS):
                ar = band0 + kr
                if start <= ar < start + NA_ROWS:
                    slab[ty, rq, kr] = ar - r + NA_ROWS - 1
    pairs = slab.reshape(3, NA_Q_ROWS, NA_BAND_ROWS // 2, 2)
    uniq, inverse = np.unique(pairs.reshape(-1, 2), axis=0, return_inverse=True)
    t2 = jnp.concatenate([t[:, uniq[:, 0]], t[:, uniq[:, 1]]], axis=-1)
    return t2, jnp.asarray(inverse.reshape(-1), jnp.int32)


def _na_kernel(pair_ref, q_ref, k_ref, v_ref, kc_ref, vc_ref, bias_ref, o_ref, *, rows, rc, n_blocks):
    i = pl.program_id(2)
    block_type = jnp.where(i == 0, 0, jnp.where(i == n_blocks - 1, 2, 1))
    pairs_per_row = NA_BAND_ROWS // 2

    def bias_rows(rq):
        base = (block_type * NA_Q_ROWS + rq) * pairs_per_row
        return jnp.concatenate([bias_ref[pair_ref[base + m]] for m in range(pairs_per_row)], axis=1)

    band0 = jnp.clip(i * NA_Q_ROWS - NA_ROWS // 2, 0, rows - NA_BAND_ROWS)
    start = pl.multiple_of(band0 * GRID_W, GRID_W)
    nb = NA_BAND_ROWS * GRID_W
    kb = k_ref[pl.ds(start, nb), :]
    vb = v_ref[pl.ds(start, nb), :]
    kc = kc_ref[...]
    vc = vc_ref[...]
    dn = (((1,), (1,)), ((), ()))
    n_chunks = q_ref.shape[0] // rc
    n_loc = nb // HEAD_DIM

    def scores(c):
        q = q_ref[c * rc:(c + 1) * rc, :]
        return (lax.dot_general(q, kb, dn, preferred_element_type=F32),
                lax.dot_general(q, kc, dn, preferred_element_type=F32))

    nxt = scores(0)
    for c in range(n_chunks):
        s_loc, s_ctx = nxt
        if c + 1 < n_chunks:
            nxt = scores(c + 1)
        q_rows = rc // GRID_W
        s_loc = s_loc + jnp.concatenate([bias_rows(c * q_rows + r) for r in range(q_rows)], axis=0)
        cols = ([s_loc[:, j * HEAD_DIM:(j + 1) * HEAD_DIM] for j in range(n_loc)]
                + [s_ctx[:, j * HEAD_DIM:(j + 1) * HEAD_DIM] for j in range(s_ctx.shape[1] // HEAD_DIM)])
        m = jnp.max(functools.reduce(jnp.maximum, cols), axis=1, keepdims=True)
        ps = [jnp.exp2(x - m) for x in cols]
        l = jnp.sum(functools.reduce(jnp.add, ps), axis=1, keepdims=True)
        p_loc = jnp.concatenate([x.astype(BF16) for x in ps[:n_loc]], axis=1)
        p_ctx = jnp.concatenate([x.astype(BF16) for x in ps[n_loc:]], axis=1)
        o = (jnp.dot(p_loc, vb, preferred_element_type=F32)
             + jnp.dot(p_ctx, vc, preferred_element_type=F32))
        o_ref[c * rc:(c + 1) * rc, :] = (o / l).astype(o_ref.dtype)


def _neighbourhood_attention(p, bias, pair_idx, *, n_batch, n_lat, n_ctx, q_col, k_col, v_col, n_heads):
    rows = n_lat // GRID_W
    tq = NA_Q_ROWS * GRID_W
    n_blocks = n_lat // tq
    ctx_blk0 = (n_batch * n_lat) // n_ctx
    n_pairs = bias.shape[1]
    return pl.pallas_call(
        functools.partial(_na_kernel, rows=rows, rc=min(256, tq), n_blocks=n_blocks),
        grid_spec=pltpu.PrefetchScalarGridSpec(
            num_scalar_prefetch=1, grid=(n_heads, n_batch, n_blocks),
            in_specs=[pl.BlockSpec((tq, HEAD_DIM), lambda h, b, i, pr: (b * n_blocks + i, q_col + h)),
                      pl.BlockSpec((n_lat, HEAD_DIM), lambda h, b, i, pr: (b, k_col + h)),
                      pl.BlockSpec((n_lat, HEAD_DIM), lambda h, b, i, pr: (b, v_col + h)),
                      pl.BlockSpec((n_ctx, HEAD_DIM), lambda h, b, i, pr: (ctx_blk0 + b, k_col + h)),
                      pl.BlockSpec((n_ctx, HEAD_DIM), lambda h, b, i, pr: (ctx_blk0 + b, v_col + h)),
                      pl.BlockSpec((None, n_pairs, GRID_W, 2 * GRID_W), lambda h, b, i, pr: (h, 0, 0, 0))],
            out_specs=pl.BlockSpec((tq, HEAD_DIM), lambda h, b, i, pr: (b * n_blocks + i, h))),
        out_shape=jax.ShapeDtypeStruct((n_batch * n_lat, n_heads * HEAD_DIM), BF16),
        compiler_params=_params("parallel", "parallel", "arbitrary"),
        name="neighbourhood_attention",
    )(pair_idx, p, p, p, p, p, bias)


def _rope_tables(n_lat, n_ctx):
    t = jnp.arange(n_lat, dtype=jnp.int32)
    pos = jnp.stack([t // GRID_W, t % GRID_W], axis=-1).astype(F32)
    n_freq = HEAD_DIM // 4
    inv = 1.0 / (ROPE_THETA ** (jnp.arange(n_freq, dtype=F32) / n_freq))
    ang = pos[:, :, None] * inv
    cos, sin = jnp.cos(ang), jnp.sin(ang)
    cos_t = jnp.concatenate([cos[:, 0], cos[:, 0], cos[:, 1], cos[:, 1]], axis=-1)
    sin_t = jnp.concatenate([-sin[:, 0], sin[:, 0], -sin[:, 1], sin[:, 1]], axis=-1)
    cos_t = jnp.concatenate([cos_t, jnp.ones((n_ctx, HEAD_DIM), F32)], axis=0)
    sin_t = jnp.concatenate([sin_t, jnp.zeros((n_ctx, HEAD_DIM), F32)], axis=0)
    return cos_t, sin_t


def _qk_prep_kernel(q_ref, k_ref, v_ref, cos_ref, sin_ref, gq_ref, gk_ref,
                    qo_ref, ko_ref, vo_ref, *, n_q, n_kv):
    cos = cos_ref[...]
    sin = sin_ref[...]
    quarter = HEAD_DIM // 4
    lane = lax.broadcasted_iota(jnp.int32, cos.shape, 1)
    first_half = (lane % (2 * quarter)) < quarter

    def prep(x, g, out_scale):
        y = _rms(x.astype(F32), g)
        swapped = jnp.where(first_half,
                            pltpu.roll(y, HEAD_DIM - quarter, 1),
                            pltpu.roll(y, quarter, 1))
        return ((y * cos + swapped * sin) * out_scale).astype(BF16)

    for h in range(n_q):
        cs = slice(h * HEAD_DIM, (h + 1) * HEAD_DIM)
        qo_ref[:, cs] = prep(q_ref[:, cs], gq_ref[...], QK_PRESCALE)
    ones = jnp.ones((v_ref.shape[0], HEAD_DIM), BF16)
    for h in range(n_kv):
        cs = slice(h * HEAD_DIM, (h + 1) * HEAD_DIM)
        ko_ref[0, :, cs] = prep(k_ref[:, cs], gk_ref[...], 1.0)
        vo_ref[0, :, 2 * h * HEAD_DIM:(2 * h + 1) * HEAD_DIM] = v_ref[:, cs]
        vo_ref[0, :, (2 * h + 1) * HEAD_DIM:(2 * h + 2) * HEAD_DIM] = ones


def _qk_prep(p, cos_t, sin_t, g_q, g_k, *, n_batch, n_lat, n_ctx, q_off, n_q, k_off, v_off, n_kv):
    t = p.shape[0]
    tm = n_ctx
    lat_tiles = n_lat // tm
    n_lat_tiles = n_batch * lat_tiles

    def pos_blk(i):
        return jnp.where(i < n_lat_tiles, i % lat_tiles, lat_tiles)

    def kv_blk(i):
        c = i - n_lat_tiles
        return (jnp.where(i < n_lat_tiles, i // lat_tiles, c),
                jnp.where(i < n_lat_tiles, i % lat_tiles, lat_tiles), 0)

    wq, wkv = n_q * HEAD_DIM, n_kv * HEAD_DIM
    return pl.pallas_call(
        functools.partial(_qk_prep_kernel, n_q=n_q, n_kv=n_kv),
        grid=(t // tm,),
        in_specs=[pl.BlockSpec((tm, wq), lambda i: (i, q_off)),
                  pl.BlockSpec((tm, wkv), lambda i: (i, k_off)),
                  pl.BlockSpec((tm, wkv), lambda i: (i, v_off)),
                  pl.BlockSpec((tm, HEAD_DIM), lambda i: (pos_blk(i), 0)),
                  pl.BlockSpec((tm, HEAD_DIM), lambda i: (pos_blk(i), 0)),
                  pl.BlockSpec((1, HEAD_DIM), lambda i: (0, 0)),
                  pl.BlockSpec((1, HEAD_DIM), lambda i: (0, 0))],
        out_specs=[pl.BlockSpec((tm, wq), lambda i: (i, 0)),
                   pl.BlockSpec((1, tm, wkv), kv_blk),
                   pl.BlockSpec((1, tm, 2 * wkv), kv_blk)],
        out_shape=[jax.ShapeDtypeStruct((t, wq), BF16),
                   jax.ShapeDtypeStruct((n_batch, n_lat + n_ctx, wkv), BF16),
                   jax.ShapeDtypeStruct((n_batch, n_lat + n_ctx, 2 * wkv), BF16)],
        compiler_params=_params("parallel"),
        name="qk_prep",
    )(p, p, p, cos_t, sin_t, g_q.reshape(1, HEAD_DIM), g_k.reshape(1, HEAD_DIM))


def _flash_kernel(q_ref, k_ref, v_ref, o_ref, m_sc, l_sc, acc_sc, *, group, tq):
    kj = pl.program_id(3)

    @pl.when(kj == 0)
    def _():
        m_sc[...] = jnp.full_like(m_sc, -jnp.inf)
        l_sc[...] = jnp.zeros_like(l_sc)
        acc_sc[...] = jnp.zeros_like(acc_sc)

    if group == 1:
        q = q_ref[...]
    else:
        q = jnp.concatenate([q_ref[:, g * HEAD_DIM:(g + 1) * HEAD_DIM] for g in range(group)], axis=0)
    s = lax.dot_general(q, k_ref[0], (((1,), (1,)), ((), ())), preferred_element_type=F32)
    m_prev = m_sc[...]
    m_new = jnp.maximum(m_prev, jnp.max(s, axis=1, keepdims=True))
    alpha = jnp.exp2(m_prev - m_new)
    p = jnp.exp2(s - m_new)
    l_sc[...] = alpha * l_sc[...] + jnp.sum(p, axis=1, keepdims=True)
    acc_sc[...] = alpha * acc_sc[...] + jnp.dot(p.astype(BF16), v_ref[0], preferred_element_type=F32)
    m_sc[...] = m_new

    @pl.when(kj == pl.num_programs(3) - 1)
    def _():
        o = acc_sc[...] / l_sc[...]
        for g in range(group):
            o_ref[:, g * HEAD_DIM:(g + 1) * HEAD_DIM] = o[g * tq:(g + 1) * tq].astype(o_ref.dtype)


def _flash(q2d, k3d, v3d, *, n_batch, n_q_rows, q_row0, q_rows_per_batch, q_col, n_kv_heads,
           group, kv_row0, n_kv_rows, tq=256):
    tq = min(tq, n_q_rows)
    tk = _pick_tile(n_kv_rows, (1024, 768, 512, 256))
    qb0, qbb = q_row0 // tq, q_rows_per_batch // tq
    kb0 = kv_row0 // tk
    gw = group * HEAD_DIM
    return pl.pallas_call(
        functools.partial(_flash_kernel, group=group, tq=tq),
        grid=(n_batch, n_kv_heads, n_q_rows // tq, n_kv_rows // tk),
        in_specs=[pl.BlockSpec((tq, gw), lambda b, h, i, j: (qb0 + b * qbb + i, q_col + h)),
                  pl.BlockSpec((1, tk, HEAD_DIM), lambda b, h, i, j: (b, kb0 + j, h)),
                  pl.BlockSpec((1, tk, HEAD_DIM), lambda b, h, i, j: (b, kb0 + j, h))],
        out_specs=pl.BlockSpec((tq, gw), lambda b, h, i, j: (b * (n_q_rows // tq) + i, h)),
        out_shape=jax.ShapeDtypeStruct((n_batch * n_q_rows, n_kv_heads * gw), BF16),
        scratch_shapes=[pltpu.VMEM((group * tq, 1), F32),
                        pltpu.VMEM((group * tq, 1), F32),
                        pltpu.VMEM((group * tq, HEAD_DIM), F32)],
        compiler_params=_params("parallel", "parallel", "parallel", "arbitrary"),
        name="flash_attention",
    )(q2d, k3d, v3d)


def _flash_gqa_kernel(q_ref, k_ref, v_ref, o_ref, m_sc, acc_sc, *, group, tq, tk, rc):
    kj = pl.program_id(3)

    @pl.when(kj == 0)
    def _():
        m_sc[...] = jnp.full_like(m_sc, -jnp.inf)
        acc_sc[...] = jnp.zeros_like(acc_sc)

    n_cols = tk // HEAD_DIM
    k = k_ref[0]
    v = v_ref[0]
    chunks = [(g, r0) for g in range(group) for r0 in range(0, tq, rc)]

    def scores(chunk):
        g, r0 = chunk
        q = q_ref[r0:r0 + rc, g * HEAD_DIM:(g + 1) * HEAD_DIM]
        return lax.dot_general(q, k, (((1,), (1,)), ((), ())), preferred_element_type=F32)

    pending = [scores(ch) for ch in chunks[:FLASH_LOOKAHEAD]]
    for ci, (g, r0) in enumerate(chunks):
        if ci + FLASH_LOOKAHEAD < len(chunks):
            pending.append(scores(chunks[ci + FLASH_LOOKAHEAD]))
        s = pending.pop(0)
        rows = slice(g * tq + r0, g * tq + r0 + rc)
        cols = [s[:, i * HEAD_DIM:(i + 1) * HEAD_DIM] for i in range(n_cols)]
        col_max = functools.reduce(jnp.maximum, cols)
        m_prev = m_sc[rows, :]
        m_new = jnp.maximum(m_prev, jnp.max(col_max, axis=1, keepdims=True))
        alpha = jnp.exp2(m_prev - m_new)
        p = jnp.concatenate([jnp.exp2(c - m_new).astype(BF16) for c in cols], axis=1)
        pv = jnp.dot(p, v, preferred_element_type=F32)
        acc_sc[rows, :] = jnp.concatenate([alpha, alpha], axis=1) * acc_sc[rows, :] + pv
        m_sc[rows, :] = m_new

    @pl.when(kj == pl.num_programs(3) - 1)
    def _():
        o = acc_sc[:, :HEAD_DIM] / acc_sc[:, HEAD_DIM:]
        for g in range(group):
            o_ref[:, g * HEAD_DIM:(g + 1) * HEAD_DIM] = o[g * tq:(g + 1) * tq].astype(o_ref.dtype)


def _flash_gqa(q2d, k3d, v3d, *, n_batch, n_q_rows, q_row0, q_rows_per_batch, n_kv_heads, group,
               kv_row0, n_kv_rows, tq=512, rc=256):
    tq = min(tq, n_q_rows)
    rc = min(rc, tq)
    tk = _pick_tile(n_kv_rows, (1408, 1024, 768, 512, 256))
    qb0, qbb = q_row0 // tq, q_rows_per_batch // tq
    kb0 = kv_row0 // tk
    gw = group * HEAD_DIM
    return pl.pallas_call(
        functools.partial(_flash_gqa_kernel, group=group, tq=tq, tk=tk, rc=rc),
        grid=(n_batch, n_kv_heads, n_q_rows // tq, n_kv_rows // tk),
        in_specs=[pl.BlockSpec((tq, gw), lambda b, h, i, j: (qb0 + b * qbb + i, h)),
                  pl.BlockSpec((1, tk, HEAD_DIM), lambda b, h, i, j: (b, kb0 + j, h)),
                  pl.BlockSpec((1, tk, 2 * HEAD_DIM), lambda b, h, i, j: (b, kb0 + j, h))],
        out_specs=pl.BlockSpec((tq, gw), lambda b, h, i, j: (b * (n_q_rows // tq) + i, h)),
        out_shape=jax.ShapeDtypeStruct((n_batch * n_q_rows, n_kv_heads * gw), BF16),
        scratch_shapes=[pltpu.VMEM((group * tq, HEAD_DIM), F32),
                        pltpu.VMEM((group * tq, 2 * HEAD_DIM), F32)],
        compiler_params=_params("parallel", "parallel", "parallel", "arbitrary"),
        name="flash_gqa",
    )(q2d, k3d, v3d)


def _merge_kernel(ya_ref, yb_lat_ref, yb_ctx_ref, yc_lat_ref, yc_ctx_ref, gb_ref, gc_ref, o_ref,
                  *, wa, wb, n_lat_tiles):
    o_ref[:, :wa] = ya_ref[...]

    def emit(yb_ref, yc_ref):
        o_ref[:, wa:wa + wb] = _rms(yb_ref[...].astype(F32), gb_ref[...]).astype(o_ref.dtype)
        o_ref[:, wa + wb:] = _rms(yc_ref[...].astype(F32), gc_ref[...]).astype(o_ref.dtype)

    is_lat = pl.program_id(0) < n_lat_tiles
    pl.when(is_lat)(lambda: emit(yb_lat_ref, yc_lat_ref))
    pl.when(jnp.logical_not(is_lat))(lambda: emit(yb_ctx_ref, yc_ctx_ref))


def _merge(ya, yb_lat, yb_ctx, yc_lat, yc_ctx, g_b, g_c, tm=256):
    t, wa = ya.shape
    wb, wc = yb_lat.shape[1], yc_lat.shape[1]
    n_lat_tiles = yb_lat.shape[0] // tm

    def lat(i):
        return (jnp.minimum(i, n_lat_tiles - 1), 0)

    def ctx(i):
        return (jnp.maximum(i - n_lat_tiles, 0), 0)

    return pl.pallas_call(
        functools.partial(_merge_kernel, wa=wa, wb=wb, n_lat_tiles=n_lat_tiles),
        grid=(t // tm,),
        in_specs=[pl.BlockSpec((tm, wa), lambda i: (i, 0)),
                  pl.BlockSpec((tm, wb), lat), pl.BlockSpec((tm, wb), ctx),
                  pl.BlockSpec((tm, wc), lat), pl.BlockSpec((tm, wc), ctx),
                  pl.BlockSpec((1, wb), lambda i: (0, 0)),
                  pl.BlockSpec((1, wc), lambda i: (0, 0))],
        out_specs=pl.BlockSpec((tm, wa + wb + wc), lambda i: (i, 0)),
        out_shape=jax.ShapeDtypeStruct((t, wa + wb + wc), BF16),
        compiler_params=_params("parallel"),
        name="merge_groups",
    )(ya, yb_lat, yb_ctx, yc_lat, yc_ctx, g_b.reshape(1, wb), g_c.reshape(1, wc))


def _router_kernel(h_ref, w_ref, b_ref, o_ref):
    logits = jnp.dot(h_ref[...].astype(BF16), w_ref[...], preferred_element_type=F32) + b_ref[...]
    lane = lax.broadcasted_iota(jnp.int32, logits.shape, 1).astype(F32)
    m1 = jnp.max(logits, axis=1, keepdims=True)
    i1 = jnp.min(jnp.where(logits == m1, lane, float(ROUTER_LANES)), axis=1, keepdims=True)
    rest = jnp.where(lane == i1, -jnp.inf, logits)
    m2 = jnp.max(rest, axis=1, keepdims=True)
    i2 = jnp.min(jnp.where(rest == m2, lane, float(ROUTER_LANES)), axis=1, keepdims=True)
    e2 = jnp.exp(m2 - m1)
    g1 = 1.0 / (1.0 + e2)
    o_ref[...] = jnp.where(lane == 0.0, i1,
                           jnp.where(lane == 1.0, i2,
                                     jnp.where(lane == 2.0, g1,
                                               jnp.where(lane == 3.0, e2 * g1, 0.0))))


def _route_plan(routes, n_experts, tm):
    t = routes.shape[0]
    a = 2 * t
    flat_e = routes[:, :2].astype(jnp.int32).reshape(a)
    onehot = (flat_e[:, None] == jnp.arange(n_experts, dtype=jnp.int32)[None, :]).astype(jnp.int32)
    csum = jnp.cumsum(onehot, axis=0)
    counts = csum[-1]
    rank = jnp.take_along_axis(csum, flat_e[:, None], axis=1)[:, 0] - 1
    padded = ((counts + tm - 1) // tm) * tm
    ends = jnp.cumsum(padded)
    starts = ends - padded
    dest = starts[flat_e] + rank
    ustarts = jnp.cumsum(counts) - counts
    order = jnp.argsort(flat_e, stable=True).astype(jnp.int32)
    n_rows = a + n_experts * tm
    row = jnp.arange(n_rows, dtype=jnp.int32)
    row_e = jnp.minimum(jnp.searchsorted(ends, row, side="right"), n_experts - 1).astype(jnp.int32)
    k = row - starts[row_e]
    src_a = order[jnp.clip(ustarts[row_e] + k, 0, a - 1)]
    src = jnp.where(k < counts[row_e], src_a // 2, 0).astype(jnp.int32)
    tile_start = jnp.arange(n_rows // tm, dtype=jnp.int32) * tm
    tile_valid = (tile_start < ends[-1]).astype(jnp.int32)
    tile_expert = row_e[jnp.minimum(tile_start, ends[-1] - 1)]
    return src, dest.astype(jnp.int32), tile_expert, tile_valid


def _gather_rows_kernel(src_ref, h_hbm, o_ref, buf, sem, *, tg):
    i = pl.program_id(0)

    def copy(tile, j):
        slot = tile % 2
        return pltpu.make_async_copy(h_hbm.at[pl.ds(src_ref[tile * tg + j], 1)],
                                     buf.at[slot, pl.ds(j, 1)], sem.at[slot])

    def issue(tile):
        def body(j, carry):
            copy(tile, j).start()
            return carry
        lax.fori_loop(0, tg, body, 0, unroll=8)

    pl.when(i == 0)(lambda: issue(0))
    pl.when(i + 1 < pl.num_programs(0))(lambda: issue(i + 1))

    slot = i % 2
    pltpu.make_async_copy(h_hbm.at[pl.ds(0, tg)], buf.at[slot], sem.at[slot]).wait()
    o_ref[...] = buf[slot].astype(o_ref.dtype)


def _gather_rows(h, src, *, out_dtype, tg=512):
    n = src.shape[0]
    d = h.shape[1]
    return pl.pallas_call(
        functools.partial(_gather_rows_kernel, tg=tg),
        grid_spec=pltpu.PrefetchScalarGridSpec(
            num_scalar_prefetch=1, grid=(n // tg,),
            in_specs=[pl.BlockSpec(memory_space=pl.ANY)],
            out_specs=pl.BlockSpec((tg, d), lambda i, s: (i, 0)),
            scratch_shapes=[pltpu.VMEM((2, tg, d), h.dtype), pltpu.SemaphoreType.DMA((2,))]),
        out_shape=jax.ShapeDtypeStruct((n, d), out_dtype),
        compiler_params=_params("arbitrary"),
        name="moe_gather",
    )(src, h)


def _moe_up_kernel(te_ref, tv_ref, x_ref, w1_ref, w3_ref, o_ref):
    del te_ref
    valid = tv_ref[pl.program_id(0)] == 1

    @pl.when(valid)
    def _():
        a = x_ref[...]
        h1 = jnp.dot(a, w1_ref[...], preferred_element_type=F32)
        h3 = jnp.dot(a, w3_ref[...], preferred_element_type=F32)
        o_ref[...] = (h1 * jax.nn.sigmoid(h1) * h3).astype(o_ref.dtype)

    @pl.when(jnp.logical_not(valid))
    def _():
        o_ref[...] = jnp.zeros_like(o_ref)


def _moe_down_kernel(te_ref, tv_ref, a_ref, w_ref, o_ref):
    del te_ref
    valid = tv_ref[pl.program_id(0)] == 1

    @pl.when(valid)
    def _():
        o_ref[...] = jnp.dot(a_ref[...], w_ref[...], preferred_element_type=F32)

    @pl.when(jnp.logical_not(valid))
    def _():
        o_ref[...] = jnp.zeros_like(o_ref)


def _moe_experts(xs, w1, w3, w2, tile_expert, tile_valid, *, tm, tn_up=512, tn_down=2048):
    rows, d = xs.shape
    f = w1.shape[-1]
    tn_up, tn_down = min(tn_up, f), min(tn_down, d)
    n_tiles = rows // tm
    hid = pl.pallas_call(
        _moe_up_kernel,
        grid_spec=pltpu.PrefetchScalarGridSpec(
            num_scalar_prefetch=2, grid=(n_tiles, f // tn_up),
            in_specs=[pl.BlockSpec((tm, d), lambda i, j, te, tv: (i, 0)),
                      pl.BlockSpec((None, d, tn_up), lambda i, j, te, tv: (te[i], 0, j)),
                      pl.BlockSpec((None, d, tn_up), lambda i, j, te, tv: (te[i], 0, j))],
            out_specs=pl.BlockSpec((tm, tn_up), lambda i, j, te, tv: (i, j))),
        out_shape=jax.ShapeDtypeStruct((rows, f), BF16),
        compiler_params=_params("parallel", "arbitrary"),
        name="moe_up",
    )(tile_expert, tile_valid, xs, w1, w3)
    return pl.pallas_call(
        _moe_down_kernel,
        grid_spec=pltpu.PrefetchScalarGridSpec(
            num_scalar_prefetch=2, grid=(n_tiles, d // tn_down),
            in_specs=[pl.BlockSpec((tm, f), lambda i, j, te, tv: (i, 0)),
                      pl.BlockSpec((None, f, tn_down), lambda i, j, te, tv: (te[i], 0, j))],
            out_specs=pl.BlockSpec((tm, tn_down), lambda i, j, te, tv: (i, j))),
        out_shape=jax.ShapeDtypeStruct((rows, d), F32),
        compiler_params=_params("parallel", "arbitrary"),
        name="moe_down",
    )(tile_expert, tile_valid, hid, w2)


def _moe_combine_kernel(dest_ref, x_ref, r_ref, g_ref, y_hbm, o_ref, ybuf, sem, *, tc):
    i = pl.program_id(0)

    def copy(tile, j, k):
        slot = tile % 2
        return pltpu.make_async_copy(y_hbm.at[pl.ds(dest_ref[(tile * tc + j) * 2 + k], 1)],
                                     ybuf.at[slot, k, pl.ds(j, 1)], sem.at[slot])

    def issue(tile):
        def body(j, carry):
            copy(tile, j, 0).start()
            copy(tile, j, 1).start()
            return carry
        lax.fori_loop(0, tc, body, 0, unroll=4)

    pl.when(i == 0)(lambda: issue(0))
    pl.when(i + 1 < pl.num_programs(0))(lambda: issue(i + 1))

    slot = i % 2
    for k in range(2):
        pltpu.make_async_copy(y_hbm.at[pl.ds(0, tc)], ybuf.at[slot, k], sem.at[slot]).wait()
    gate = g_ref[0]
    rows = 32

    def row_block(c, carry):
        rs = pl.ds(pl.multiple_of(c * rows, rows), rows)
        r = r_ref[rs, :]
        y = r[:, 2:3] * ybuf[slot, 0, rs, :] + r[:, 3:4] * ybuf[slot, 1, rs, :]
        o_ref[rs, :] = x_ref[rs, :] + gate * y
        return carry

    lax.fori_loop(0, tc // rows, row_block, 0, unroll=2)


def _moe_combine(x, y, routes, dest, mods, k_gate, *, n_lat, n_batch, tc=256):
    t, d = x.shape
    seg = _seg_fn(tc, n_lat, n_batch)
    return pl.pallas_call(
        functools.partial(_moe_combine_kernel, tc=tc),
        grid_spec=pltpu.PrefetchScalarGridSpec(
            num_scalar_prefetch=1, grid=(t // tc,),
            in_specs=[pl.BlockSpec((tc, d), lambda i, ds: (i, 0)),
                      pl.BlockSpec((tc, ROUTER_LANES), lambda i, ds: (i, 0)),
                      pl.BlockSpec((1, 1, d), lambda i, ds: (seg(i) * N_MOD + k_gate, 0, 0)),
                      pl.BlockSpec(memory_space=pl.ANY)],
            out_specs=pl.BlockSpec((tc, d), lambda i, ds: (i, 0)),
            scratch_shapes=[pltpu.VMEM((2, 2, tc, d), F32), pltpu.SemaphoreType.DMA((2,))]),
        out_shape=jax.ShapeDtypeStruct((t, d), F32),
        compiler_params=_params("arbitrary"),
        name="moe_combine",
    )(dest, x, routes, mods, y)


def _router(h, w_r, b_r, tm=512):
    t, d = h.shape
    e = w_r.shape[1]
    w_pad = jnp.zeros((d, ROUTER_LANES), BF16).at[:, :e].set(w_r.astype(BF16))
    b_pad = jnp.full((1, ROUTER_LANES), -jnp.inf, F32).at[0, :e].set(b_r)
    return pl.pallas_call(
        _router_kernel,
        grid=(t // tm,),
        in_specs=[pl.BlockSpec((tm, d), lambda i: (i, 0)),
                  pl.BlockSpec((d, ROUTER_LANES), lambda i: (0, 0)),
                  pl.BlockSpec((1, ROUTER_LANES), lambda i: (0, 0))],
        out_specs=pl.BlockSpec((tm, ROUTER_LANES), lambda i: (i, 0)),
        out_shape=jax.ShapeDtypeStruct((t, ROUTER_LANES), F32),
        compiler_params=_params("parallel"),
        name="router",
    )(h, w_pad, b_pad)


def kernel(x, c, ctx, c_ctx, ada_down, ada_up, ada_bias, norm1_g, norm2_g, w_in, sgu_norm_g, sgu_w,
           sgu_b, na_rpb, qk_norm_g, group_norm_g, w_out, ffn_w1, ffn_w3, ffn_w2, moe_router,
           moe_router_b, moe_w1, moe_w3, moe_w2, final_norm_g):
    n_batch, n_lat, d = x.shape
    n_ctx = ctx.shape[1]
    depth = w_in.shape[0]
    mix = w_out.shape[1]
    wa = wb = mix // 4
    wc = mix - wa - wb
    n_b_heads = wb // HEAD_DIM
    n_c_heads = wc // HEAD_DIM
    n_kv_heads = n_c_heads // 4
    wkv = n_kv_heads * HEAD_DIM
    group = n_c_heads // n_kv_heads
    n_experts = moe_router.shape[-1]
    dims = dict(n_lat=n_lat, n_batch=n_batch)
    lat_rows = n_batch * n_lat

    off_bq, off_bk, off_bv = 2 * wa, 2 * wa + wb, 2 * wa + 2 * wb
    src_cq = 2 * wa + 3 * wb
    off_ck = src_cq
    off_cv = off_ck + wkv
    off_cq = off_cv + wkv
    in_width = off_cq + wc
    tn_in = math.gcd(512, wkv)
    nb_pre, nb_kv, nb_q = src_cq // tn_in, 2 * wkv // tn_in, wc // tn_in

    def in_col_block(j):
        return jnp.where(j < nb_pre, j, jnp.where(j < nb_pre + nb_kv, j + nb_q, j - nb_kv))

    in_col_scale = jnp.ones((1, in_width), F32).at[:, off_bq:off_bk].set(QK_PRESCALE)

    xa = jnp.concatenate([x.reshape(lat_rows, d), ctx.reshape(n_batch * n_ctx, d)], axis=0)
    cond = jnp.zeros((8, d), F32).at[:n_batch].set(c).at[n_batch].set(c_ctx)
    mods_all = _ada_mods(cond, ada_down, ada_up, ada_bias)
    cos_t, sin_t = _rope_tables(n_lat, n_ctx)

    for l in range(depth):
        mods = mods_all[l].reshape(8 * N_MOD, 1, d)
        h = _norm_mod(xa, norm1_g[l], mods, 0, 1, out_dtype=BF16, **dims)
        p = _matmul_w32(h, w_in[l], in_col_scale, in_col_block, out_dtype=BF16, tn=tn_in)
        ya = _sgu(p, sgu_norm_g[l], sgu_w[l], sgu_b[l], group_norm_g[l, :wa], width=wa)
        bias, pair_idx = _na_bias_tables(na_rpb[l], n_lat // GRID_W)
        yb_lat = _neighbourhood_attention(
            p, bias, pair_idx, n_batch=n_batch, n_lat=n_lat, n_ctx=n_ctx, q_col=off_bq // HEAD_DIM,
            k_col=off_bk // HEAD_DIM, v_col=off_bv // HEAD_DIM, n_heads=n_b_heads)
        kb_ctx = p[lat_rows:, off_bk:off_bk + wb].reshape(n_batch, n_ctx, wb)
        vb_ctx = p[lat_rows:, off_bv:off_bv + wb].reshape(n_batch, n_ctx, wb)
        yb_ctx = _flash(p, kb_ctx, vb_ctx, n_batch=n_batch, n_q_rows=n_ctx, q_row0=lat_rows,
                        q_rows_per_batch=n_ctx, q_col=off_bq // HEAD_DIM, n_kv_heads=n_b_heads,
                        group=1, kv_row0=0, n_kv_rows=n_ctx)
        qc, kc, vc = _qk_prep(p, cos_t, sin_t, qk_norm_g[l, 0], qk_norm_g[l, 1], n_batch=n_batch,
                              n_lat=n_lat, n_ctx=n_ctx, q_off=off_cq // wc, n_q=n_c_heads,
                              k_off=off_ck // wkv, v_off=off_cv // wkv, n_kv=n_kv_heads)
        yc_lat = _flash_gqa(qc, kc, vc, n_batch=n_batch, n_q_rows=n_lat, q_row0=0,
                            q_rows_per_batch=n_lat, n_kv_heads=n_kv_heads, group=group,
                            kv_row0=0, n_kv_rows=n_lat + n_ctx)
        yc_ctx = _flash_gqa(qc, kc, vc, n_batch=n_batch, n_q_rows=n_ctx, q_row0=lat_rows,
                            q_rows_per_batch=n_ctx, n_kv_heads=n_kv_heads, group=group,
                            kv_row0=n_lat, n_kv_rows=n_ctx)
        merged = _merge(ya, yb_lat, yb_ctx, yc_lat, yc_ctx,
                        group_norm_g[l, wa:wa + wb], group_norm_g[l, wa + wb:])
        xa = _matmul_residual_w32(merged, w_out[l], xa, mods, 2, **dims)
        j = l // 2
        if l % 2 == 0:
            h2 = _norm_mod(xa, norm2_g[l], mods, 3, 4, out_dtype=BF16, **dims)
            hid = _matmul_swiglu_w32(h2, ffn_w1[j], ffn_w3[j])
            xa = _matmul_residual(hid, ffn_w2[j].astype(BF16), xa, mods, 5, **dims)
        else:
            h2 = _norm_mod(xa, norm2_g[l], mods, 3, 4, out_dtype=F32, **dims)
            routes = _router(h2, moe_router[j], moe_router_b[j])
            src, dest, tile_expert, tile_valid = _route_plan(routes, n_experts, MOE_TILE)
            xs = _gather_rows(h2, src, out_dtype=BF16)
            ys = _moe_experts(xs, moe_w1[j].astype(BF16), moe_w3[j].astype(BF16),
                              moe_w2[j].astype(BF16), tile_expert, tile_valid, tm=MOE_TILE)
            xa = _moe_combine(xa, ys, routes, dest, mods, 5, **dims)

    out = _final_norm(xa, final_norm_g, lat_rows)
    return out.reshape(n_batch, n_lat, d)
```

```python
import functools
import math

import numpy as np
import jax
import jax.numpy as jnp
from jax import lax
from jax.experimental import pallas as pl
from jax.experimental.pallas import tpu as pltpu

F32 = jnp.float32
BF16 = jnp.bfloat16

HEAD_DIM = 128
GRID_W = 64
CHUNK = 128
NA_ROWS = 8
NA_COLS = 16
ROPE_THETA = 10000.0
EPS = 1e-6
N_MOD = 6
MASK_VALUE = -1e30
ROUTER_LANES = 128
MOE_TILE = 512
FLASH_LOOKAHEAD = 3
ROW_TILES = (1536, 1024, 512, 256)
NA_Q_ROWS = 8
NA_BAND_ROWS = 16
LOG2E = 1.4426950408889634
QK_PRESCALE = HEAD_DIM ** -0.5 * LOG2E
V7X_VMEM_BYTES = 64 * 1024 * 1024
VMEM_LIMIT = V7X_VMEM_BYTES - 8 * 1024 * 1024


def _params(*sem):
    return pltpu.CompilerParams(dimension_semantics=sem, vmem_limit_bytes=VMEM_LIMIT)


def _rms(x, g):
    return x * lax.rsqrt(jnp.mean(x * x, axis=-1, keepdims=True) + EPS) * g


def _ada_down_kernel(c_ref, w_ref, o_ref):
    c = c_ref[...]
    a = (c * jax.nn.sigmoid(c)).astype(BF16)
    o_ref[0] = jnp.dot(a, w_ref[0].astype(BF16), preferred_element_type=F32)


def _ada_up_kernel(t_ref, w_ref, b_ref, o_ref):
    t = t_ref[0].astype(BF16)
    o_ref[0] = jnp.dot(t, w_ref[0].astype(BF16), preferred_element_type=F32) + b_ref[0]


def _ada_mods(cond, w_down, w_up, b_up):
    depth, d, r = w_down.shape
    n_out = w_up.shape[2]
    rows = cond.shape[0]
    tn1 = min(r, 512)
    t = pl.pallas_call(
        _ada_down_kernel,
        grid=(depth, r // tn1),
        in_specs=[pl.BlockSpec((rows, d), lambda l, j: (0, 0)),
                  pl.BlockSpec((1, d, tn1), lambda l, j: (l, 0, j))],
        out_specs=pl.BlockSpec((1, rows, tn1), lambda l, j: (l, 0, j)),
        out_shape=jax.ShapeDtypeStruct((depth, rows, r), F32),
        compiler_params=_params("parallel", "parallel"),
        name="ada_down",
    )(cond, w_down)
    tn2 = min(n_out, 2048)
    return pl.pallas_call(
        _ada_up_kernel,
        grid=(depth, n_out // tn2),
        in_specs=[pl.BlockSpec((1, rows, r), lambda l, j: (l, 0, 0)),
                  pl.BlockSpec((1, r, tn2), lambda l, j: (l, 0, j)),
                  pl.BlockSpec((1, 1, tn2), lambda l, j: (l, 0, j))],
        out_specs=pl.BlockSpec((1, rows, tn2), lambda l, j: (l, 0, j)),
        out_shape=jax.ShapeDtypeStruct((depth, rows, n_out), F32),
        compiler_params=_params("parallel", "parallel"),
        name="ada_up",
    )(t, w_up, b_up.reshape(depth, 1, n_out))


def _norm_mod_kernel(x_ref, g_ref, sh_ref, sc_ref, o_ref):
    y = _rms(x_ref[...], g_ref[...])
    o_ref[...] = (y * (1.0 + sc_ref[0]) + sh_ref[0]).astype(o_ref.dtype)


def _norm_kernel(x_ref, g_ref, o_ref):
    o_ref[...] = _rms(x_ref[...], g_ref[...]).astype(o_ref.dtype)


def _seg_fn(tm, n_lat, n_batch):
    return lambda i: jnp.minimum((i * tm) // n_lat, n_batch)


def _norm_mod(x, g, mods, k_shift, k_scale, *, n_lat, n_batch, out_dtype, tm=256):
    t, d = x.shape
    seg = _seg_fn(tm, n_lat, n_batch)
    return pl.pallas_call(
        _norm_mod_kernel,
        grid=(t // tm,),
        in_specs=[pl.BlockSpec((tm, d), lambda i: (i, 0)),
                  pl.BlockSpec((1, d), lambda i: (0, 0)),
                  pl.BlockSpec((1, 1, d), lambda i: (seg(i) * N_MOD + k_shift, 0, 0)),
                  pl.BlockSpec((1, 1, d), lambda i: (seg(i) * N_MOD + k_scale, 0, 0))],
        out_specs=pl.BlockSpec((tm, d), lambda i: (i, 0)),
        out_shape=jax.ShapeDtypeStruct((t, d), out_dtype),
        compiler_params=_params("parallel"),
        name="norm_mod",
    )(x, g.reshape(1, d), mods, mods)


def _final_norm(x, g, rows, tm=256):
    d = x.shape[1]
    return pl.pallas_call(
        _norm_kernel,
        grid=(rows // tm,),
        in_specs=[pl.BlockSpec((tm, d), lambda i: (i, 0)),
                  pl.BlockSpec((1, d), lambda i: (0, 0))],
        out_specs=pl.BlockSpec((tm, d), lambda i: (i, 0)),
        out_shape=jax.ShapeDtypeStruct((rows, d), F32),
        compiler_params=_params("parallel"),
        name="final_norm",
    )(x, g.reshape(1, d))


def _mmw_kernel(a_ref, w_ref, cs_ref, o_ref, wb_sc):
    @pl.when(pl.program_id(1) == 0)
    def _():
        wb_sc[...] = (w_ref[...] * cs_ref[...]).astype(BF16)

    o_ref[...] = jnp.dot(a_ref[...], wb_sc[...], preferred_element_type=F32).astype(o_ref.dtype)


def _mmw_swiglu_kernel(a_ref, w1_ref, w3_ref, o_ref, w1_sc, w3_sc):
    @pl.when(pl.program_id(1) == 0)
    def _():
        w1_sc[...] = w1_ref[...].astype(BF16)
        w3_sc[...] = w3_ref[...].astype(BF16)

    a = a_ref[...]
    h1 = jnp.dot(a, w1_sc[...], preferred_element_type=F32)
    h3 = jnp.dot(a, w3_sc[...], preferred_element_type=F32)
    o_ref[...] = (h1 * jax.nn.sigmoid(h1) * h3).astype(o_ref.dtype)


def _mm_res_kernel(a_ref, b_ref, x_ref, g_ref, o_ref):
    acc = jnp.dot(a_ref[...], b_ref[...], preferred_element_type=F32)
    o_ref[...] = x_ref[...] + g_ref[0] * acc


def _pick_tile(n, candidates):
    for c in candidates:
        if n % c == 0:
            return c
    raise ValueError(f"no tile for {n}")


def _matmul_w32(a, w, layer, col_scale, col_block, *, out_dtype, tn):
    m, k = a.shape
    n = w.shape[2]
    tm = _pick_tile(m, ROW_TILES)
    return pl.pallas_call(
        _mmw_kernel,
        grid=(n // tn, m // tm),
        in_specs=[pl.BlockSpec((tm, k), lambda j, i: (i, 0)),
                  pl.BlockSpec((None, k, tn), lambda j, i: (layer, 0, col_block(j))),
                  pl.BlockSpec((1, tn), lambda j, i: (0, col_block(j)))],
        out_specs=pl.BlockSpec((tm, tn), lambda j, i: (i, j)),
        out_shape=jax.ShapeDtypeStruct((m, n), out_dtype),
        scratch_shapes=[pltpu.VMEM((k, tn), BF16)],
        compiler_params=_params("parallel", "arbitrary"),
        name="matmul_w32",
    )(a, w, col_scale)


def _matmul_swiglu_w32(a, w1, w3, layer, *, tn=256):
    m, k = a.shape
    n = w1.shape[2]
    tn = min(tn, n)
    tm = _pick_tile(m, ROW_TILES)
    return pl.pallas_call(
        _mmw_swiglu_kernel,
        grid=(n // tn, m // tm),
        in_specs=[pl.BlockSpec((tm, k), lambda j, i: (i, 0)),
                  pl.BlockSpec((None, k, tn), lambda j, i: (layer, 0, j)),
                  pl.BlockSpec((None, k, tn), lambda j, i: (layer, 0, j))],
        out_specs=pl.BlockSpec((tm, tn), lambda j, i: (i, j)),
        out_shape=jax.ShapeDtypeStruct((m, n), BF16),
        scratch_shapes=[pltpu.VMEM((k, tn), BF16), pltpu.VMEM((k, tn), BF16)],
        compiler_params=_params("parallel", "arbitrary"),
        name="matmul_swiglu_w32",
    )(a, w1, w3)


def _matmul_residual(a, w, layer, x, mods, k_gate, *, n_lat, n_batch, tm=512, tn=512):
    m, k = a.shape
    n = w.shape[2]
    tn = min(tn, n)
    seg = _seg_fn(tm, n_lat, n_batch)
    return pl.pallas_call(
        _mm_res_kernel,
        grid=(m // tm, n // tn),
        in_specs=[pl.BlockSpec((tm, k), lambda i, j: (i, 0)),
                  pl.BlockSpec((None, k, tn), lambda i, j: (layer, 0, j)),
                  pl.BlockSpec((tm, tn), lambda i, j: (i, j)),
                  pl.BlockSpec((1, 1, tn), lambda i, j: (seg(i) * N_MOD + k_gate, 0, j))],
        out_specs=pl.BlockSpec((tm, tn), lambda i, j: (i, j)),
        out_shape=jax.ShapeDtypeStruct((m, n), F32),
        compiler_params=_params("parallel", "arbitrary"),
        name="matmul_residual",
    )(a, w, x, mods)


def _sgu_kernel(u_ref, v_ref, gv_ref, ws_ref, bs_ref, ga_ref, o_ref, *, n_chunks, n_groups):
    for c in range(n_chunks):
        rows = slice(c * CHUNK, (c + 1) * CHUNK)
        u = jax.nn.gelu(u_ref[rows, :].astype(F32), approximate=True)
        v = jax.nn.gelu(v_ref[rows, :].astype(F32), approximate=True)
        vb = _rms(v, gv_ref[...]).astype(BF16)
        parts = [jnp.dot(ws_ref[g], vb[:, g * HEAD_DIM:(g + 1) * HEAD_DIM],
                         preferred_element_type=F32) for g in range(n_groups)]
        y = u * (jnp.concatenate(parts, axis=1) + bs_ref[...])
        o_ref[rows, :] = _rms(y, ga_ref[...]).astype(o_ref.dtype)


def _sgu(p, g_v, w_s, b_s, g_a, *, width, ta=256):
    t = p.shape[0]
    n_groups = width // HEAD_DIM
    bias =jnp.repeat(b_s.T, HEAD_DIM, axis=1)
    return pl.pallas_call(
        functools.partial(_sgu_kernel, n_chunks=ta // CHUNK, n_groups=n_groups),
        grid=(t // ta,),
        in_specs=[pl.BlockSpec((ta, width), lambda i: (i, 0)),
                  pl.BlockSpec((ta, width), lambda i: (i, 1)),
                  pl.BlockSpec((1, width), lambda i: (0, 0)),
                  pl.BlockSpec((n_groups, CHUNK, CHUNK), lambda i: (0, 0, 0)),
                  pl.BlockSpec((CHUNK, width), lambda i: (0, 0)),
                  pl.BlockSpec((1, width), lambda i: (0, 0))],
        out_specs=pl.BlockSpec((ta, width), lambda i: (i, 0)),
        out_shape=jax.ShapeDtypeStruct((t, width), BF16),
        compiler_params=_params("parallel"),
        name="sgu",
    )(p, p, g_v.reshape(1, width), w_s.astype(BF16), bias, g_a.reshape(1, width))


def _na_bias_tables(rpb, rows):
    n_blocks = rows // NA_Q_ROWS
    cols = np.arange(GRID_W)
    col_start = np.clip(cols - NA_COLS // 2, 0, GRID_W - NA_COLS)
    dc = cols[None, :] - cols[:, None]
    col_ok = (cols[None, :] >= col_start[:, None]) & (cols[None, :] < col_start[:, None] + NA_COLS)
    col_idx = np.where(col_ok, dc + NA_COLS - 1, 0)
    n_dr = 2 * NA_ROWS - 1
    t = jnp.where(col_ok[None, None], rpb[:, :, col_idx] * LOG2E, MASK_VALUE)
    t = jnp.concatenate([t, jnp.full((rpb.shape[0], 1, GRID_W, GRID_W), MASK_VALUE, F32)], axis=1)
    slab = np.full((3, NA_Q_ROWS, NA_BAND_ROWS), n_dr, np.int32)
    rep = [0, min(1, n_blocks - 1), n_blocks - 1]
    for ty, blk in enumerate(rep):
        band0 = int(np.clip(blk * NA_Q_ROWS - NA_ROWS // 2, 0, rows - NA_BAND_ROWS))
        for rq in range(NA_Q_ROWS):
            r = blk * NA_Q_ROWS + rq
            start = int(np.clip(r - NA_ROWS // 2, 0, rows - NA_ROWS))
            for kr in range(NA_BAND_ROWS):
                ar = band0 + kr
                if start <= ar < start + NA_ROWS:
                    slab[ty, rq, kr] = ar - r + NA_ROWS - 1
    pairs = slab.reshape(3, NA_Q_ROWS, NA_BAND_ROWS // 2, 2)
    uniq, inverse = np.unique(pairs.reshape(-1, 2), axis=0, return_inverse=True)
    t2 = jnp.concatenate([t[:, uniq[:, 0]], t[:, uniq[:, 1]]], axis=-1)
    return t2, jnp.asarray(inverse.reshape(-1), jnp.int32)


def _na_kernel(pair_ref, q_ref, k_ref, v_ref, kc_ref, vc_ref, bias_ref, o_ref, *, rows, rc, n_blocks):
    i = pl.program_id(2)
    block_type = jnp.where(i == 0, 0, jnp.where(i == n_blocks - 1, 2, 1))
    pairs_per_row = NA_BAND_ROWS // 2

    def bias_rows(rq):
        base = (block_type * NA_Q_ROWS + rq) * pairs_per_row
        return jnp.concatenate([bias_ref[pair_ref[base + m]] for m in range(pairs_per_row)], axis=1)

    band0 = jnp.clip(i * NA_Q_ROWS - NA_ROWS // 2, 0, rows - NA_BAND_ROWS)
    start = pl.multiple_of(band0 * GRID_W, GRID_W)
    nb = NA_BAND_ROWS * GRID_W
    kb = k_ref[pl.ds(start, nb), :]
    vb = v_ref[pl.ds(start, nb), :]
    kc = kc_ref[...]
    vc = vc_ref[...]
    dn = (((1,), (1,)), ((), ()))
    n_chunks = q_ref.shape[0] // rc
    n_loc = nb // HEAD_DIM

    def scores(c):
        q = q_ref[c * rc:(c + 1) * rc, :]
        return (lax.dot_general(q, kb, dn, preferred_element_type=F32),
                lax.dot_general(q, kc, dn, preferred_element_type=F32))

    nxt = scores(0)
    for c in range(n_chunks):
        s_loc, s_ctx = nxt
        if c + 1 < n_chunks:
            nxt = scores(c + 1)
        q_rows = rc // GRID_W
        s_loc = s_loc + jnp.concatenate([bias_rows(c * q_rows + r) for r in range(q_rows)], axis=0)
        cols = ([s_loc[:, j * HEAD_DIM:(j + 1) * HEAD_DIM] for j in range(n_loc)]
                + [s_ctx[:, j * HEAD_DIM:(j + 1) * HEAD_DIM] for j in range(s_ctx.shape[1] // HEAD_DIM)])
        m = jnp.max(functools.reduce(jnp.maximum, cols), axis=1, keepdims=True)
        ps = [jnp.exp2(x - m) for x in cols]
        l = jnp.sum(functools.reduce(jnp.add, ps), axis=1, keepdims=True)
        p_loc = jnp.concatenate([x.astype(BF16) for x in ps[:n_loc]], axis=1)
        p_ctx = jnp.concatenate([x.astype(BF16) for x in ps[n_loc:]], axis=1)
        o = (jnp.dot(p_loc, vb, preferred_element_type=F32)
             + jnp.dot(p_ctx, vc, preferred_element_type=F32))
        o_ref[c * rc:(c + 1) * rc, :] = (o / l).astype(o_ref.dtype)


def _neighbourhood_attention(p, bias, pair_idx, *, n_batch, n_lat, n_ctx, q_col, k_col, v_col, n_heads):
    rows = n_lat // GRID_W
    tq = NA_Q_ROWS * GRID_W
    n_blocks = n_lat // tq
    ctx_blk0 = (n_batch * n_lat) // n_ctx
    n_pairs = bias.shape[1]
    return pl.pallas_call(
        functools.partial(_na_kernel, rows=rows, rc=min(256, tq), n_blocks=n_blocks),
        grid_spec=pltpu.PrefetchScalarGridSpec(
            num_scalar_prefetch=1, grid=(n_heads, n_batch, n_blocks),
            in_specs=[pl.BlockSpec((tq, HEAD_DIM), lambda h, b, i, pr: (b * n_blocks + i, q_col + h)),
                      pl.BlockSpec((n_lat, HEAD_DIM), lambda h, b, i, pr: (b, k_col + h)),
                      pl.BlockSpec((n_lat, HEAD_DIM), lambda h, b, i, pr: (b, v_col + h)),
                      pl.BlockSpec((n_ctx, HEAD_DIM), lambda h, b, i, pr: (ctx_blk0 + b, k_col + h)),
                      pl.BlockSpec((n_ctx, HEAD_DIM), lambda h, b, i, pr: (ctx_blk0 + b, v_col + h)),
                      pl.BlockSpec((None, n_pairs, GRID_W, 2 * GRID_W), lambda h, b, i, pr: (h, 0, 0, 0))],
            out_specs=pl.BlockSpec((tq, HEAD_DIM), lambda h, b, i, pr: (b * n_blocks + i, h))),
        out_shape=jax.ShapeDtypeStruct((n_batch * n_lat, n_heads * HEAD_DIM), BF16),
        compiler_params=_params("parallel", "parallel", "arbitrary"),
        name="neighbourhood_attention",
    )(pair_idx, p, p, p, p, p, bias)


def _rope_tables(n_lat, n_ctx):
    t = jnp.arange(n_lat, dtype=jnp.int32)
    pos = jnp.stack([t // GRID_W, t % GRID_W], axis=-1).astype(F32)
    n_freq = HEAD_DIM // 4
    inv = 1.0 / (ROPE_THETA ** (jnp.arange(n_freq, dtype=F32) / n_freq))
    ang = pos[:, :, None] * inv
    cos, sin = jnp.cos(ang), jnp.sin(ang)
    cos_t = jnp.concatenate([cos[:, 0], cos[:, 0], cos[:, 1], cos[:, 1]], axis=-1)
    sin_t = jnp.concatenate([-sin[:, 0], sin[:, 0], -sin[:, 1], sin[:, 1]], axis=-1)
    cos_t = jnp.concatenate([cos_t, jnp.ones((n_ctx, HEAD_DIM), F32)], axis=0)
    sin_t = jnp.concatenate([sin_t, jnp.zeros((n_ctx, HEAD_DIM), F32)], axis=0)
    return cos_t, sin_t


def _qk_prep_kernel(q_ref, k_ref, v_ref, cos_ref, sin_ref, gq_ref, gk_ref,
                    qo_ref, ko_ref, vo_ref, *, n_q, n_kv):
    cos = cos_ref[...]
    sin = sin_ref[...]
    quarter = HEAD_DIM // 4
    lane = lax.broadcasted_iota(jnp.int32, cos.shape, 1)
    first_half = (lane % (2 * quarter)) < quarter

    def prep(x, g, out_scale):
        y = _rms(x.astype(F32), g)
        swapped = jnp.where(first_half,
                            pltpu.roll(y, HEAD_DIM - quarter, 1),
                            pltpu.roll(y, quarter, 1))
        return ((y * cos + swapped * sin) * out_scale).astype(BF16)

    for h in range(n_q):
        cs = slice(h * HEAD_DIM, (h + 1) * HEAD_DIM)
        qo_ref[:, cs] = prep(q_ref[:, cs], gq_ref[...], QK_PRESCALE)
    ones = jnp.ones((v_ref.shape[0], HEAD_DIM), BF16)
    for h in range(n_kv):
        cs = slice(h * HEAD_DIM, (h + 1) * HEAD_DIM)
        ko_ref[0, :, cs] = prep(k_ref[:, cs], gk_ref[...], 1.0)
        vo_ref[0, :, 2 * h * HEAD_DIM:(2 * h + 1) * HEAD_DIM] = v_ref[:, cs]
        vo_ref[0, :, (2 * h + 1) * HEAD_DIM:(2 * h + 2) * HEAD_DIM] = ones


def _qk_prep(p, cos_t, sin_t, g_q, g_k, *, n_batch, n_lat, n_ctx, q_off, n_q, k_off, v_off, n_kv):
    t = p.shape[0]
    tm = n_ctx
    lat_tiles = n_lat // tm
    n_lat_tiles = n_batch * lat_tiles

    def pos_blk(i):
        return jnp.where(i < n_lat_tiles, i % lat_tiles, lat_tiles)

    def kv_blk(i):
        c = i - n_lat_tiles
        return (jnp.where(i < n_lat_tiles, i // lat_tiles, c),
                jnp.where(i < n_lat_tiles, i % lat_tiles, lat_tiles), 0)

    wq, wkv = n_q * HEAD_DIM, n_kv * HEAD_DIM
    return pl.pallas_call(
        functools.partial(_qk_prep_kernel, n_q=n_q, n_kv=n_kv),
        grid=(t // tm,),
        in_specs=[pl.BlockSpec((tm, wq), lambda i: (i, q_off)),
                  pl.BlockSpec((tm, wkv), lambda i: (i, k_off)),
                  pl.BlockSpec((tm, wkv), lambda i: (i, v_off)),
                  pl.BlockSpec((tm, HEAD_DIM), lambda i: (pos_blk(i), 0)),
                  pl.BlockSpec((tm, HEAD_DIM), lambda i: (pos_blk(i), 0)),
                  pl.BlockSpec((1, HEAD_DIM), lambda i: (0, 0)),
                  pl.BlockSpec((1, HEAD_DIM), lambda i: (0, 0))],
        out_specs=[pl.BlockSpec((tm, wq), lambda i: (i, 0)),
                   pl.BlockSpec((1, tm, wkv), kv_blk),
                   pl.BlockSpec((1, tm, 2 * wkv), kv_blk)],
        out_shape=[jax.ShapeDtypeStruct((t, wq), BF16),
                   jax.ShapeDtypeStruct((n_batch, n_lat + n_ctx, wkv), BF16),
                   jax.ShapeDtypeStruct((n_batch, n_lat + n_ctx, 2 * wkv), BF16)],
        compiler_params=_params("parallel"),
        name="qk_prep",
    )(p, p, p, cos_t, sin_t, g_q.reshape(1, HEAD_DIM), g_k.reshape(1, HEAD_DIM))


def _flash_kernel(q_ref, k_ref, v_ref, o_ref, m_sc, l_sc, acc_sc, *, group, tq):
    kj = pl.program_id(3)

    @pl.when(kj == 0)
    def _():
        m_sc[...] = jnp.full_like(m_sc, -jnp.inf)
        l_sc[...] = jnp.zeros_like(l_sc)
        acc_sc[...] = jnp.zeros_like(acc_sc)

    if group == 1:
        q = q_ref[...]
    else:
        q = jnp.concatenate([q_ref[:, g * HEAD_DIM:(g + 1) * HEAD_DIM] for g in range(group)], axis=0)
    s = lax.dot_general(q, k_ref[0], (((1,), (1,)), ((), ())), preferred_element_type=F32)
    m_prev = m_sc[...]
    m_new = jnp.maximum(m_prev, jnp.max(s, axis=1, keepdims=True))
    alpha = jnp.exp2(m_prev - m_new)
    p = jnp.exp2(s - m_new)
    l_sc[...] = alpha * l_sc[...] + jnp.sum(p, axis=1, keepdims=True)
    acc_sc[...] = alpha * acc_sc[...] + jnp.dot(p.astype(BF16), v_ref[0], preferred_element_type=F32)
    m_sc[...] = m_new

    @pl.when(kj == pl.num_programs(3) - 1)
    def _():
        o = acc_sc[...] / l_sc[...]
        for g in range(group):
            o_ref[:, g * HEAD_DIM:(g + 1) * HEAD_DIM] = o[g * tq:(g + 1) * tq].astype(o_ref.dtype)


def _flash(q2d, k3d, v3d, *, n_batch, n_q_rows, q_row0, q_rows_per_batch, q_col, n_kv_heads,
           group, kv_row0, n_kv_rows, tq=256):
    tq = min(tq, n_q_rows)
    tk = _pick_tile(n_kv_rows, (1024, 768, 512, 256))
    qb0, qbb = q_row0 // tq, q_rows_per_batch // tq
    kb0 = kv_row0 // tk
    gw = group * HEAD_DIM
    return pl.pallas_call(
        functools.partial(_flash_kernel, group=group, tq=tq),
        grid=(n_batch, n_kv_heads, n_q_rows // tq, n_kv_rows // tk),
        in_specs=[pl.BlockSpec((tq, gw), lambda b, h, i, j: (qb0 + b * qbb + i, q_col + h)),
                  pl.BlockSpec((1, tk, HEAD_DIM), lambda b, h, i, j: (b, kb0 + j, h)),
                  pl.BlockSpec((1, tk, HEAD_DIM), lambda b, h, i, j: (b, kb0 + j, h))],
        out_specs=pl.BlockSpec((tq, gw), lambda b, h, i, j: (b * (n_q_rows // tq) + i, h)),
        out_shape=jax.ShapeDtypeStruct((n_batch * n_q_rows, n_kv_heads * gw), BF16),
        scratch_shapes=[pltpu.VMEM((group * tq, 1), F32),
                        pltpu.VMEM((group * tq, 1), F32),
                        pltpu.VMEM((group * tq, HEAD_DIM), F32)],
        compiler_params=_params("parallel", "parallel", "parallel", "arbitrary"),
        name="flash_attention",
    )(q2d, k3d, v3d)


def _flash_gqa_kernel(q_ref, k_ref, v_ref, o_ref, m_sc, acc_sc, *, group, tq, tk, rc):
    kj = pl.program_id(3)

    @pl.when(kj == 0)
    def _():
        m_sc[...] = jnp.full_like(m_sc, -jnp.inf)
        acc_sc[...] = jnp.zeros_like(acc_sc)

    n_cols = tk // HEAD_DIM
    k = k_ref[0]
    v = v_ref[0]
    chunks = [(g, r0) for g in range(group) for r0 in range(0, tq, rc)]

    def scores(chunk):
        g, r0 = chunk
        q = q_ref[r0:r0 + rc, g * HEAD_DIM:(g + 1) * HEAD_DIM]
        return lax.dot_general(q, k, (((1,), (1,)), ((), ())), preferred_element_type=F32)

    pending = [scores(ch) for ch in chunks[:FLASH_LOOKAHEAD]]
    for ci, (g, r0) in enumerate(chunks):
        if ci + FLASH_LOOKAHEAD < len(chunks):
            pending.append(scores(chunks[ci + FLASH_LOOKAHEAD]))
        s = pending.pop(0)
        rows = slice(g * tq + r0, g * tq + r0 + rc)
        cols = [s[:, i * HEAD_DIM:(i + 1) * HEAD_DIM] for i in range(n_cols)]
        col_max = functools.reduce(jnp.maximum, cols)
        m_prev = m_sc[rows, :]
        m_new = jnp.maximum(m_prev, jnp.max(col_max, axis=1, keepdims=True))
        alpha = jnp.exp2(m_prev - m_new)
        p = jnp.concatenate([jnp.exp2(c - m_new).astype(BF16) for c in cols], axis=1)
        pv = jnp.dot(p, v, preferred_element_type=F32)
        acc_sc[rows, :] = jnp.concatenate([alpha, alpha], axis=1) * acc_sc[rows, :] + pv
        m_sc[rows, :] = m_new

    @pl.when(kj == pl.num_programs(3) - 1)
    def _():
        o = acc_sc[:, :HEAD_DIM] / acc_sc[:, HEAD_DIM:]
        for g in range(group):
            o_ref[:, g * HEAD_DIM:(g + 1) * HEAD_DIM] = o[g * tq:(g + 1) * tq].astype(o_ref.dtype)


def _flash_gqa(q2d, k3d, v3d, *, n_batch, n_q_rows, q_row0, q_rows_per_batch, n_kv_heads, group,
               kv_row0, n_kv_rows, tq=512, rc=256):
    tq = min(tq, n_q_rows)
    rc = min(rc, tq)
    tk = _pick_tile(n_kv_rows, (1408, 1024, 768, 512, 256))
    qb0, qbb = q_row0 // tq, q_rows_per_batch // tq
    kb0 = kv_row0 // tk
    gw = group * HEAD_DIM
    return pl.pallas_call(
        functools.partial(_flash_gqa_kernel, group=group, tq=tq, tk=tk, rc=rc),
        grid=(n_batch, n_kv_heads, n_q_rows // tq, n_kv_rows // tk),
        in_specs=[pl.BlockSpec((tq, gw), lambda b, h, i, j: (qb0 + b * qbb + i, h)),
                  pl.BlockSpec((1, tk, HEAD_DIM), lambda b, h, i, j: (b, kb0 + j, h)),
                  pl.BlockSpec((1, tk, 2 * HEAD_DIM), lambda b, h, i, j: (b, kb0 + j, h))],
        out_specs=pl.BlockSpec((tq, gw), lambda b, h, i, j: (b * (n_q_rows // tq) + i, h)),
        out_shape=jax.ShapeDtypeStruct((n_batch * n_q_rows, n_kv_heads * gw), BF16),
        scratch_shapes=[pltpu.VMEM((group * tq, HEAD_DIM), F32),
                        pltpu.VMEM((group * tq, 2 * HEAD_DIM), F32)],
        compiler_params=_params("parallel", "parallel", "parallel", "arbitrary"),
        name="flash_gqa",
    )(q2d, k3d, v3d)


def _merge_kernel(ya_ref, yb_lat_ref, yb_ctx_ref, yc_lat_ref, yc_ctx_ref, gb_ref, gc_ref, o_ref,
                  *, wa, wb, n_lat_tiles):
    o_ref[:, :wa] = ya_ref[...]

    def emit(yb_ref, yc_ref):
        o_ref[:, wa:wa + wb] = _rms(yb_ref[...].astype(F32), gb_ref[...]).astype(o_ref.dtype)
        o_ref[:, wa + wb:] = _rms(yc_ref[...].astype(F32), gc_ref[...]).astype(o_ref.dtype)

    is_lat = pl.program_id(0) < n_lat_tiles
    pl.when(is_lat)(lambda: emit(yb_lat_ref, yc_lat_ref))
    pl.when(jnp.logical_not(is_lat))(lambda: emit(yb_ctx_ref, yc_ctx_ref))


def _merge(ya, yb_lat, yb_ctx, yc_lat, yc_ctx, g_b, g_c, tm=256):
    t, wa = ya.shape
    wb, wc = yb_lat.shape[1], yc_lat.shape[1]
    n_lat_tiles = yb_lat.shape[0] // tm

    def lat(i):
        return (jnp.minimum(i, n_lat_tiles - 1), 0)

    def ctx(i):
        return (jnp.maximum(i - n_lat_tiles, 0), 0)

    return pl.pallas_call(
        functools.partial(_merge_kernel, wa=wa, wb=wb, n_lat_tiles=n_lat_tiles),
        grid=(t // tm,),
        in_specs=[pl.BlockSpec((tm, wa), lambda i: (i, 0)),
                  pl.BlockSpec((tm, wb), lat), pl.BlockSpec((tm, wb), ctx),
                  pl.BlockSpec((tm, wc), lat), pl.BlockSpec((tm, wc), ctx),
                  pl.BlockSpec((1, wb), lambda i: (0, 0)),
                  pl.BlockSpec((1, wc), lambda i: (0, 0))],
        out_specs=pl.BlockSpec((tm, wa + wb + wc), lambda i: (i, 0)),
        out_shape=jax.ShapeDtypeStruct((t, wa + wb + wc), BF16),
        compiler_params=_params("parallel"),
        name="merge_groups",
    )(ya, yb_lat, yb_ctx, yc_lat, yc_ctx, g_b.reshape(1, wb), g_c.reshape(1, wc))


def _router_kernel(h_ref, w_ref, b_ref, o_ref):
    logits = jnp.dot(h_ref[...].astype(BF16), w_ref[...], preferred_element_type=F32) + b_ref[...]
    lane = lax.broadcasted_iota(jnp.int32, logits.shape, 1).astype(F32)
    m1 = jnp.max(logits, axis=1, keepdims=True)
    i1 = jnp.min(jnp.where(logits == m1, lane, float(ROUTER_LANES)), axis=1, keepdims=True)
    rest = jnp.where(lane == i1, -jnp.inf, logits)
    m2 = jnp.max(rest, axis=1, keepdims=True)
    i2 = jnp.min(jnp.where(rest == m2, lane, float(ROUTER_LANES)), axis=1, keepdims=True)
    e2 = jnp.exp(m2 - m1)
    g1 = 1.0 / (1.0 + e2)
    o_ref[...] = jnp.where(lane == 0.0, i1,
                           jnp.where(lane == 1.0, i2,
                                     jnp.where(lane == 2.0, g1,
                                               jnp.where(lane == 3.0, e2 * g1, 0.0))))


def _route_plan(routes, n_experts, tm):
    t = routes.shape[0]
    a = 2 * t
    flat_e = routes[:, :2].astype(jnp.int32).reshape(a)
    onehot = (flat_e[:, None] == jnp.arange(n_experts, dtype=jnp.int32)[None, :]).astype(jnp.int32)
    csum = jnp.cumsum(onehot, axis=0)
    counts = csum[-1]
    rank = jnp.take_along_axis(csum, flat_e[:, None], axis=1)[:, 0] - 1
    padded = ((counts + tm - 1) // tm) * tm
    ends = jnp.cumsum(padded)
    starts = ends - padded
    dest = starts[flat_e] + rank
    ustarts = jnp.cumsum(counts) - counts
    order = jnp.argsort(flat_e, stable=True).astype(jnp.int32)
    n_rows = a + n_experts * tm
    row = jnp.arange(n_rows, dtype=jnp.int32)
    row_e = jnp.minimum(jnp.searchsorted(ends, row, side="right"), n_experts - 1).astype(jnp.int32)
    k = row - starts[row_e]
    src_a = order[jnp.clip(ustarts[row_e] + k, 0, a - 1)]
    src = jnp.where(k < counts[row_e], src_a // 2, 0).astype(jnp.int32)
    tile_start = jnp.arange(n_rows // tm, dtype=jnp.int32) * tm
    tile_valid = (tile_start < ends[-1]).astype(jnp.int32)
    tile_expert = row_e[jnp.minimum(tile_start, ends[-1] - 1)]
    return src, dest.astype(jnp.int32), tile_expert, tile_valid


def _gather_rows_kernel(src_ref, h_hbm, o_ref, buf, sem, *, tg):
    i = pl.program_id(0)

    def copy(tile, j):
        slot = tile % 2
        return pltpu.make_async_copy(h_hbm.at[pl.ds(src_ref[tile * tg + j], 1)],
                                     buf.at[slot, pl.ds(j, 1)], sem.at[slot])

    def issue(tile):
        def body(j, carry):
            copy(tile, j).start()
            return carry
        lax.fori_loop(0, tg, body, 0, unroll=8)

    pl.when(i == 0)(lambda: issue(0))
    pl.when(i + 1 < pl.num_programs(0))(lambda: issue(i + 1))

    slot = i % 2
    pltpu.make_async_copy(h_hbm.at[pl.ds(0, tg)], buf.at[slot], sem.at[slot]).wait()
    o_ref[...] = buf[slot].astype(o_ref.dtype)


def _gather_rows(h, src, *, out_dtype, tg=512):
    n = src.shape[0]
    d = h.shape[1]
    return pl.pallas_call(
        functools.partial(_gather_rows_kernel, tg=tg),
        grid_spec=pltpu.PrefetchScalarGridSpec(
            num_scalar_prefetch=1, grid=(n // tg,),
            in_specs=[pl.BlockSpec(memory_space=pl.ANY)],
            out_specs=pl.BlockSpec((tg, d), lambda i, s: (i, 0)),
            scratch_shapes=[pltpu.VMEM((2, tg, d), h.dtype), pltpu.SemaphoreType.DMA((2,))]),
        out_shape=jax.ShapeDtypeStruct((n, d), out_dtype),
        compiler_params=_params("arbitrary"),
        name="moe_gather",
    )(src, h)


def _moe_up_kernel(te_ref, tv_ref, x_ref, w1_ref, w3_ref, o_ref):
    del te_ref
    valid = tv_ref[pl.program_id(0)] == 1

    @pl.when(valid)
    def _():
        a = x_ref[...]
        h1 = jnp.dot(a, w1_ref[...], preferred_element_type=F32)
        h3 = jnp.dot(a, w3_ref[...], preferred_element_type=F32)
        o_ref[...] = (h1 * jax.nn.sigmoid(h1) * h3).astype(o_ref.dtype)

    @pl.when(jnp.logical_not(valid))
    def _():
        o_ref[...] = jnp.zeros_like(o_ref)


def _moe_down_kernel(te_ref, tv_ref, a_ref, w_ref, o_ref):
    del te_ref
    valid = tv_ref[pl.program_id(0)] == 1

    @pl.when(valid)
    def _():
        o_ref[...] = jnp.dot(a_ref[...], w_ref[...], preferred_element_type=F32)

    @pl.when(jnp.logical_not(valid))
    def _():
        o_ref[...] = jnp.zeros_like(o_ref)


def _moe_experts(xs, w1, w3, w2, layer, tile_expert, tile_valid, *, tm, tn_up=512, tn_down=2048):
    rows, d = xs.shape
    f = w1.shape[-1]
    tn_up, tn_down = min(tn_up, f), min(tn_down, d)
    n_tiles = rows // tm
    hid = pl.pallas_call(
        _moe_up_kernel,
        grid_spec=pltpu.PrefetchScalarGridSpec(
            num_scalar_prefetch=2, grid=(n_tiles, f // tn_up),
            in_specs=[pl.BlockSpec((tm, d), lambda i, j, te, tv: (i, 0)),
                      pl.BlockSpec((None, None, d, tn_up), lambda i, j, te, tv: (layer, te[i], 0, j)),
                      pl.BlockSpec((None, None, d, tn_up), lambda i, j, te, tv: (layer, te[i], 0, j))],
            out_specs=pl.BlockSpec((tm, tn_up), lambda i, j, te, tv: (i, j))),
        out_shape=jax.ShapeDtypeStruct((rows, f), BF16),
        compiler_params=_params("parallel", "arbitrary"),
        name="moe_up",
    )(tile_expert, tile_valid, xs, w1, w3)
    return pl.pallas_call(
        _moe_down_kernel,
        grid_spec=pltpu.PrefetchScalarGridSpec(
            num_scalar_prefetch=2, grid=(n_tiles, d // tn_down),
            in_specs=[pl.BlockSpec((tm, f), lambda i, j, te, tv: (i, 0)),
                      pl.BlockSpec((None, None, f, tn_down), lambda i, j, te, tv: (layer, te[i], 0, j))],
            out_specs=pl.BlockSpec((tm, tn_down), lambda i, j, te, tv: (i, j))),
        out_shape=jax.ShapeDtypeStruct((rows, d), F32),
        compiler_params=_params("parallel", "arbitrary"),
        name="moe_down",
    )(tile_expert, tile_valid, hid, w2)


def _moe_combine_kernel(dest_ref, x_ref, r_ref, g_ref, y_hbm, o_ref, ybuf, sem, *, tc):
    i = pl.program_id(0)

    def copy(tile, j, k):
        slot = tile % 2
        return pltpu.make_async_copy(y_hbm.at[pl.ds(dest_ref[(tile * tc + j) * 2 + k], 1)],
                                     ybuf.at[slot, k, pl.ds(j, 1)], sem.at[slot])

    def issue(tile):
        def body(j, carry):
            copy(tile, j, 0).start()
            copy(tile, j, 1).start()
            return carry
        lax.fori_loop(0, tc, body, 0, unroll=4)

    pl.when(i == 0)(lambda: issue(0))
    pl.when(i + 1 < pl.num_programs(0))(lambda: issue(i + 1))

    slot = i % 2
    for k in range(2):
        pltpu.make_async_copy(y_hbm.at[pl.ds(0, tc)], ybuf.at[slot, k], sem.at[slot]).wait()
    gate = g_ref[0]
    rows = 32

    def row_block(c, carry):
        rs = pl.ds(pl.multiple_of(c * rows, rows), rows)
        r = r_ref[rs, :]
        y = r[:, 2:3] * ybuf[slot, 0, rs, :] + r[:, 3:4] * ybuf[slot, 1, rs, :]
        o_ref[rs, :] = x_ref[rs, :] + gate * y
        return carry

    lax.fori_loop(0, tc // rows, row_block, 0, unroll=2)


def _moe_combine(x, y, routes, dest, mods, k_gate, *, n_lat, n_batch, tc=256):
    t, d = x.shape
    seg = _seg_fn(tc, n_lat, n_batch)
    return pl.pallas_call(
        functools.partial(_moe_combine_kernel, tc=tc),
        grid_spec=pltpu.PrefetchScalarGridSpec(
            num_scalar_prefetch=1, grid=(t // tc,),
            in_specs=[pl.BlockSpec((tc, d), lambda i, ds: (i, 0)),
                      pl.BlockSpec((tc, ROUTER_LANES), lambda i, ds: (i, 0)),
                      pl.BlockSpec((1, 1, d), lambda i, ds: (seg(i) * N_MOD + k_gate, 0, 0)),
                      pl.BlockSpec(memory_space=pl.ANY)],
            out_specs=pl.BlockSpec((tc, d), lambda i, ds: (i, 0)),
            scratch_shapes=[pltpu.VMEM((2, 2, tc, d), F32), pltpu.SemaphoreType.DMA((2,))]),
        out_shape=jax.ShapeDtypeStruct((t, d), F32),
        compiler_params=_params("arbitrary"),
        name="moe_combine",
    )(dest, x, routes, mods, y)


def _router(h, w_r, b_r, tm=512):
    t, d = h.shape
    e = w_r.shape[1]
    w_pad = jnp.zeros((d, ROUTER_LANES), BF16).at[:, :e].set(w_r.astype(BF16))
    b_pad = jnp.full((1, ROUTER_LANES), -jnp.inf, F32).at[0, :e].set(b_r)
    return pl.pallas_call(
        _router_kernel,
        grid=(t // tm,),
        in_specs=[pl.BlockSpec((tm, d), lambda i: (i, 0)),
                  pl.BlockSpec((d, ROUTER_LANES), lambda i: (0, 0)),
                  pl.BlockSpec((1, ROUTER_LANES), lambda i: (0, 0))],
        out_specs=pl.BlockSpec((tm, ROUTER_LANES), lambda i: (i, 0)),
        out_shape=jax.ShapeDtypeStruct((t, ROUTER_LANES), F32),
        compiler_params=_params("parallel"),
        name="router",
    )(h, w_pad, b_pad)


def kernel(x, c, ctx, c_ctx, ada_down, ada_up, ada_bias, norm1_g, norm2_g, w_in, sgu_norm_g, sgu_w,
           sgu_b, na_rpb, qk_norm_g, group_norm_g, w_out, ffn_w1, ffn_w3, ffn_w2, moe_router,
           moe_router_b, moe_w1, moe_w3, moe_w2, final_norm_g):
    n_batch, n_lat, d = x.shape
    n_ctx = ctx.shape[1]
    depth = w_in.shape[0]
    mix = w_out.shape[1]
    wa = wb = mix // 4
    wc = mix - wa - wb
    n_b_heads = wb // HEAD_DIM
    n_c_heads = wc // HEAD_DIM
    n_kv_heads = n_c_heads // 4
    wkv = n_kv_heads * HEAD_DIM
    group = n_c_heads // n_kv_heads
    n_experts = moe_router.shape[-1]
    dims = dict(n_lat=n_lat, n_batch=n_batch)
    lat_rows = n_batch * n_lat

    off_bq, off_bk, off_bv = 2 * wa, 2 * wa + wb, 2 * wa + 2 * wb
    src_cq = 2 * wa + 3 * wb
    off_ck = src_cq
    off_cv = off_ck + wkv
    off_cq = off_cv + wkv
    in_width = off_cq + wc
    tn_in = math.gcd(512, wkv)
    nb_pre, nb_kv, nb_q = src_cq // tn_in, 2 * wkv // tn_in, wc // tn_in

    def in_col_block(j):
        return jnp.where(j < nb_pre, j, jnp.where(j < nb_pre + nb_kv, j + nb_q, j - nb_kv))

    in_col_scale = jnp.ones((1, in_width), F32).at[:, off_bq:off_bk].set(QK_PRESCALE)

    xa = jnp.concatenate([x.reshape(lat_rows, d), ctx.reshape(n_batch * n_ctx, d)], axis=0)
    cond = jnp.zeros((8, d), F32).at[:n_batch].set(c).at[n_batch].set(c_ctx)
    mods_all = _ada_mods(cond, ada_down, ada_up, ada_bias)
    cos_t, sin_t = _rope_tables(n_lat, n_ctx)
    w_out_b, ffn_w2_b = w_out.astype(BF16), ffn_w2.astype(BF16)
    moe_w1_b, moe_w3_b, moe_w2_b = moe_w1.astype(BF16), moe_w3.astype(BF16), moe_w2.astype(BF16)

    for l in range(depth):
        mods = mods_all[l].reshape(8 * N_MOD, 1, d)
        h = _norm_mod(xa, norm1_g[l], mods, 0, 1, out_dtype=BF16, **dims)
        p = _matmul_w32(h, w_in, l, in_col_scale, in_col_block, out_dtype=BF16, tn=tn_in)
        ya = _sgu(p, sgu_norm_g[l], sgu_w[l], sgu_b[l], group_norm_g[l, :wa], width=wa)
        bias, pair_idx = _na_bias_tables(na_rpb[l], n_lat // GRID_W)
        yb_lat = _neighbourhood_attention(
            p, bias, pair_idx, n_batch=n_batch, n_lat=n_lat, n_ctx=n_ctx, q_col=off_bq // HEAD_DIM,
            k_col=off_bk // HEAD_DIM, v_col=off_bv // HEAD_DIM, n_heads=n_b_heads)
        kb_ctx = p[lat_rows:, off_bk:off_bk + wb].reshape(n_batch, n_ctx, wb)
        vb_ctx = p[lat_rows:, off_bv:off_bv + wb].reshape(n_batch, n_ctx, wb)
        yb_ctx = _flash(p, kb_ctx, vb_ctx, n_batch=n_batch, n_q_rows=n_ctx, q_row0=lat_rows,
                        q_rows_per_batch=n_ctx, q_col=off_bq // HEAD_DIM, n_kv_heads=n_b_heads,
                        group=1, kv_row0=0, n_kv_rows=n_ctx)
        qc, kc, vc = _qk_prep(p, cos_t, sin_t, qk_norm_g[l, 0], qk_norm_g[l, 1], n_batch=n_batch,
                              n_lat=n_lat, n_ctx=n_ctx, q_off=off_cq // wc, n_q=n_c_heads,
                              k_off=off_ck // wkv, v_off=off_cv // wkv, n_kv=n_kv_heads)
        yc_lat = _flash_gqa(qc, kc, vc, n_batch=n_batch, n_q_rows=n_lat, q_row0=0,
                            q_rows_per_batch=n_lat, n_kv_heads=n_kv_heads, group=group,
                            kv_row0=0, n_kv_rows=n_lat + n_ctx)
        yc_ctx = _flash_gqa(qc, kc, vc, n_batch=n_batch, n_q_rows=n_ctx, q_row0=lat_rows,
                            q_rows_per_batch=n_ctx, n_kv_heads=n_kv_heads, group=group,
                            kv_row0=n_lat, n_kv_rows=n_ctx)
        merged = _merge(ya, yb_lat, yb_ctx, yc_lat, yc_ctx,
                        group_norm_g[l, wa:wa + wb], group_norm_g[l, wa + wb:])
        xa = _matmul_residual(merged, w_out_b, l, xa, mods, 2, tn=1024, **dims)
        j = l // 2
        if l % 2 == 0:
            h2 = _norm_mod(xa, norm2_g[l], mods, 3, 4, out_dtype=BF16, **dims)
            hid = _matmul_swiglu_w32(h2, ffn_w1, ffn_w3, j)
            xa = _matmul_residual(hid, ffn_w2_b, j, xa, mods, 5, **dims)
        else:
            h2 = _norm_mod(xa, norm2_g[l], mods, 3, 4, out_dtype=F32, **dims)
            routes = _router(h2, moe_router[j], moe_router_b[j])
            src, dest, tile_expert, tile_valid = _route_plan(routes, n_experts, MOE_TILE)
            xs = _gather_rows(h2, src, out_dtype=BF16)
            ys = _moe_experts(xs, moe_w1_b, moe_w3_b, moe_w2_b, j, tile_expert, tile_valid,
                              tm=MOE_TILE)
            xa = _moe_combine(xa, ys, routes, dest, mods, 5, **dims)

    out = _final_norm(xa, final_norm_g, lat_rows)
    return out.reshape(n_batch, n_lat, d)
```

```python
import functools
import math

import numpy as np
import jax
import jax.numpy as jnp
from jax import lax
from jax.experimental import pallas as pl
from jax.experimental.pallas import tpu as pltpu

F32 = jnp.float32
BF16 = jnp.bfloat16

HEAD_DIM = 128
GRID_W = 64
CHUNK = 128
NA_ROWS = 8
NA_COLS = 16
ROPE_THETA = 10000.0
EPS = 1e-6
N_MOD = 6
MASK_VALUE = -1e30
ROUTER_LANES = 128
MOE_TILE = 512
FLASH_LOOKAHEAD = 3
ROW_TILES = (1536, 1024, 512, 256)
NA_Q_ROWS = 8
NA_BAND_ROWS = 16
LOG2E = 1.4426950408889634
QK_PRESCALE = HEAD_DIM ** -0.5 * LOG2E
V7X_VMEM_BYTES = 64 * 1024 * 1024
VMEM_LIMIT = V7X_VMEM_BYTES - 8 * 1024 * 1024


def _params(*sem):
    return pltpu.CompilerParams(dimension_semantics=sem, vmem_limit_bytes=VMEM_LIMIT)


def _rms(x, g):
    return x * lax.rsqrt(jnp.mean(x * x, axis=-1, keepdims=True) + EPS) * g


def _ada_down_kernel(c_ref, w_ref, o_ref):
    c = c_ref[...]
    a = (c * jax.nn.sigmoid(c)).astype(BF16)
    o_ref[0] = jnp.dot(a, w_ref[0].astype(BF16), preferred_element_type=F32)


def _ada_up_kernel(t_ref, w_ref, b_ref, o_ref):
    t = t_ref[0].astype(BF16)
    o_ref[0] = jnp.dot(t, w_ref[0].astype(BF16), preferred_element_type=F32) + b_ref[0]


def _ada_mods(cond, w_down, w_up, b_up):
    depth, d, r = w_down.shape
    n_out = w_up.shape[2]
    rows = cond.shape[0]
    tn1 = min(r, 512)
    t = pl.pallas_call(
        _ada_down_kernel,
        grid=(depth, r // tn1),
        in_specs=[pl.BlockSpec((rows, d), lambda l, j: (0, 0)),
                  pl.BlockSpec((1, d, tn1), lambda l, j: (l, 0, j))],
        out_specs=pl.BlockSpec((1, rows, tn1), lambda l, j: (l, 0, j)),
        out_shape=jax.ShapeDtypeStruct((depth, rows, r), F32),
        compiler_params=_params("parallel", "parallel"),
        name="ada_down",
    )(cond, w_down)
    tn2 = min(n_out, 2048)
    return pl.pallas_call(
        _ada_up_kernel,
        grid=(depth, n_out // tn2),
        in_specs=[pl.BlockSpec((1, rows, r), lambda l, j: (l, 0, 0)),
                  pl.BlockSpec((1, r, tn2), lambda l, j: (l, 0, j)),
                  pl.BlockSpec((1, 1, tn2), lambda l, j: (l, 0, j))],
        out_specs=pl.BlockSpec((1, rows, tn2), lambda l, j: (l, 0, j)),
        out_shape=jax.ShapeDtypeStruct((depth, rows, n_out), F32),
        compiler_params=_params("parallel", "parallel"),
        name="ada_up",
    )(t, w_up, b_up.reshape(depth, 1, n_out))


def _norm_mod_kernel(x_ref, g_ref, sh_ref, sc_ref, o_ref):
    y = _rms(x_ref[...], g_ref[...])
    o_ref[...] = (y * (1.0 + sc_ref[0]) + sh_ref[0]).astype(o_ref.dtype)


def _norm_kernel(x_ref, g_ref, o_ref):
    o_ref[...] = _rms(x_ref[...], g_ref[...]).astype(o_ref.dtype)


def _seg_fn(tm, n_lat, n_batch):
    return lambda i: jnp.minimum((i * tm) // n_lat, n_batch)


def _norm_mod(x, g, mods, k_shift, k_scale, *, n_lat, n_batch, out_dtype, tm=256):
    t, d = x.shape
    seg = _seg_fn(tm, n_lat, n_batch)
    return pl.pallas_call(
        _norm_mod_kernel,
        grid=(t // tm,),
        in_specs=[pl.BlockSpec((tm, d), lambda i: (i, 0)),
                  pl.BlockSpec((1, d), lambda i: (0, 0)),
                  pl.BlockSpec((1, 1, d), lambda i: (seg(i) * N_MOD + k_shift, 0, 0)),
                  pl.BlockSpec((1, 1, d), lambda i: (seg(i) * N_MOD + k_scale, 0, 0))],
        out_specs=pl.BlockSpec((tm, d), lambda i: (i, 0)),
        out_shape=jax.ShapeDtypeStruct((t, d), out_dtype),
        compiler_params=_params("parallel"),
        name="norm_mod",
    )(x, g.reshape(1, d), mods, mods)


def _final_norm(x, g, rows, tm=256):
    d = x.shape[1]
    return pl.pallas_call(
        _norm_kernel,
        grid=(rows // tm,),
        in_specs=[pl.BlockSpec((tm, d), lambda i: (i, 0)),
                  pl.BlockSpec((1, d), lambda i: (0, 0))],
        out_specs=pl.BlockSpec((tm, d), lambda i: (i, 0)),
        out_shape=jax.ShapeDtypeStruct((rows, d), F32),
        compiler_params=_params("parallel"),
        name="final_norm",
    )(x, g.reshape(1, d))


def _mmw_kernel(a_ref, w_ref, cs_ref, o_ref, wb_sc):
    @pl.when(pl.program_id(1) == 0)
    def _():
        wb_sc[...] = (w_ref[...] * cs_ref[...]).astype(BF16)

    o_ref[...] = jnp.dot(a_ref[...], wb_sc[...], preferred_element_type=F32).astype(o_ref.dtype)


def _mmw_swiglu_kernel(a_ref, w1_ref, w3_ref, o_ref, w1_sc, w3_sc):
    @pl.when(pl.program_id(1) == 0)
    def _():
        w1_sc[...] = w1_ref[...].astype(BF16)
        w3_sc[...] = w3_ref[...].astype(BF16)

    a = a_ref[...]
    h1 = jnp.dot(a, w1_sc[...], preferred_element_type=F32)
    h3 = jnp.dot(a, w3_sc[...], preferred_element_type=F32)
    o_ref[...] = (h1 * jax.nn.sigmoid(h1) * h3).astype(o_ref.dtype)


def _mm_res_kernel(a_ref, b_ref, x_ref, g_ref, o_ref):
    acc = jnp.dot(a_ref[...], b_ref[...], preferred_element_type=F32)
    o_ref[...] = x_ref[...] + g_ref[0] * acc


def _pick_tile(n, candidates):
    for c in candidates:
        if n % c == 0:
            return c
    raise ValueError(f"no tile for {n}")


def _matmul_w32(a, w, layer, col_scale, col_block, *, out_dtype, tn):
    m, k = a.shape
    n = w.shape[2]
    tm = _pick_tile(m, ROW_TILES)
    return pl.pallas_call(
        _mmw_kernel,
        grid=(n // tn, m // tm),
        in_specs=[pl.BlockSpec((tm, k), lambda j, i: (i, 0)),
                  pl.BlockSpec((None, k, tn), lambda j, i: (layer, 0, col_block(j))),
                  pl.BlockSpec((1, tn), lambda j, i: (0, col_block(j)))],
        out_specs=pl.BlockSpec((tm, tn), lambda j, i: (i, j)),
        out_shape=jax.ShapeDtypeStruct((m, n), out_dtype),
        scratch_shapes=[pltpu.VMEM((k, tn), BF16)],
        compiler_params=_params("parallel", "arbitrary"),
        name="matmul_w32",
    )(a, w, col_scale)


def _matmul_swiglu_w32(a, w1, w3, layer, *, tn=256):
    m, k = a.shape
    n = w1.shape[2]
    tn = min(tn, n)
    tm = _pick_tile(m, ROW_TILES)
    return pl.pallas_call(
        _mmw_swiglu_kernel,
        grid=(n // tn, m // tm),
        in_specs=[pl.BlockSpec((tm, k), lambda j, i: (i, 0)),
                  pl.BlockSpec((None, k, tn), lambda j, i: (layer, 0, j)),
                  pl.BlockSpec((None, k, tn), lambda j, i: (layer, 0, j))],
        out_specs=pl.BlockSpec((tm, tn), lambda j, i: (i, j)),
        out_shape=jax.ShapeDtypeStruct((m, n), BF16),
        scratch_shapes=[pltpu.VMEM((k, tn), BF16), pltpu.VMEM((k, tn), BF16)],
        compiler_params=_params("parallel", "arbitrary"),
        name="matmul_swiglu_w32",
    )(a, w1, w3)


def _matmul_residual(a, w, layer, x, mods, k_gate, *, n_lat, n_batch, tm=512, tn=512):
    m, k = a.shape
    n = w.shape[2]
    tn = min(tn, n)
    seg = _seg_fn(tm, n_lat, n_batch)
    return pl.pallas_call(
        _mm_res_kernel,
        grid=(m // tm, n // tn),
        in_specs=[pl.BlockSpec((tm, k), lambda i, j: (i, 0)),
                  pl.BlockSpec((None, k, tn), lambda i, j: (layer, 0, j)),
                  pl.BlockSpec((tm, tn), lambda i, j: (i, j)),
                  pl.BlockSpec((1, 1, tn), lambda i, j: (seg(i) * N_MOD + k_gate, 0, j))],
        out_specs=pl.BlockSpec((tm, tn), lambda i, j: (i, j)),
        out_shape=jax.ShapeDtypeStruct((m, n), F32),
        compiler_params=_params("parallel", "arbitrary"),
        name="matmul_residual",
    )(a, w, x, mods)


def _sgu_kernel(u_ref, v_ref, gv_ref, ws_ref, bs_ref, ga_ref, o_ref, *, n_chunks, n_groups):
    for c in range(n_chunks):
        rows = slice(c * CHUNK, (c + 1) * CHUNK)
        u = jax.nn.gelu(u_ref[rows, :].astype(F32), approximate=True)
        v = jax.nn.gelu(v_ref[rows, :].astype(F32), approximate=True)
        vb = _rms(v, gv_ref[...]).astype(BF16)
        parts = [jnp.dot(ws_ref[g], vb[:, g * HEAD_DIM:(g + 1) * HEAD_DIM],
                         preferred_element_type=F32) for g in range(n_groups)]
        y = u * (jnp.concatenate(parts, axis=1) + bs_ref[...])
        o_ref[rows, :] = _rms(y, ga_ref[...]).astype(o_ref.dtype)


def _sgu(p, g_v, w_s, b_s, g_a, *, width, ta=256):
    t = p.shape[0]
    n_groups = width // HEAD_DIM
    bias =jnp.repeat(b_s.T, HEAD_DIM, axis=1)
    return pl.pallas_call(
        functools.partial(_sgu_kernel, n_chunks=ta // CHUNK, n_groups=n_groups),
        grid=(t // ta,),
        in_specs=[pl.BlockSpec((ta, width), lambda i: (i, 0)),
                  pl.BlockSpec((ta, width), lambda i: (i, 1)),
                  pl.BlockSpec((1, width), lambda i: (0, 0)),
                  pl.BlockSpec((n_groups, CHUNK, CHUNK), lambda i: (0, 0, 0)),
                  pl.BlockSpec((CHUNK, width), lambda i: (0, 0)),
                  pl.BlockSpec((1, width), lambda i: (0, 0))],
        out_specs=pl.BlockSpec((ta, width), lambda i: (i, 0)),
        out_shape=jax.ShapeDtypeStruct((t, width), BF16),
        compiler_params=_params("parallel"),
        name="sgu",
    )(p, p, g_v.reshape(1, width), w_s.astype(BF16), bias, g_a.reshape(1, width))


def _na_bias_tables(rpb, rows):
    n_blocks = rows // NA_Q_ROWS
    cols = np.arange(GRID_W)
    col_start = np.clip(cols - NA_COLS // 2, 0, GRID_W - NA_COLS)
    dc = cols[None, :] - cols[:, None]
    col_ok = (cols[None, :] >= col_start[:, None]) & (cols[None, :] < col_start[:, None] + NA_COLS)
    col_idx = np.where(col_ok, dc + NA_COLS - 1, 0)
    n_dr = 2 * NA_ROWS - 1
    t = jnp.where(col_ok[None, None], rpb[:, :, col_idx] * LOG2E, MASK_VALUE)
    t = jnp.concatenate([t, jnp.full((rpb.shape[0], 1, GRID_W, GRID_W), MASK_VALUE, F32)], axis=1)
    slab = np.full((3, NA_Q_ROWS, NA_BAND_ROWS), n_dr, np.int32)
    rep = [0, min(1, n_blocks - 1), n_blocks - 1]
    for ty, blk in enumerate(rep):
        band0 = int(np.clip(blk * NA_Q_ROWS - NA_ROWS // 2, 0, rows - NA_BAND_ROWS))
        for rq in range(NA_Q_ROWS):
            r = blk * NA_Q_ROWS + rq
            start = int(np.clip(r - NA_ROWS // 2, 0, rows - NA_ROWS))
            for kr in range(NA_BAND_ROWS):
                ar = band0 + kr
                if start <= ar < start + NA_ROWS:
                    slab[ty, rq, kr] = ar - r + NA_ROWS - 1
    pairs = slab.reshape(3, NA_Q_ROWS, NA_BAND_ROWS // 2, 2)
    uniq, inverse = np.unique(pairs.reshape(-1, 2), axis=0, return_inverse=True)
    t2 = jnp.concatenate([t[:, uniq[:, 0]], t[:, uniq[:, 1]]], axis=-1)
    return t2, jnp.asarray(inverse.reshape(-1), jnp.int32)


def _na_kernel(pair_ref, q_ref, k_ref, v_ref, kc_ref, vc_ref, bias_ref, o_ref, *, rows, rc, n_blocks):
    i = pl.program_id(2)
    block_type = jnp.where(i == 0, 0, jnp.where(i == n_blocks - 1, 2, 1))
    pairs_per_row = NA_BAND_ROWS // 2

    def bias_rows(rq):
        base = (block_type * NA_Q_ROWS + rq) * pairs_per_row
        return jnp.concatenate([bias_ref[pair_ref[base + m]] for m in range(pairs_per_row)], axis=1)

    band0 = jnp.clip(i * NA_Q_ROWS - NA_ROWS // 2, 0, rows - NA_BAND_ROWS)
    start = pl.multiple_of(band0 * GRID_W, GRID_W)
    nb = NA_BAND_ROWS * GRID_W
    kb = k_ref[pl.ds(start, nb), :]
    vb = v_ref[pl.ds(start, nb), :]
    kc = kc_ref[...]
    vc = vc_ref[...]
    dn = (((1,), (1,)), ((), ()))
    n_chunks = q_ref.shape[0] // rc
    n_loc = nb // HEAD_DIM

    def scores(c):
        q = q_ref[c * rc:(c + 1) * rc, :]
        return (lax.dot_general(q, kb, dn, preferred_element_type=F32),
                lax.dot_general(q, kc, dn, preferred_element_type=F32))

    nxt = scores(0)
    for c in range(n_chunks):
        s_loc, s_ctx = nxt
        if c + 1 < n_chunks:
            nxt = scores(c + 1)
        q_rows = rc // GRID_W
        s_loc = s_loc + jnp.concatenate([bias_rows(c * q_rows + r) for r in range(q_rows)], axis=0)
        cols = ([s_loc[:, j * HEAD_DIM:(j + 1) * HEAD_DIM] for j in range(n_loc)]
                + [s_ctx[:, j * HEAD_DIM:(j + 1) * HEAD_DIM] for j in range(s_ctx.shape[1] // HEAD_DIM)])
        m = jnp.max(functools.reduce(jnp.maximum, cols), axis=1, keepdims=True)
        ps = [jnp.exp2(x - m) for x in cols]
        l = jnp.sum(functools.reduce(jnp.add, ps), axis=1, keepdims=True)
        p_loc = jnp.concatenate([x.astype(BF16) for x in ps[:n_loc]], axis=1)
        p_ctx = jnp.concatenate([x.astype(BF16) for x in ps[n_loc:]], axis=1)
        o = (jnp.dot(p_loc, vb, preferred_element_type=F32)
             + jnp.dot(p_ctx, vc, preferred_element_type=F32))
        o_ref[c * rc:(c + 1) * rc, :] = (o / l).astype(o_ref.dtype)


def _neighbourhood_attention(p, bias, pair_idx, *, n_batch, n_lat, n_ctx, q_col, k_col, v_col, n_heads):
    rows = n_lat // GRID_W
    tq = NA_Q_ROWS * GRID_W
    n_blocks = n_lat // tq
    ctx_blk0 = (n_batch * n_lat) // n_ctx
    n_pairs = bias.shape[1]
    return pl.pallas_call(
        functools.partial(_na_kernel, rows=rows, rc=min(256, tq), n_blocks=n_blocks),
        grid_spec=pltpu.PrefetchScalarGridSpec(
            num_scalar_prefetch=1, grid=(n_heads, n_batch, n_blocks),
            in_specs=[pl.BlockSpec((tq, HEAD_DIM), lambda h, b, i, pr: (b * n_blocks + i, q_col + h)),
                      pl.BlockSpec((n_lat, HEAD_DIM), lambda h, b, i, pr: (b, k_col + h)),
                      pl.BlockSpec((n_lat, HEAD_DIM), lambda h, b, i, pr: (b, v_col + h)),
                      pl.BlockSpec((n_ctx, HEAD_DIM), lambda h, b, i, pr: (ctx_blk0 + b, k_col + h)),
                      pl.BlockSpec((n_ctx, HEAD_DIM), lambda h, b, i, pr: (ctx_blk0 + b, v_col + h)),
                      pl.BlockSpec((None, n_pairs, GRID_W, 2 * GRID_W), lambda h, b, i, pr: (h, 0, 0, 0))],
            out_specs=pl.BlockSpec((tq, HEAD_DIM), lambda h, b, i, pr: (b * n_blocks + i, h))),
        out_shape=jax.ShapeDtypeStruct((n_batch * n_lat, n_heads * HEAD_DIM), BF16),
        compiler_params=_params("parallel", "parallel", "arbitrary"),
        name="neighbourhood_attention",
    )(pair_idx, p, p, p, p, p, bias)


def _rope_tables(n_lat, n_ctx):
    t = jnp.arange(n_lat, dtype=jnp.int32)
    pos = jnp.stack([t // GRID_W, t % GRID_W], axis=-1).astype(F32)
    n_freq = HEAD_DIM // 4
    inv = 1.0 / (ROPE_THETA ** (jnp.arange(n_freq, dtype=F32) / n_freq))
    ang = pos[:, :, None] * inv
    cos, sin = jnp.cos(ang), jnp.sin(ang)
    cos_t = jnp.concatenate([cos[:, 0], cos[:, 0], cos[:, 1], cos[:, 1]], axis=-1)
    sin_t = jnp.concatenate([-sin[:, 0], sin[:, 0], -sin[:, 1], sin[:, 1]], axis=-1)
    cos_t = jnp.concatenate([cos_t, jnp.ones((n_ctx, HEAD_DIM), F32)], axis=0)
    sin_t = jnp.concatenate([sin_t, jnp.zeros((n_ctx, HEAD_DIM), F32)], axis=0)
    return cos_t, sin_t


def _qk_prep_kernel(q_ref, k_ref, v_ref, cos_ref, sin_ref, gq_ref, gk_ref,
                    qo_ref, ko_ref, vo_ref, *, n_q, n_kv):
    cos = cos_ref[...]
    sin = sin_ref[...]
    quarter = HEAD_DIM // 4
    lane = lax.broadcasted_iota(jnp.int32, cos.shape, 1)
    first_half = (lane % (2 * quarter)) < quarter

    def prep(x, g, out_scale):
        y = _rms(x.astype(F32), g)
        swapped = jnp.where(first_half,
                            pltpu.roll(y, HEAD_DIM - quarter, 1),
                            pltpu.roll(y, quarter, 1))
        return ((y * cos + swapped * sin) * out_scale).astype(BF16)

    for h in range(n_q):
        cs = slice(h * HEAD_DIM, (h + 1) * HEAD_DIM)
        qo_ref[:, cs] = prep(q_ref[:, cs], gq_ref[...], QK_PRESCALE)
    ones = jnp.ones((v_ref.shape[0], HEAD_DIM), BF16)
    for h in range(n_kv):
        cs = slice(h * HEAD_DIM, (h + 1) * HEAD_DIM)
        ko_ref[0, :, cs] = prep(k_ref[:, cs], gk_ref[...], 1.0)
        vo_ref[0, :, 2 * h * HEAD_DIM:(2 * h + 1) * HEAD_DIM] = v_ref[:, cs]
        vo_ref[0, :, (2 * h + 1) * HEAD_DIM:(2 * h + 2) * HEAD_DIM] = ones


def _qk_prep(p, cos_t, sin_t, g_q, g_k, *, n_batch, n_lat, n_ctx, q_off, n_q, k_off, v_off, n_kv):
    t = p.shape[0]
    tm = n_ctx
    lat_tiles = n_lat // tm
    n_lat_tiles = n_batch * lat_tiles

    def pos_blk(i):
        return jnp.where(i < n_lat_tiles, i % lat_tiles, lat_tiles)

    def kv_blk(i):
        c = i - n_lat_tiles
        return (jnp.where(i < n_lat_tiles, i // lat_tiles, c),
                jnp.where(i < n_lat_tiles, i % lat_tiles, lat_tiles), 0)

    wq, wkv = n_q * HEAD_DIM, n_kv * HEAD_DIM
    return pl.pallas_call(
        functools.partial(_qk_prep_kernel, n_q=n_q, n_kv=n_kv),
        grid=(t // tm,),
        in_specs=[pl.BlockSpec((tm, wq), lambda i: (i, q_off)),
                  pl.BlockSpec((tm, wkv), lambda i: (i, k_off)),
                  pl.BlockSpec((tm, wkv), lambda i: (i, v_off)),
                  pl.BlockSpec((tm, HEAD_DIM), lambda i: (pos_blk(i), 0)),
                  pl.BlockSpec((tm, HEAD_DIM), lambda i: (pos_blk(i), 0)),
                  pl.BlockSpec((1, HEAD_DIM), lambda i: (0, 0)),
                  pl.BlockSpec((1, HEAD_DIM), lambda i: (0, 0))],
        out_specs=[pl.BlockSpec((tm, wq), lambda i: (i, 0)),
                   pl.BlockSpec((1, tm, wkv), kv_blk),
                   pl.BlockSpec((1, tm, 2 * wkv), kv_blk)],
        out_shape=[jax.ShapeDtypeStruct((t, wq), BF16),
                   jax.ShapeDtypeStruct((n_batch, n_lat + n_ctx, wkv), BF16),
                   jax.ShapeDtypeStruct((n_batch, n_lat + n_ctx, 2 * wkv), BF16)],
        compiler_params=_params("parallel"),
        name="qk_prep",
    )(p, p, p, cos_t, sin_t, g_q.reshape(1, HEAD_DIM), g_k.reshape(1, HEAD_DIM))


def _flash_kernel(q_ref, k_ref, v_ref, o_ref, m_sc, l_sc, acc_sc, *, group, tq):
    kj = pl.program_id(3)

    @pl.when(kj == 0)
    def _():
        m_sc[...] = jnp.full_like(m_sc, -jnp.inf)
        l_sc[...] = jnp.zeros_like(l_sc)
        acc_sc[...] = jnp.zeros_like(acc_sc)

    if group == 1:
        q = q_ref[...]
    else:
        q = jnp.concatenate([q_ref[:, g * HEAD_DIM:(g + 1) * HEAD_DIM] for g in range(group)], axis=0)
    s = lax.dot_general(q, k_ref[0], (((1,), (1,)), ((), ())), preferred_element_type=F32)
    m_prev = m_sc[...]
    m_new = jnp.maximum(m_prev, jnp.max(s, axis=1, keepdims=True))
    alpha = jnp.exp2(m_prev - m_new)
    p = jnp.exp2(s - m_new)
    l_sc[...] = alpha * l_sc[...] + jnp.sum(p, axis=1, keepdims=True)
    acc_sc[...] = alpha * acc_sc[...] + jnp.dot(p.astype(BF16), v_ref[0], preferred_element_type=F32)
    m_sc[...] = m_new

    @pl.when(kj == pl.num_programs(3) - 1)
    def _():
        o = acc_sc[...] / l_sc[...]
        for g in range(group):
            o_ref[:, g * HEAD_DIM:(g + 1) * HEAD_DIM] = o[g * tq:(g + 1) * tq].astype(o_ref.dtype)


def _flash(q2d, k3d, v3d, *, n_batch, n_q_rows, q_row0, q_rows_per_batch, q_col, n_kv_heads,
           group, kv_row0, n_kv_rows, tq=256):
    tq = min(tq, n_q_rows)
    tk = _pick_tile(n_kv_rows, (1024, 768, 512, 256))
    qb0, qbb = q_row0 // tq, q_rows_per_batch // tq
    kb0 = kv_row0 // tk
    gw = group * HEAD_DIM
    return pl.pallas_call(
        functools.partial(_flash_kernel, group=group, tq=tq),
        grid=(n_batch, n_kv_heads, n_q_rows // tq, n_kv_rows // tk),
        in_specs=[pl.BlockSpec((tq, gw), lambda b, h, i, j: (qb0 + b * qbb + i, q_col + h)),
                  pl.BlockSpec((1, tk, HEAD_DIM), lambda b, h, i, j: (b, kb0 + j, h)),
                  pl.BlockSpec((1, tk, HEAD_DIM), lambda b, h, i, j: (b, kb0 + j, h))],
        out_specs=pl.BlockSpec((tq, gw), lambda b, h, i, j: (b * (n_q_rows // tq) + i, h)),
        out_shape=jax.ShapeDtypeStruct((n_batch * n_q_rows, n_kv_heads * gw), BF16),
        scratch_shapes=[pltpu.VMEM((group * tq, 1), F32),
                        pltpu.VMEM((group * tq, 1), F32),
                        pltpu.VMEM((group * tq, HEAD_DIM), F32)],
        compiler_params=_params("parallel", "parallel", "parallel", "arbitrary"),
        name="flash_attention",
    )(q2d, k3d, v3d)


def _flash_gqa_kernel(q_ref, k_ref, v_ref, o_ref, m_sc, acc_sc, *, group, tq, tk, rc):
    kj = pl.program_id(3)

    @pl.when(kj == 0)
    def _():
        m_sc[...] = jnp.full_like(m_sc, -jnp.inf)
        acc_sc[...] = jnp.zeros_like(acc_sc)

    n_cols = tk // HEAD_DIM
    k = k_ref[0]
    v = v_ref[0]
    chunks = [(g, r0) for g in range(group) for r0 in range(0, tq, rc)]

    def scores(chunk):
        g, r0 = chunk
        q = q_ref[r0:r0 + rc, g * HEAD_DIM:(g + 1) * HEAD_DIM]
        return lax.dot_general(q, k, (((1,), (1,)), ((), ())), preferred_element_type=F32)

    pending = [scores(ch) for ch in chunks[:FLASH_LOOKAHEAD]]
    for ci, (g, r0) in enumerate(chunks):
        if ci + FLASH_LOOKAHEAD < len(chunks):
            pending.append(scores(chunks[ci + FLASH_LOOKAHEAD]))
        s = pending.pop(0)
        rows = slice(g * tq + r0, g * tq + r0 + rc)
        cols = [s[:, i * HEAD_DIM:(i + 1) * HEAD_DIM] for i in range(n_cols)]
        col_max = functools.reduce(jnp.maximum, cols)
        m_prev = m_sc[rows, :]
        m_new = jnp.maximum(m_prev, jnp.max(col_max, axis=1, keepdims=True))
        alpha = jnp.exp2(m_prev - m_new)
        p = jnp.concatenate([jnp.exp2(c - m_new).astype(BF16) for c in cols], axis=1)
        pv = jnp.dot(p, v, preferred_element_type=F32)
        acc_sc[rows, :] = jnp.concatenate([alpha, alpha], axis=1) * acc_sc[rows, :] + pv
        m_sc[rows, :] = m_new

    @pl.when(kj == pl.num_programs(3) - 1)
    def _():
        o = acc_sc[:, :HEAD_DIM] / acc_sc[:, HEAD_DIM:]
        for g in range(group):
            o_ref[:, g * HEAD_DIM:(g + 1) * HEAD_DIM] = o[g * tq:(g + 1) * tq].astype(o_ref.dtype)


def _flash_gqa(q2d, k3d, v3d, *, n_batch, n_q_rows, q_row0, q_rows_per_batch, n_kv_heads, group,
               kv_row0, n_kv_rows, tq=512, rc=256):
    tq = min(tq, n_q_rows)
    rc = min(rc, tq)
    tk = _pick_tile(n_kv_rows, (2816, 1408, 1024, 768, 512, 256))
    qb0, qbb = q_row0 // tq, q_rows_per_batch // tq
    kb0 = kv_row0 // tk
    gw = group * HEAD_DIM
    return pl.pallas_call(
        functools.partial(_flash_gqa_kernel, group=group, tq=tq, tk=tk, rc=rc),
        grid=(n_batch, n_kv_heads, n_q_rows // tq, n_kv_rows // tk),
        in_specs=[pl.BlockSpec((tq, gw), lambda b, h, i, j: (qb0 + b * qbb + i, h)),
                  pl.BlockSpec((1, tk, HEAD_DIM), lambda b, h, i, j: (b, kb0 + j, h)),
                  pl.BlockSpec((1, tk, 2 * HEAD_DIM), lambda b, h, i, j: (b, kb0 + j, h))],
        out_specs=pl.BlockSpec((tq, gw), lambda b, h, i, j: (b * (n_q_rows // tq) + i, h)),
        out_shape=jax.ShapeDtypeStruct((n_batch * n_q_rows, n_kv_heads * gw), BF16),
        scratch_shapes=[pltpu.VMEM((group * tq, HEAD_DIM), F32),
                        pltpu.VMEM((group * tq, 2 * HEAD_DIM), F32)],
        compiler_params=_params("parallel", "parallel", "parallel", "arbitrary"),
        name="flash_gqa",
    )(q2d, k3d, v3d)


def _merge_kernel(ya_ref, yb_lat_ref, yb_ctx_ref, yc_lat_ref, yc_ctx_ref, gb_ref, gc_ref, o_ref,
                  *, wa, wb, n_lat_tiles):
    o_ref[:, :wa] = ya_ref[...]

    def emit(yb_ref, yc_ref):
        o_ref[:, wa:wa + wb] = _rms(yb_ref[...].astype(F32), gb_ref[...]).astype(o_ref.dtype)
        o_ref[:, wa + wb:] = _rms(yc_ref[...].astype(F32), gc_ref[...]).astype(o_ref.dtype)

    is_lat = pl.program_id(0) < n_lat_tiles
    pl.when(is_lat)(lambda: emit(yb_lat_ref, yc_lat_ref))
    pl.when(jnp.logical_not(is_lat))(lambda: emit(yb_ctx_ref, yc_ctx_ref))


def _merge(ya, yb_lat, yb_ctx, yc_lat, yc_ctx, g_b, g_c, tm=256):
    t, wa = ya.shape
    wb, wc = yb_lat.shape[1], yc_lat.shape[1]
    n_lat_tiles = yb_lat.shape[0] // tm

    def lat(i):
        return (jnp.minimum(i, n_lat_tiles - 1), 0)

    def ctx(i):
        return (jnp.maximum(i - n_lat_tiles, 0), 0)

    return pl.pallas_call(
        functools.partial(_merge_kernel, wa=wa, wb=wb, n_lat_tiles=n_lat_tiles),
        grid=(t // tm,),
        in_specs=[pl.BlockSpec((tm, wa), lambda i: (i, 0)),
                  pl.BlockSpec((tm, wb), lat), pl.BlockSpec((tm, wb), ctx),
                  pl.BlockSpec((tm, wc), lat), pl.BlockSpec((tm, wc), ctx),
                  pl.BlockSpec((1, wb), lambda i: (0, 0)),
                  pl.BlockSpec((1, wc), lambda i: (0, 0))],
        out_specs=pl.BlockSpec((tm, wa + wb + wc), lambda i: (i, 0)),
        out_shape=jax.ShapeDtypeStruct((t, wa + wb + wc), BF16),
        compiler_params=_params("parallel"),
        name="merge_groups",
    )(ya, yb_lat, yb_ctx, yc_lat, yc_ctx, g_b.reshape(1, wb), g_c.reshape(1, wc))


def _router_kernel(h_ref, w_ref, b_ref, o_ref):
    logits = jnp.dot(h_ref[...].astype(BF16), w_ref[...], preferred_element_type=F32) + b_ref[...]
    lane = lax.broadcasted_iota(jnp.int32, logits.shape, 1).astype(F32)
    m1 = jnp.max(logits, axis=1, keepdims=True)
    i1 = jnp.min(jnp.where(logits == m1, lane, float(ROUTER_LANES)), axis=1, keepdims=True)
    rest = jnp.where(lane == i1, -jnp.inf, logits)
    m2 = jnp.max(rest, axis=1, keepdims=True)
    i2 = jnp.min(jnp.where(rest == m2, lane, float(ROUTER_LANES)), axis=1, keepdims=True)
    e2 = jnp.exp(m2 - m1)
    g1 = 1.0 / (1.0 + e2)
    o_ref[...] = jnp.where(lane == 0.0, i1,
                           jnp.where(lane == 1.0, i2,
                                     jnp.where(lane == 2.0, g1,
                                               jnp.where(lane == 3.0, e2 * g1, 0.0))))


def _route_plan(routes, n_experts, tm):
    t = routes.shape[0]
    a = 2 * t
    flat_e = routes[:, :2].astype(jnp.int32).reshape(a)
    onehot = (flat_e[:, None] == jnp.arange(n_experts, dtype=jnp.int32)[None, :]).astype(jnp.int32)
    csum = jnp.cumsum(onehot, axis=0)
    counts = csum[-1]
    rank = jnp.take_along_axis(csum, flat_e[:, None], axis=1)[:, 0] - 1
    padded = ((counts + tm - 1) // tm) * tm
    ends = jnp.cumsum(padded)
    starts = ends - padded
    dest = starts[flat_e] + rank
    ustarts = jnp.cumsum(counts) - counts
    order = jnp.argsort(flat_e, stable=True).astype(jnp.int32)
    n_rows = a + n_experts * tm
    row = jnp.arange(n_rows, dtype=jnp.int32)
    row_e = jnp.minimum(jnp.searchsorted(ends, row, side="right"), n_experts - 1).astype(jnp.int32)
    k = row - starts[row_e]
    src_a = order[jnp.clip(ustarts[row_e] + k, 0, a - 1)]
    src = jnp.where(k < counts[row_e], src_a // 2, 0).astype(jnp.int32)
    tile_start = jnp.arange(n_rows // tm, dtype=jnp.int32) * tm
    tile_valid = (tile_start < ends[-1]).astype(jnp.int32)
    tile_expert = row_e[jnp.minimum(tile_start, ends[-1] - 1)]
    return src, dest.astype(jnp.int32), tile_expert, tile_valid


def _gather_rows_kernel(src_ref, h_hbm, o_ref, buf, sem, *, tg):
    i = pl.program_id(0)

    def copy(tile, j):
        slot = tile % 2
        return pltpu.make_async_copy(h_hbm.at[pl.ds(src_ref[tile * tg + j], 1)],
                                     buf.at[slot, pl.ds(j, 1)], sem.at[slot])

    def issue(tile):
        def body(j, carry):
            copy(tile, j).start()
            return carry
        lax.fori_loop(0, tg, body, 0, unroll=8)

    pl.when(i == 0)(lambda: issue(0))
    pl.when(i + 1 < pl.num_programs(0))(lambda: issue(i + 1))

    slot = i % 2
    pltpu.make_async_copy(h_hbm.at[pl.ds(0, tg)], buf.at[slot], sem.at[slot]).wait()
    o_ref[...] = buf[slot].astype(o_ref.dtype)


def _gather_rows(h, src, *, out_dtype, tg=512):
    n = src.shape[0]
    d = h.shape[1]
    return pl.pallas_call(
        functools.partial(_gather_rows_kernel, tg=tg),
        grid_spec=pltpu.PrefetchScalarGridSpec(
            num_scalar_prefetch=1, grid=(n // tg,),
            in_specs=[pl.BlockSpec(memory_space=pl.ANY)],
            out_specs=pl.BlockSpec((tg, d), lambda i, s: (i, 0)),
            scratch_shapes=[pltpu.VMEM((2, tg, d), h.dtype), pltpu.SemaphoreType.DMA((2,))]),
        out_shape=jax.ShapeDtypeStruct((n, d), out_dtype),
        compiler_params=_params("arbitrary"),
        name="moe_gather",
    )(src, h)


def _expert_changed(te_ref, i):
    return jnp.logical_or(i == 0, te_ref[i] != te_ref[jnp.maximum(i - 1, 0)])


def _moe_up_kernel(te_ref, tv_ref, x_ref, w1_ref, w3_ref, o_ref, w1_sc, w3_sc):
    i = pl.program_id(1)
    valid = tv_ref[i] == 1

    @pl.when(jnp.logical_and(valid, _expert_changed(te_ref, i)))
    def _():
        w1_sc[...] = w1_ref[...].astype(BF16)
        w3_sc[...] = w3_ref[...].astype(BF16)

    @pl.when(valid)
    def _():
        a = x_ref[...]
        h1 = jnp.dot(a, w1_sc[...], preferred_element_type=F32)
        h3 = jnp.dot(a, w3_sc[...], preferred_element_type=F32)
        o_ref[...] = (h1 * jax.nn.sigmoid(h1) * h3).astype(o_ref.dtype)

    @pl.when(jnp.logical_not(valid))
    def _():
        o_ref[...] = jnp.zeros_like(o_ref)


def _moe_down_kernel(te_ref, tv_ref, a_ref, w_ref, o_ref, w_sc):
    i = pl.program_id(1)
    valid = tv_ref[i] == 1

    @pl.when(jnp.logical_and(valid, _expert_changed(te_ref, i)))
    def _():
        w_sc[...] = w_ref[...].astype(BF16)

    @pl.when(valid)
    def _():
        o_ref[...] = jnp.dot(a_ref[...], w_sc[...], preferred_element_type=F32)

    @pl.when(jnp.logical_not(valid))
    def _():
        o_ref[...] = jnp.zeros_like(o_ref)


def _moe_experts(xs, w1, w3, w2, layer, tile_expert, tile_valid, *, tm, tn_up=512, tn_down=2048):
    rows, d = xs.shape
    f = w1.shape[-1]
    tn_up, tn_down = min(tn_up, f), min(tn_down, d)
    n_tiles = rows // tm
    hid = pl.pallas_call(
        _moe_up_kernel,
        grid_spec=pltpu.PrefetchScalarGridSpec(
            num_scalar_prefetch=2, grid=(f // tn_up, n_tiles),
            in_specs=[pl.BlockSpec((tm, d), lambda j, i, te, tv: (i, 0)),
                      pl.BlockSpec((None, None, d, tn_up), lambda j, i, te, tv: (layer, te[i], 0, j)),
                      pl.BlockSpec((None, None, d, tn_up), lambda j, i, te, tv: (layer, te[i], 0, j))],
            out_specs=pl.BlockSpec((tm, tn_up), lambda j, i, te, tv: (i, j)),
            scratch_shapes=[pltpu.VMEM((d, tn_up), BF16), pltpu.VMEM((d, tn_up), BF16)]),
        out_shape=jax.ShapeDtypeStruct((rows, f), BF16),
        compiler_params=_params("parallel", "arbitrary"),
        name="moe_up",
    )(tile_expert, tile_valid, xs, w1, w3)
    return pl.pallas_call(
        _moe_down_kernel,
        grid_spec=pltpu.PrefetchScalarGridSpec(
            num_scalar_prefetch=2, grid=(d // tn_down, n_tiles),
            in_specs=[pl.BlockSpec((tm, f), lambda j, i, te, tv: (i, 0)),
                      pl.BlockSpec((None, None, f, tn_down), lambda j, i, te, tv: (layer, te[i], 0, j))],
            out_specs=pl.BlockSpec((tm, tn_down), lambda j, i, te, tv: (i, j)),
            scratch_shapes=[pltpu.VMEM((f, tn_down), BF16)]),
        out_shape=jax.ShapeDtypeStruct((rows, d), F32),
        compiler_params=_params("parallel", "arbitrary"),
        name="moe_down",
    )(tile_expert, tile_valid, hid, w2)


def _moe_combine_kernel(dest_ref, x_ref, r_ref, g_ref, y_hbm, o_ref, ybuf, sem, *, tc):
    i = pl.program_id(0)

    def copy(tile, j, k):
        slot = tile % 2
        return pltpu.make_async_copy(y_hbm.at[pl.ds(dest_ref[(tile * tc + j) * 2 + k], 1)],
                                     ybuf.at[slot, k, pl.ds(j, 1)], sem.at[slot])

    def issue(tile):
        def body(j, carry):
            copy(tile, j, 0).start()
            copy(tile, j, 1).start()
            return carry
        lax.fori_loop(0, tc, body, 0, unroll=4)

    pl.when(i == 0)(lambda: issue(0))
    pl.when(i + 1 < pl.num_programs(0))(lambda: issue(i + 1))

    slot = i % 2
    for k in range(2):
        pltpu.make_async_copy(y_hbm.at[pl.ds(0, tc)], ybuf.at[slot, k], sem.at[slot]).wait()
    gate = g_ref[0]
    rows = 32

    def row_block(c, carry):
        rs = pl.ds(pl.multiple_of(c * rows, rows), rows)
        r = r_ref[rs, :]
        y = r[:, 2:3] * ybuf[slot, 0, rs, :] + r[:, 3:4] * ybuf[slot, 1, rs, :]
        o_ref[rs, :] = x_ref[rs, :] + gate * y
        return carry

    lax.fori_loop(0, tc // rows, row_block, 0, unroll=2)


def _moe_combine(x, y, routes, dest, mods, k_gate, *, n_lat, n_batch, tc=256):
    t, d = x.shape
    seg = _seg_fn(tc, n_lat, n_batch)
    return pl.pallas_call(
        functools.partial(_moe_combine_kernel, tc=tc),
        grid_spec=pltpu.PrefetchScalarGridSpec(
            num_scalar_prefetch=1, grid=(t // tc,),
            in_specs=[pl.BlockSpec((tc, d), lambda i, ds: (i, 0)),
                      pl.BlockSpec((tc, ROUTER_LANES), lambda i, ds: (i, 0)),
                      pl.BlockSpec((1, 1, d), lambda i, ds: (seg(i) * N_MOD + k_gate, 0, 0)),
                      pl.BlockSpec(memory_space=pl.ANY)],
            out_specs=pl.BlockSpec((tc, d), lambda i, ds: (i, 0)),
            scratch_shapes=[pltpu.VMEM((2, 2, tc, d), F32), pltpu.SemaphoreType.DMA((2,))]),
        out_shape=jax.ShapeDtypeStruct((t, d), F32),
        compiler_params=_params("arbitrary"),
        name="moe_combine",
    )(dest, x, routes, mods, y)


def _router(h, w_r, b_r, tm=512):
    t, d = h.shape
    e = w_r.shape[1]
    w_pad = jnp.zeros((d, ROUTER_LANES), BF16).at[:, :e].set(w_r.astype(BF16))
    b_pad = jnp.full((1, ROUTER_LANES), -jnp.inf, F32).at[0, :e].set(b_r)
    return pl.pallas_call(
        _router_kernel,
        grid=(t // tm,),
        in_specs=[pl.BlockSpec((tm, d), lambda i: (i, 0)),
                  pl.BlockSpec((d, ROUTER_LANES), lambda i: (0, 0)),
                  pl.BlockSpec((1, ROUTER_LANES), lambda i: (0, 0))],
        out_specs=pl.BlockSpec((tm, ROUTER_LANES), lambda i: (i, 0)),
        out_shape=jax.ShapeDtypeStruct((t, ROUTER_LANES), F32),
        compiler_params=_params("parallel"),
        name="router",
    )(h, w_pad, b_pad)


def kernel(x, c, ctx, c_ctx, ada_down, ada_up, ada_bias, norm1_g, norm2_g, w_in, sgu_norm_g, sgu_w,
           sgu_b, na_rpb, qk_norm_g, group_norm_g, w_out, ffn_w1, ffn_w3, ffn_w2, moe_router,
           moe_router_b, moe_w1, moe_w3, moe_w2, final_norm_g):
    n_batch, n_lat, d = x.shape
    n_ctx = ctx.shape[1]
    depth = w_in.shape[0]
    mix = w_out.shape[1]
    wa = wb = mix // 4
    wc = mix - wa - wb
    n_b_heads = wb // HEAD_DIM
    n_c_heads = wc // HEAD_DIM
    n_kv_heads = n_c_heads // 4
    wkv = n_kv_heads * HEAD_DIM
    group = n_c_heads // n_kv_heads
    n_experts = moe_router.shape[-1]
    dims = dict(n_lat=n_lat, n_batch=n_batch)
    lat_rows = n_batch * n_lat

    off_bq, off_bk, off_bv = 2 * wa, 2 * wa + wb, 2 * wa + 2 * wb
    src_cq = 2 * wa + 3 * wb
    off_ck = src_cq
    off_cv = off_ck + wkv
    off_cq = off_cv + wkv
    in_width = off_cq + wc
    tn_in = math.gcd(512, wkv)
    nb_pre, nb_kv, nb_q = src_cq // tn_in, 2 * wkv // tn_in, wc // tn_in

    def in_col_block(j):
        return jnp.where(j < nb_pre, j, jnp.where(j < nb_pre + nb_kv, j + nb_q, j - nb_kv))

    in_col_scale = jnp.ones((1, in_width), F32).at[:, off_bq:off_bk].set(QK_PRESCALE)

    xa = jnp.concatenate([x.reshape(lat_rows, d), ctx.reshape(n_batch * n_ctx, d)], axis=0)
    cond = jnp.zeros((8, d), F32).at[:n_batch].set(c).at[n_batch].set(c_ctx)
    mods_all = _ada_mods(cond, ada_down, ada_up, ada_bias)
    cos_t, sin_t = _rope_tables(n_lat, n_ctx)
    w_out_b, ffn_w2_b = w_out.astype(BF16), ffn_w2.astype(BF16)

    for l in range(depth):
        mods = mods_all[l].reshape(8 * N_MOD, 1, d)
        h = _norm_mod(xa, norm1_g[l], mods, 0, 1, out_dtype=BF16, **dims)
        p = _matmul_w32(h, w_in, l, in_col_scale, in_col_block, out_dtype=BF16, tn=tn_in)
        ya = _sgu(p, sgu_norm_g[l], sgu_w[l], sgu_b[l], group_norm_g[l, :wa], width=wa)
        bias, pair_idx = _na_bias_tables(na_rpb[l], n_lat // GRID_W)
        yb_lat = _neighbourhood_attention(
            p, bias, pair_idx, n_batch=n_batch, n_lat=n_lat, n_ctx=n_ctx, q_col=off_bq // HEAD_DIM,
            k_col=off_bk // HEAD_DIM, v_col=off_bv // HEAD_DIM, n_heads=n_b_heads)
        kb_ctx = p[lat_rows:, off_bk:off_bk + wb].reshape(n_batch, n_ctx, wb)
        vb_ctx = p[lat_rows:, off_bv:off_bv + wb].reshape(n_batch, n_ctx, wb)
        yb_ctx = _flash(p, kb_ctx, vb_ctx, n_batch=n_batch, n_q_rows=n_ctx, q_row0=lat_rows,
                        q_rows_per_batch=n_ctx, q_col=off_bq // HEAD_DIM, n_kv_heads=n_b_heads,
                        group=1, kv_row0=0, n_kv_rows=n_ctx)
        qc, kc, vc = _qk_prep(p, cos_t, sin_t, qk_norm_g[l, 0], qk_norm_g[l, 1], n_batch=n_batch,
                              n_lat=n_lat, n_ctx=n_ctx, q_off=off_cq // wc, n_q=n_c_heads,
                              k_off=off_ck // wkv, v_off=off_cv // wkv, n_kv=n_kv_heads)
        yc_lat = _flash_gqa(qc, kc, vc, n_batch=n_batch, n_q_rows=n_lat, q_row0=0,
                            q_rows_per_batch=n_lat, n_kv_heads=n_kv_heads, group=group,
                            kv_row0=0, n_kv_rows=n_lat + n_ctx)
        yc_ctx = _flash_gqa(qc, kc, vc, n_batch=n_batch, n_q_rows=n_ctx, q_row0=lat_rows,
                            q_rows_per_batch=n_ctx, n_kv_heads=n_kv_heads, group=group,
                            kv_row0=n_lat, n_kv_rows=n_ctx)
        merged = _merge(ya, yb_lat, yb_ctx, yc_lat, yc_ctx,
                        group_norm_g[l, wa:wa + wb], group_norm_g[l, wa + wb:])
        xa = _matmul_residual(merged, w_out_b, l, xa, mods, 2, tn=1024, **dims)
        j = l // 2
        if l % 2 == 0:
            h2 = _norm_mod(xa, norm2_g[l], mods, 3, 4, out_dtype=BF16, **dims)
            hid = _matmul_swiglu_w32(h2, ffn_w1, ffn_w3, j)
            xa = _matmul_residual(hid, ffn_w2_b, j, xa, mods, 5, **dims)
        else:
            h2 = _norm_mod(xa, norm2_g[l], mods, 3, 4, out_dtype=F32, **dims)
            routes = _router(h2, moe_router[j], moe_router_b[j])
            src, dest, tile_expert, tile_valid = _route_plan(routes, n_experts, MOE_TILE)
            xs = _gather_rows(h2, src, out_dtype=BF16)
            ys = _moe_experts(xs, moe_w1, moe_w3, moe_w2, j, tile_expert, tile_valid, tm=MOE_TILE)
            xa = _moe_combine(xa, ys, routes, dest, mods, 5, **dims)

    out = _final_norm(xa, final_norm_g, lat_rows)
    return out.reshape(n_batch, n_lat, d)
```

```python
import functools
import math

import numpy as np
import jax
import jax.numpy as jnp
from jax import lax
from jax.experimental import pallas as pl
from jax.experimental.pallas import tpu as pltpu

F32 = jnp.float32
BF16 = jnp.bfloat16

HEAD_DIM = 128
GRID_W = 64
CHUNK = 128
NA_ROWS = 8
NA_COLS = 16
ROPE_THETA = 10000.0
EPS = 1e-6
N_MOD = 6
MASK_VALUE = -1e30
ROUTER_LANES = 128
MOE_TILE = 512
FLASH_LOOKAHEAD = 3
ROW_TILES = (1536, 1024, 512, 256)
NA_Q_ROWS = 8
NA_BAND_ROWS = 16
LOG2E = 1.4426950408889634
QK_PRESCALE = HEAD_DIM ** -0.5 * LOG2E
V7X_VMEM_BYTES = 64 * 1024 * 1024
VMEM_LIMIT = V7X_VMEM_BYTES - 8 * 1024 * 1024


def _params(*sem):
    return pltpu.CompilerParams(dimension_semantics=sem, vmem_limit_bytes=VMEM_LIMIT)


def _rms(x, g):
    return x * lax.rsqrt(jnp.mean(x * x, axis=-1, keepdims=True) + EPS) * g


def _ada_down_kernel(c_ref, w_ref, o_ref):
    c = c_ref[...]
    a = (c * jax.nn.sigmoid(c)).astype(BF16)
    o_ref[0] = jnp.dot(a, w_ref[0].astype(BF16), preferred_element_type=F32)


def _ada_up_kernel(t_ref, w_ref, b_ref, o_ref):
    t = t_ref[0].astype(BF16)
    o_ref[0] = jnp.dot(t, w_ref[0].astype(BF16), preferred_element_type=F32) + b_ref[0]


def _ada_mods(cond, w_down, w_up, b_up):
    depth, d, r = w_down.shape
    n_out = w_up.shape[2]
    rows = cond.shape[0]
    tn1 = min(r, 512)
    t = pl.pallas_call(
        _ada_down_kernel,
        grid=(depth, r // tn1),
        in_specs=[pl.BlockSpec((rows, d), lambda l, j: (0, 0)),
                  pl.BlockSpec((1, d, tn1), lambda l, j: (l, 0, j))],
        out_specs=pl.BlockSpec((1, rows, tn1), lambda l, j: (l, 0, j)),
        out_shape=jax.ShapeDtypeStruct((depth, rows, r), F32),
        compiler_params=_params("parallel", "parallel"),
        name="ada_down",
    )(cond, w_down)
    tn2 = min(n_out, 2048)
    return pl.pallas_call(
        _ada_up_kernel,
        grid=(depth, n_out // tn2),
        in_specs=[pl.BlockSpec((1, rows, r), lambda l, j: (l, 0, 0)),
                  pl.BlockSpec((1, r, tn2), lambda l, j: (l, 0, j)),
                  pl.BlockSpec((1, 1, tn2), lambda l, j: (l, 0, j))],
        out_specs=pl.BlockSpec((1, rows, tn2), lambda l, j: (l, 0, j)),
        out_shape=jax.ShapeDtypeStruct((depth, rows, n_out), F32),
        compiler_params=_params("parallel", "parallel"),
        name="ada_up",
    )(t, w_up, b_up.reshape(depth, 1, n_out))


def _norm_mod_kernel(x_ref, g_ref, sh_ref, sc_ref, o_ref):
    y = _rms(x_ref[...], g_ref[...])
    o_ref[...] = (y * (1.0 + sc_ref[0]) + sh_ref[0]).astype(o_ref.dtype)


def _norm_kernel(x_ref, g_ref, o_ref):
    o_ref[...] = _rms(x_ref[...], g_ref[...]).astype(o_ref.dtype)


def _seg_fn(tm, n_lat, n_batch):
    return lambda i: jnp.minimum((i * tm) // n_lat, n_batch)


def _norm_mod(x, g, mods, k_shift, k_scale, *, n_lat, n_batch, out_dtype, tm=256):
    t, d = x.shape
    seg = _seg_fn(tm, n_lat, n_batch)
    return pl.pallas_call(
        _norm_mod_kernel,
        grid=(t // tm,),
        in_specs=[pl.BlockSpec((tm, d), lambda i: (i, 0)),
                  pl.BlockSpec((1, d), lambda i: (0, 0)),
                  pl.BlockSpec((1, 1, d), lambda i: (seg(i) * N_MOD + k_shift, 0, 0)),
                  pl.BlockSpec((1, 1, d), lambda i: (seg(i) * N_MOD + k_scale, 0, 0))],
        out_specs=pl.BlockSpec((tm, d), lambda i: (i, 0)),
        out_shape=jax.ShapeDtypeStruct((t, d), out_dtype),
        compiler_params=_params("parallel"),
        name="norm_mod",
    )(x, g.reshape(1, d), mods, mods)


def _final_norm(x, g, rows, tm=256):
    d = x.shape[1]
    return pl.pallas_call(
        _norm_kernel,
        grid=(rows // tm,),
        in_specs=[pl.BlockSpec((tm, d), lambda i: (i, 0)),
                  pl.BlockSpec((1, d), lambda i: (0, 0))],
        out_specs=pl.BlockSpec((tm, d), lambda i: (i, 0)),
        out_shape=jax.ShapeDtypeStruct((rows, d), F32),
        compiler_params=_params("parallel"),
        name="final_norm",
    )(x, g.reshape(1, d))


def _mmw_kernel(a_ref, w_ref, cs_ref, o_ref, wb_sc):
    @pl.when(pl.program_id(1) == 0)
    def _():
        wb_sc[...] = (w_ref[...] * cs_ref[...]).astype(BF16)

    o_ref[...] = jnp.dot(a_ref[...], wb_sc[...], preferred_element_type=F32).astype(o_ref.dtype)


def _mmw_swiglu_kernel(a_ref, w1_ref, w3_ref, o_ref, w1_sc, w3_sc):
    @pl.when(pl.program_id(1) == 0)
    def _():
        w1_sc[...] = w1_ref[...].astype(BF16)
        w3_sc[...] = w3_ref[...].astype(BF16)

    a = a_ref[...]
    h1 = jnp.dot(a, w1_sc[...], preferred_element_type=F32)
    h3 = jnp.dot(a, w3_sc[...], preferred_element_type=F32)
    o_ref[...] = (h1 * jax.nn.sigmoid(h1) * h3).astype(o_ref.dtype)


def _mm_res_kernel(a_ref, b_ref, x_ref, g_ref, o_ref):
    acc = jnp.dot(a_ref[...], b_ref[...], preferred_element_type=F32)
    o_ref[...] = x_ref[...] + g_ref[0] * acc


def _pick_tile(n, candidates):
    for c in candidates:
        if n % c == 0:
            return c
    raise ValueError(f"no tile for {n}")


def _matmul_w32(a, w, layer, col_scale, col_block, *, out_dtype, tn):
    m, k = a.shape
    n = w.shape[2]
    tm = _pick_tile(m, ROW_TILES)
    return pl.pallas_call(
        _mmw_kernel,
        grid=(n // tn, m // tm),
        in_specs=[pl.BlockSpec((tm, k), lambda j, i: (i, 0)),
                  pl.BlockSpec((None, k, tn), lambda j, i: (layer, 0, col_block(j))),
                  pl.BlockSpec((1, tn), lambda j, i: (0, col_block(j)))],
        out_specs=pl.BlockSpec((tm, tn), lambda j, i: (i, j)),
        out_shape=jax.ShapeDtypeStruct((m, n), out_dtype),
        scratch_shapes=[pltpu.VMEM((k, tn), BF16)],
        compiler_params=_params("parallel", "arbitrary"),
        name="matmul_w32",
    )(a, w, col_scale)


def _matmul_swiglu_w32(a, w1, w3, layer, *, tn=256):
    m, k = a.shape
    n = w1.shape[2]
    tn = min(tn, n)
    tm = _pick_tile(m, ROW_TILES)
    return pl.pallas_call(
        _mmw_swiglu_kernel,
        grid=(n // tn, m // tm),
        in_specs=[pl.BlockSpec((tm, k), lambda j, i: (i, 0)),
                  pl.BlockSpec((None, k, tn), lambda j, i: (layer, 0, j)),
                  pl.BlockSpec((None, k, tn), lambda j, i: (layer, 0, j))],
        out_specs=pl.BlockSpec((tm, tn), lambda j, i: (i, j)),
        out_shape=jax.ShapeDtypeStruct((m, n), BF16),
        scratch_shapes=[pltpu.VMEM((k, tn), BF16), pltpu.VMEM((k, tn), BF16)],
        compiler_params=_params("parallel", "arbitrary"),
        name="matmul_swiglu_w32",
    )(a, w1, w3)


def _matmul_residual(a, w, layer, x, mods, k_gate, *, n_lat, n_batch, tm=512, tn=512):
    m, k = a.shape
    n = w.shape[2]
    tn = min(tn, n)
    seg = _seg_fn(tm, n_lat, n_batch)
    return pl.pallas_call(
        _mm_res_kernel,
        grid=(m // tm, n // tn),
        in_specs=[pl.BlockSpec((tm, k), lambda i, j: (i, 0)),
                  pl.BlockSpec((None, k, tn), lambda i, j: (layer, 0, j)),
                  pl.BlockSpec((tm, tn), lambda i, j: (i, j)),
                  pl.BlockSpec((1, 1, tn), lambda i, j: (seg(i) * N_MOD + k_gate, 0, j))],
        out_specs=pl.BlockSpec((tm, tn), lambda i, j: (i, j)),
        out_shape=jax.ShapeDtypeStruct((m, n), F32),
        compiler_params=_params("parallel", "arbitrary"),
        name="matmul_residual",
    )(a, w, x, mods)


def _sgu_kernel(u_ref, v_ref, gv_ref, ws_ref, bs_ref, ga_ref, o_ref, *, n_chunks, n_groups):
    for c in range(n_chunks):
        rows = slice(c * CHUNK, (c + 1) * CHUNK)
        u = jax.nn.gelu(u_ref[rows, :].astype(F32), approximate=True)
        v = jax.nn.gelu(v_ref[rows, :].astype(F32), approximate=True)
        vb = _rms(v, gv_ref[...]).astype(BF16)
        parts = [jnp.dot(ws_ref[g], vb[:, g * HEAD_DIM:(g + 1) * HEAD_DIM],
                         preferred_element_type=F32) for g in range(n_groups)]
        y = u * (jnp.concatenate(parts, axis=1) + bs_ref[...])
        o_ref[rows, :] = _rms(y, ga_ref[...]).astype(o_ref.dtype)


def _sgu(p, g_v, w_s, b_s, g_a, *, width, ta=256):
    t = p.shape[0]
    n_groups = width // HEAD_DIM
    bias =jnp.repeat(b_s.T, HEAD_DIM, axis=1)
    return pl.pallas_call(
        functools.partial(_sgu_kernel, n_chunks=ta // CHUNK, n_groups=n_groups),
        grid=(t // ta,),
        in_specs=[pl.BlockSpec((ta, width), lambda i: (i, 0)),
                  pl.BlockSpec((ta, width), lambda i: (i, 1)),
                  pl.BlockSpec((1, width), lambda i: (0, 0)),
                  pl.BlockSpec((n_groups, CHUNK, CHUNK), lambda i: (0, 0, 0)),
                  pl.BlockSpec((CHUNK, width), lambda i: (0, 0)),
                  pl.BlockSpec((1, width), lambda i: (0, 0))],
        out_specs=pl.BlockSpec((ta, width), lambda i: (i, 0)),
        out_shape=jax.ShapeDtypeStruct((t, width), BF16),
        compiler_params=_params("parallel"),
        name="sgu",
    )(p, p, g_v.reshape(1, width), w_s.astype(BF16), bias, g_a.reshape(1, width))


def _na_bias_tables(rpb, rows):
    n_blocks = rows // NA_Q_ROWS
    cols = np.arange(GRID_W)
    col_start = np.clip(cols - NA_COLS // 2, 0, GRID_W - NA_COLS)
    dc = cols[None, :] - cols[:, None]
    col_ok = (cols[None, :] >= col_start[:, None]) & (cols[None, :] < col_start[:, None] + NA_COLS)
    col_idx = np.where(col_ok, dc + NA_COLS - 1, 0)
    n_dr = 2 * NA_ROWS - 1
    t = jnp.where(col_ok[None, None], rpb[:, :, col_idx] * LOG2E, MASK_VALUE)
    t = jnp.concatenate([t, jnp.full((rpb.shape[0], 1, GRID_W, GRID_W), MASK_VALUE, F32)], axis=1)
    slab = np.full((3, NA_Q_ROWS, NA_BAND_ROWS), n_dr, np.int32)
    rep = [0, min(1, n_blocks - 1), n_blocks - 1]
    for ty, blk in enumerate(rep):
        band0 = int(np.clip(blk * NA_Q_ROWS - NA_ROWS // 2, 0, rows - NA_BAND_ROWS))
        for rq in range(NA_Q_ROWS):
            r = blk * NA_Q_ROWS + rq
            start = int(np.clip(r - NA_ROWS // 2, 0, rows - NA_ROWS))
            for kr in range(NA_BAND_ROWS):
                ar = band0 + kr
                if start <= ar < start + NA_ROWS:
                    slab[ty, rq, kr] = ar - r + NA_ROWS - 1
    pairs = slab.reshape(3, NA_Q_ROWS, NA_BAND_ROWS // 2, 2)
    uniq, inverse = np.unique(pairs.reshape(-1, 2), axis=0, return_inverse=True)
    t2 = jnp.concatenate([t[:, uniq[:, 0]], t[:, uniq[:, 1]]], axis=-1)
    return t2, jnp.asarray(inverse.reshape(-1), jnp.int32)


def _na_kernel(pair_ref, q_ref, k_ref, v_ref, kc_ref, vc_ref, bias_ref, o_ref, *, rows, rc, n_blocks,
               heads):
    i = pl.program_id(2)
    block_type = jnp.where(i == 0, 0, jnp.where(i == n_blocks - 1, 2, 1))
    pairs_per_row = NA_BAND_ROWS // 2

    def bias_rows(hh, rq):
        base = (block_type * NA_Q_ROWS + rq) * pairs_per_row
        return jnp.concatenate([bias_ref[hh, pair_ref[base + m]] for m in range(pairs_per_row)], axis=1)

    band0 = jnp.clip(i * NA_Q_ROWS - NA_ROWS // 2, 0, rows - NA_BAND_ROWS)
    start = pl.multiple_of(band0 * GRID_W, GRID_W)
    nb = NA_BAND_ROWS * GRID_W
    dn = (((1,), (1,)), ((), ()))
    n_chunks = q_ref.shape[0] // rc
    n_loc = nb // HEAD_DIM
    units = [(hh, c) for hh in range(heads) for c in range(n_chunks)]

    def lanes(hh):
        return slice(hh * HEAD_DIM, (hh + 1) * HEAD_DIM)

    def scores(unit):
        hh, c = unit
        q = q_ref[c * rc:(c + 1) * rc, lanes(hh)]
        return (lax.dot_general(q, k_ref[pl.ds(start, nb), lanes(hh)], dn, preferred_element_type=F32),
                lax.dot_general(q, kc_ref[:, lanes(hh)], dn, preferred_element_type=F32))

    nxt = scores(units[0])
    for u, (hh, c) in enumerate(units):
        s_loc, s_ctx = nxt
        if u + 1 < len(units):
            nxt = scores(units[u + 1])
        vb = v_ref[pl.ds(start, nb), lanes(hh)]
        vc = vc_ref[:, lanes(hh)]
        q_rows = rc // GRID_W
        s_loc = s_loc + jnp.concatenate([bias_rows(hh, c * q_rows + r) for r in range(q_rows)], axis=0)
        cols = ([s_loc[:, j * HEAD_DIM:(j + 1) * HEAD_DIM] for j in range(n_loc)]
                + [s_ctx[:, j * HEAD_DIM:(j + 1) * HEAD_DIM] for j in range(s_ctx.shape[1] // HEAD_DIM)])
        m = jnp.max(functools.reduce(jnp.maximum, cols), axis=1, keepdims=True)
        ps = [jnp.exp2(x - m) for x in cols]
        l = jnp.sum(functools.reduce(jnp.add, ps), axis=1, keepdims=True)
        p_loc = jnp.concatenate([x.astype(BF16) for x in ps[:n_loc]], axis=1)
        p_ctx = jnp.concatenate([x.astype(BF16) for x in ps[n_loc:]], axis=1)
        o = (jnp.dot(p_loc, vb, preferred_element_type=F32)
             + jnp.dot(p_ctx, vc, preferred_element_type=F32))
        o_ref[c * rc:(c + 1) * rc, lanes(hh)] = (o / l).astype(o_ref.dtype)


def _neighbourhood_attention(p, bias, pair_idx, *, n_batch, n_lat, n_ctx, q_col, k_col, v_col, n_heads):
    rows = n_lat // GRID_W
    tq = NA_Q_ROWS * GRID_W
    n_blocks = n_lat // tq
    ctx_blk0 = (n_batch * n_lat) // n_ctx
    n_pairs = bias.shape[1]
    hp = 2 if (n_heads % 2 == 0 and q_col % 2 == 0 and k_col % 2 == 0 and v_col % 2 == 0) else 1
    w = hp * HEAD_DIM
    qc, kc, vc = q_col // hp, k_col // hp, v_col // hp
    return pl.pallas_call(
        functools.partial(_na_kernel, rows=rows, rc=min(256, tq), n_blocks=n_blocks, heads=hp),
        grid_spec=pltpu.PrefetchScalarGridSpec(
            num_scalar_prefetch=1, grid=(n_heads // hp, n_batch, n_blocks),
            in_specs=[pl.BlockSpec((tq, w), lambda h, b, i, pr: (b * n_blocks + i, qc + h)),
                      pl.BlockSpec((n_lat, w), lambda h, b, i, pr: (b, kc + h)),
                      pl.BlockSpec((n_lat, w), lambda h, b, i, pr: (b, vc + h)),
                      pl.BlockSpec((n_ctx, w), lambda h, b, i, pr: (ctx_blk0 + b, kc + h)),
                      pl.BlockSpec((n_ctx, w), lambda h, b, i, pr: (ctx_blk0 + b, vc + h)),
                      pl.BlockSpec((hp, n_pairs, GRID_W, 2 * GRID_W), lambda h, b, i, pr: (h, 0, 0, 0))],
            out_specs=pl.BlockSpec((tq, w), lambda h, b, i, pr: (b * n_blocks + i, h))),
        out_shape=jax.ShapeDtypeStruct((n_batch * n_lat, n_heads * HEAD_DIM), BF16),
        compiler_params=_params("parallel", "parallel", "arbitrary"),
        name="neighbourhood_attention",
    )(pair_idx, p, p, p, p, p, bias)


def _rope_tables(n_lat, n_ctx):
    t = jnp.arange(n_lat, dtype=jnp.int32)
    pos = jnp.stack([t // GRID_W, t % GRID_W], axis=-1).astype(F32)
    n_freq = HEAD_DIM // 4
    inv = 1.0 / (ROPE_THETA ** (jnp.arange(n_freq, dtype=F32) / n_freq))
    ang = pos[:, :, None] * inv
    cos, sin = jnp.cos(ang), jnp.sin(ang)
    cos_t = jnp.concatenate([cos[:, 0], cos[:, 0], cos[:, 1], cos[:, 1]], axis=-1)
    sin_t = jnp.concatenate([-sin[:, 0], sin[:, 0], -sin[:, 1], sin[:, 1]], axis=-1)
    cos_t = jnp.concatenate([cos_t, jnp.ones((n_ctx, HEAD_DIM), F32)], axis=0)
    sin_t = jnp.concatenate([sin_t, jnp.zeros((n_ctx, HEAD_DIM), F32)], axis=0)
    return cos_t, sin_t


def _qk_prep_kernel(q_ref, k_ref, v_ref, cos_ref, sin_ref, gq_ref, gk_ref,
                    qo_ref, ko_ref, vo_ref, *, n_q, n_kv):
    cos = cos_ref[...]
    sin = sin_ref[...]
    quarter = HEAD_DIM // 4
    lane = lax.broadcasted_iota(jnp.int32, cos.shape, 1)
    first_half = (lane % (2 * quarter)) < quarter

    def prep(x, g, out_scale):
        y = _rms(x.astype(F32), g)
        swapped = jnp.where(first_half,
                            pltpu.roll(y, HEAD_DIM - quarter, 1),
                            pltpu.roll(y, quarter, 1))
        return ((y * cos + swapped * sin) * out_scale).astype(BF16)

    for h in range(n_q):
        cs = slice(h * HEAD_DIM, (h + 1) * HEAD_DIM)
        qo_ref[:, cs] = prep(q_ref[:, cs], gq_ref[...], QK_PRESCALE)
    ones = jnp.ones((v_ref.shape[0], HEAD_DIM), BF16)
    for h in range(n_kv):
        cs = slice(h * HEAD_DIM, (h + 1) * HEAD_DIM)
        ko_ref[0, :, cs] = prep(k_ref[:, cs], gk_ref[...], 1.0)
        vo_ref[0, :, 2 * h * HEAD_DIM:(2 * h + 1) * HEAD_DIM] = v_ref[:, cs]
        vo_ref[0, :, (2 * h + 1) * HEAD_DIM:(2 * h + 2) * HEAD_DIM] = ones


def _qk_prep(p, cos_t, sin_t, g_q, g_k, *, n_batch, n_lat, n_ctx, q_off, n_q, k_off, v_off, n_kv):
    t = p.shape[0]
    tm = n_ctx
    lat_tiles = n_lat // tm
    n_lat_tiles = n_batch * lat_tiles

    def pos_blk(i):
        return jnp.where(i < n_lat_tiles, i % lat_tiles, lat_tiles)

    def kv_blk(i):
        c = i - n_lat_tiles
        return (jnp.where(i < n_lat_tiles, i // lat_tiles, c),
                jnp.where(i < n_lat_tiles, i % lat_tiles, lat_tiles), 0)

    wq, wkv = n_q * HEAD_DIM, n_kv * HEAD_DIM
    return pl.pallas_call(
        functools.partial(_qk_prep_kernel, n_q=n_q, n_kv=n_kv),
        grid=(t // tm,),
        in_specs=[pl.BlockSpec((tm, wq), lambda i: (i, q_off)),
                  pl.BlockSpec((tm, wkv), lambda i: (i, k_off)),
                  pl.BlockSpec((tm, wkv), lambda i: (i, v_off)),
                  pl.BlockSpec((tm, HEAD_DIM), lambda i: (pos_blk(i), 0)),
                  pl.BlockSpec((tm, HEAD_DIM), lambda i: (pos_blk(i), 0)),
                  pl.BlockSpec((1, HEAD_DIM), lambda i: (0, 0)),
                  pl.BlockSpec((1, HEAD_DIM), lambda i: (0, 0))],
        out_specs=[pl.BlockSpec((tm, wq), lambda i: (i, 0)),
                   pl.BlockSpec((1, tm, wkv), kv_blk),
                   pl.BlockSpec((1, tm, 2 * wkv), kv_blk)],
        out_shape=[jax.ShapeDtypeStruct((t, wq), BF16),
                   jax.ShapeDtypeStruct((n_batch, n_lat + n_ctx, wkv), BF16),
                   jax.ShapeDtypeStruct((n_batch, n_lat + n_ctx, 2 * wkv), BF16)],
        compiler_params=_params("parallel"),
        name="qk_prep",
    )(p, p, p, cos_t, sin_t, g_q.reshape(1, HEAD_DIM), g_k.reshape(1, HEAD_DIM))


def _flash_kernel(q_ref, k_ref, v_ref, o_ref, m_sc, l_sc, acc_sc, *, group, tq):
    kj = pl.program_id(3)

    @pl.when(kj == 0)
    def _():
        m_sc[...] = jnp.full_like(m_sc, -jnp.inf)
        l_sc[...] = jnp.zeros_like(l_sc)
        acc_sc[...] = jnp.zeros_like(acc_sc)

    if group == 1:
        q = q_ref[...]
    else:
        q = jnp.concatenate([q_ref[:, g * HEAD_DIM:(g + 1) * HEAD_DIM] for g in range(group)], axis=0)
    s = lax.dot_general(q, k_ref[0], (((1,), (1,)), ((), ())), preferred_element_type=F32)
    m_prev = m_sc[...]
    m_new = jnp.maximum(m_prev, jnp.max(s, axis=1, keepdims=True))
    alpha = jnp.exp2(m_prev - m_new)
    p = jnp.exp2(s - m_new)
    l_sc[...] = alpha * l_sc[...] + jnp.sum(p, axis=1, keepdims=True)
    acc_sc[...] = alpha * acc_sc[...] + jnp.dot(p.astype(BF16), v_ref[0], preferred_element_type=F32)
    m_sc[...] = m_new

    @pl.when(kj == pl.num_programs(3) - 1)
    def _():
        o = acc_sc[...] / l_sc[...]
        for g in range(group):
            o_ref[:, g * HEAD_DIM:(g + 1) * HEAD_DIM] = o[g * tq:(g + 1) * tq].astype(o_ref.dtype)


def _flash(q2d, k3d, v3d, *, n_batch, n_q_rows, q_row0, q_rows_per_batch, q_col, n_kv_heads,
           group, kv_row0, n_kv_rows, tq=256):
    tq = min(tq, n_q_rows)
    tk = _pick_tile(n_kv_rows, (1024, 768, 512, 256))
    qb0, qbb = q_row0 // tq, q_rows_per_batch // tq
    kb0 = kv_row0 // tk
    gw = group * HEAD_DIM
    return pl.pallas_call(
        functools.partial(_flash_kernel, group=group, tq=tq),
        grid=(n_batch, n_kv_heads, n_q_rows // tq, n_kv_rows // tk),
        in_specs=[pl.BlockSpec((tq, gw), lambda b, h, i, j: (qb0 + b * qbb + i, q_col + h)),
                  pl.BlockSpec((1, tk, HEAD_DIM), lambda b, h, i, j: (b, kb0 + j, h)),
                  pl.BlockSpec((1, tk, HEAD_DIM), lambda b, h, i, j: (b, kb0 + j, h))],
        out_specs=pl.BlockSpec((tq, gw), lambda b, h, i, j: (b * (n_q_rows // tq) + i, h)),
        out_shape=jax.ShapeDtypeStruct((n_batch * n_q_rows, n_kv_heads * gw), BF16),
        scratch_shapes=[pltpu.VMEM((group * tq, 1), F32),
                        pltpu.VMEM((group * tq, 1), F32),
                        pltpu.VMEM((group * tq, HEAD_DIM), F32)],
        compiler_params=_params("parallel", "parallel", "parallel", "arbitrary"),
        name="flash_attention",
    )(q2d, k3d, v3d)


def _flash_gqa_kernel(q_ref, k_ref, v_ref, o_ref, m_sc, acc_sc, *, group, tq, tk, rc):
    kj = pl.program_id(3)

    @pl.when(kj == 0)
    def _():
        m_sc[...] = jnp.full_like(m_sc, -jnp.inf)
        acc_sc[...] = jnp.zeros_like(acc_sc)

    n_cols = tk // HEAD_DIM
    k = k_ref[0]
    v = v_ref[0]
    chunks = [(g, r0) for g in range(group) for r0 in range(0, tq, rc)]

    def scores(chunk):
        g, r0 = chunk
        q = q_ref[r0:r0 + rc, g * HEAD_DIM:(g + 1) * HEAD_DIM]
        return lax.dot_general(q, k, (((1,), (1,)), ((), ())), preferred_element_type=F32)

    pending = [scores(ch) for ch in chunks[:FLASH_LOOKAHEAD]]
    for ci, (g, r0) in enumerate(chunks):
        if ci + FLASH_LOOKAHEAD < len(chunks):
            pending.append(scores(chunks[ci + FLASH_LOOKAHEAD]))
        s = pending.pop(0)
        rows = slice(g * tq + r0, g * tq + r0 + rc)
        cols = [s[:, i * HEAD_DIM:(i + 1) * HEAD_DIM] for i in range(n_cols)]
        col_max = functools.reduce(jnp.maximum, cols)
        m_prev = m_sc[rows, :]
        m_new = jnp.maximum(m_prev, jnp.max(col_max, axis=1, keepdims=True))
        alpha = jnp.exp2(m_prev - m_new)
        p = jnp.concatenate([jnp.exp2(c - m_new).astype(BF16) for c in cols], axis=1)
        pv = jnp.dot(p, v, preferred_element_type=F32)
        acc_sc[rows, :] = jnp.concatenate([alpha, alpha], axis=1) * acc_sc[rows, :] + pv
        m_sc[rows, :] = m_new

    @pl.when(kj == pl.num_programs(3) - 1)
    def _():
        o = acc_sc[:, :HEAD_DIM] / acc_sc[:, HEAD_DIM:]
        for g in range(group):
            o_ref[:, g * HEAD_DIM:(g + 1) * HEAD_DIM] = o[g * tq:(g + 1) * tq].astype(o_ref.dtype)


def _flash_gqa(q2d, k3d, v3d, *, n_batch, n_q_rows, q_row0, q_rows_per_batch, n_kv_heads, group,
               kv_row0, n_kv_rows, tq=512, rc=512):
    tq = min(tq, n_q_rows)
    rc = min(rc, tq)
    tk = _pick_tile(n_kv_rows, (2816, 1408, 1024, 768, 512, 256))
    qb0, qbb = q_row0 // tq, q_rows_per_batch // tq
    kb0 = kv_row0 // tk
    gw = group * HEAD_DIM
    return pl.pallas_call(
        functools.partial(_flash_gqa_kernel, group=group, tq=tq, tk=tk, rc=rc),
        grid=(n_batch, n_kv_heads, n_q_rows // tq, n_kv_rows // tk),
        in_specs=[pl.BlockSpec((tq, gw), lambda b, h, i, j: (qb0 + b * qbb + i, h)),
                  pl.BlockSpec((1, tk, HEAD_DIM), lambda b, h, i, j: (b, kb0 + j, h)),
                  pl.BlockSpec((1, tk, 2 * HEAD_DIM), lambda b, h, i, j: (b, kb0 + j, h))],
        out_specs=pl.BlockSpec((tq, gw), lambda b, h, i, j: (b * (n_q_rows // tq) + i, h)),
        out_shape=jax.ShapeDtypeStruct((n_batch * n_q_rows, n_kv_heads * gw), BF16),
        scratch_shapes=[pltpu.VMEM((group * tq, HEAD_DIM), F32),
                        pltpu.VMEM((group * tq, 2 * HEAD_DIM), F32)],
        compiler_params=_params("parallel", "parallel", "parallel", "arbitrary"),
        name="flash_gqa",
    )(q2d, k3d, v3d)


def _merge_kernel(ya_ref, yb_lat_ref, yb_ctx_ref, yc_lat_ref, yc_ctx_ref, gb_ref, gc_ref, o_ref,
                  *, wa, wb, n_lat_tiles):
    o_ref[:, :wa] = ya_ref[...]

    def emit(yb_ref, yc_ref):
        o_ref[:, wa:wa + wb] = _rms(yb_ref[...].astype(F32), gb_ref[...]).astype(o_ref.dtype)
        o_ref[:, wa + wb:] = _rms(yc_ref[...].astype(F32), gc_ref[...]).astype(o_ref.dtype)

    is_lat = pl.program_id(0) < n_lat_tiles
    pl.when(is_lat)(lambda: emit(yb_lat_ref, yc_lat_ref))
    pl.when(jnp.logical_not(is_lat))(lambda: emit(yb_ctx_ref, yc_ctx_ref))


def _merge(ya, yb_lat, yb_ctx, yc_lat, yc_ctx, g_b, g_c, tm=256):
    t, wa = ya.shape
    wb, wc = yb_lat.shape[1], yc_lat.shape[1]
    n_lat_tiles = yb_lat.shape[0] // tm

    def lat(i):
        return (jnp.minimum(i, n_lat_tiles - 1), 0)

    def ctx(i):
        return (jnp.maximum(i - n_lat_tiles, 0), 0)

    return pl.pallas_call(
        functools.partial(_merge_kernel, wa=wa, wb=wb, n_lat_tiles=n_lat_tiles),
        grid=(t // tm,),
        in_specs=[pl.BlockSpec((tm, wa), lambda i: (i, 0)),
                  pl.BlockSpec((tm, wb), lat), pl.BlockSpec((tm, wb), ctx),
                  pl.BlockSpec((tm, wc), lat), pl.BlockSpec((tm, wc), ctx),
                  pl.BlockSpec((1, wb), lambda i: (0, 0)),
                  pl.BlockSpec((1, wc), lambda i: (0, 0))],
        out_specs=pl.BlockSpec((tm, wa + wb + wc), lambda i: (i, 0)),
        out_shape=jax.ShapeDtypeStruct((t, wa + wb + wc), BF16),
        compiler_params=_params("parallel"),
        name="merge_groups",
    )(ya, yb_lat, yb_ctx, yc_lat, yc_ctx, g_b.reshape(1, wb), g_c.reshape(1, wc))


def _router_kernel(h_ref, w_ref, b_ref, o_ref):
    logits = jnp.dot(h_ref[...].astype(BF16), w_ref[...], preferred_element_type=F32) + b_ref[...]
    lane = lax.broadcasted_iota(jnp.int32, logits.shape, 1).astype(F32)
    m1 = jnp.max(logits, axis=1, keepdims=True)
    i1 = jnp.min(jnp.where(logits == m1, lane, float(ROUTER_LANES)), axis=1, keepdims=True)
    rest = jnp.where(lane == i1, -jnp.inf, logits)
    m2 = jnp.max(rest, axis=1, keepdims=True)
    i2 = jnp.min(jnp.where(rest == m2, lane, float(ROUTER_LANES)), axis=1, keepdims=True)
    e2 = jnp.exp(m2 - m1)
    g1 = 1.0 / (1.0 + e2)
    o_ref[...] = jnp.where(lane == 0.0, i1,
                           jnp.where(lane == 1.0, i2,
                                     jnp.where(lane == 2.0, g1,
                                               jnp.where(lane == 3.0, e2 * g1, 0.0))))


def _route_plan(routes, n_experts, tm):
    t = routes.shape[0]
    a = 2 * t
    flat_e = routes[:, :2].astype(jnp.int32).reshape(a)
    onehot = (flat_e[:, None] == jnp.arange(n_experts, dtype=jnp.int32)[None, :]).astype(jnp.int32)
    csum = jnp.cumsum(onehot, axis=0)
    counts = csum[-1]
    rank = jnp.take_along_axis(csum, flat_e[:, None], axis=1)[:, 0] - 1
    padded = ((counts + tm - 1) // tm) * tm
    ends = jnp.cumsum(padded)
    starts = ends - padded
    dest = starts[flat_e] + rank
    ustarts = jnp.cumsum(counts) - counts
    order = jnp.argsort(flat_e, stable=True).astype(jnp.int32)
    n_rows = a + n_experts * tm
    row = jnp.arange(n_rows, dtype=jnp.int32)
    row_e = jnp.minimum(jnp.searchsorted(ends, row, side="right"), n_experts - 1).astype(jnp.int32)
    k = row - starts[row_e]
    src_a = order[jnp.clip(ustarts[row_e] + k, 0, a - 1)]
    src = jnp.where(k < counts[row_e], src_a // 2, 0).astype(jnp.int32)
    tile_start = jnp.arange(n_rows // tm, dtype=jnp.int32) * tm
    tile_valid = (tile_start < ends[-1]).astype(jnp.int32)
    tile_expert = row_e[jnp.minimum(tile_start, ends[-1] - 1)]
    return src, dest.astype(jnp.int32), tile_expert, tile_valid


def _gather_rows_kernel(src_ref, h_hbm, o_ref, buf, sem, *, tg):
    i = pl.program_id(0)

    def copy(tile, j):
        slot = tile % 2
        return pltpu.make_async_copy(h_hbm.at[pl.ds(src_ref[tile * tg + j], 1)],
                                     buf.at[slot, pl.ds(j, 1)], sem.at[slot])

    def issue(tile):
        def body(j, carry):
            copy(tile, j).start()
            return carry
        lax.fori_loop(0, tg, body, 0, unroll=8)

    pl.when(i == 0)(lambda: issue(0))
    pl.when(i + 1 < pl.num_programs(0))(lambda: issue(i + 1))

    slot = i % 2
    pltpu.make_async_copy(h_hbm.at[pl.ds(0, tg)], buf.at[slot], sem.at[slot]).wait()
    o_ref[...] = buf[slot].astype(o_ref.dtype)


def _gather_rows(h, src, *, out_dtype, tg=512):
    n = src.shape[0]
    d = h.shape[1]
    return pl.pallas_call(
        functools.partial(_gather_rows_kernel, tg=tg),
        grid_spec=pltpu.PrefetchScalarGridSpec(
            num_scalar_prefetch=1, grid=(n // tg,),
            in_specs=[pl.BlockSpec(memory_space=pl.ANY)],
            out_specs=pl.BlockSpec((tg, d), lambda i, s: (i, 0)),
            scratch_shapes=[pltpu.VMEM((2, tg, d), h.dtype), pltpu.SemaphoreType.DMA((2,))]),
        out_shape=jax.ShapeDtypeStruct((n, d), out_dtype),
        compiler_params=_params("arbitrary"),
        name="moe_gather",
    )(src, h)


def _expert_changed(te_ref, i):
    return jnp.logical_or(i == 0, te_ref[i] != te_ref[jnp.maximum(i - 1, 0)])


def _moe_up_kernel(te_ref, tv_ref, x_ref, w1_ref, w3_ref, o_ref, w1_sc, w3_sc):
    i = pl.program_id(1)
    valid = tv_ref[i] == 1

    @pl.when(jnp.logical_and(valid, _expert_changed(te_ref, i)))
    def _():
        w1_sc[...] = w1_ref[...].astype(BF16)
        w3_sc[...] = w3_ref[...].astype(BF16)

    @pl.when(valid)
    def _():
        a = x_ref[...]
        h1 = jnp.dot(a, w1_sc[...], preferred_element_type=F32)
        h3 = jnp.dot(a, w3_sc[...], preferred_element_type=F32)
        o_ref[...] = (h1 * jax.nn.sigmoid(h1) * h3).astype(o_ref.dtype)

    @pl.when(jnp.logical_not(valid))
    def _():
        o_ref[...] = jnp.zeros_like(o_ref)


def _moe_down_kernel(te_ref, tv_ref, a_ref, w_ref, o_ref, w_sc):
    i = pl.program_id(1)
    valid = tv_ref[i] == 1

    @pl.when(jnp.logical_and(valid, _expert_changed(te_ref, i)))
    def _():
        w_sc[...] = w_ref[...].astype(BF16)

    @pl.when(valid)
    def _():
        o_ref[...] = jnp.dot(a_ref[...], w_sc[...], preferred_element_type=F32)

    @pl.when(jnp.logical_not(valid))
    def _():
        o_ref[...] = jnp.zeros_like(o_ref)


def _moe_experts(xs, w1, w3, w2, layer, tile_expert, tile_valid, *, tm, tn_up=512, tn_down=2048):
    rows, d = xs.shape
    f = w1.shape[-1]
    tn_up, tn_down = min(tn_up, f), min(tn_down, d)
    n_tiles = rows // tm
    hid = pl.pallas_call(
        _moe_up_kernel,
        grid_spec=pltpu.PrefetchScalarGridSpec(
            num_scalar_prefetch=2, grid=(f // tn_up, n_tiles),
            in_specs=[pl.BlockSpec((tm, d), lambda j, i, te, tv: (i, 0)),
                      pl.BlockSpec((None, None, d, tn_up), lambda j, i, te, tv: (layer, te[i], 0, j)),
                      pl.BlockSpec((None, None, d, tn_up), lambda j, i, te, tv: (layer, te[i], 0, j))],
            out_specs=pl.BlockSpec((tm, tn_up), lambda j, i, te, tv: (i, j)),
            scratch_shapes=[pltpu.VMEM((d, tn_up), BF16), pltpu.VMEM((d, tn_up), BF16)]),
        out_shape=jax.ShapeDtypeStruct((rows, f), BF16),
        compiler_params=_params("parallel", "arbitrary"),
        name="moe_up",
    )(tile_expert, tile_valid, xs, w1, w3)
    return pl.pallas_call(
        _moe_down_kernel,
        grid_spec=pltpu.PrefetchScalarGridSpec(
            num_scalar_prefetch=2, grid=(d // tn_down, n_tiles),
            in_specs=[pl.BlockSpec((tm, f), lambda j, i, te, tv: (i, 0)),
                      pl.BlockSpec((None, None, f, tn_down), lambda j, i, te, tv: (layer, te[i], 0, j))],
            out_specs=pl.BlockSpec((tm, tn_down), lambda j, i, te, tv: (i, j)),
            scratch_shapes=[pltpu.VMEM((f, tn_down), BF16)]),
        out_shape=jax.ShapeDtypeStruct((rows, d), F32),
        compiler_params=_params("parallel", "arbitrary"),
        name="moe_down",
    )(tile_expert, tile_valid, hid, w2)


def _moe_combine_kernel(dest_ref, x_ref, r_ref, g_ref, y_hbm, o_ref, ybuf, sem, *, tc):
    i = pl.program_id(0)

    def copy(tile, j, k):
        slot = tile % 2
        return pltpu.make_async_copy(y_hbm.at[pl.ds(dest_ref[(tile * tc + j) * 2 + k], 1)],
                                     ybuf.at[slot, k, pl.ds(j, 1)], sem.at[slot])

    def issue(tile):
        def body(j, carry):
            copy(tile, j, 0).start()
            copy(tile, j, 1).start()
            return carry
        lax.fori_loop(0, tc, body, 0, unroll=4)

    pl.when(i == 0)(lambda: issue(0))
    pl.when(i + 1 < pl.num_programs(0))(lambda: issue(i + 1))

    slot = i % 2
    for k in range(2):
        pltpu.make_async_copy(y_hbm.at[pl.ds(0, tc)], ybuf.at[slot, k], sem.at[slot]).wait()
    gate = g_ref[0]
    rows = 32

    def row_block(c, carry):
        rs = pl.ds(pl.multiple_of(c * rows, rows), rows)
        r = r_ref[rs, :]
        y = r[:, 2:3] * ybuf[slot, 0, rs, :] + r[:, 3:4] * ybuf[slot, 1, rs, :]
        o_ref[rs, :] = x_ref[rs, :] + gate * y
        return carry

    lax.fori_loop(0, tc // rows, row_block, 0, unroll=2)


def _moe_combine(x, y, routes, dest, mods, k_gate, *, n_lat, n_batch, tc=256):
    t, d = x.shape
    seg = _seg_fn(tc, n_lat, n_batch)
    return pl.pallas_call(
        functools.partial(_moe_combine_kernel, tc=tc),
        grid_spec=pltpu.PrefetchScalarGridSpec(
            num_scalar_prefetch=1, grid=(t // tc,),
            in_specs=[pl.BlockSpec((tc, d), lambda i, ds: (i, 0)),
                      pl.BlockSpec((tc, ROUTER_LANES), lambda i, ds: (i, 0)),
                      pl.BlockSpec((1, 1, d), lambda i, ds: (seg(i) * N_MOD + k_gate, 0, 0)),
                      pl.BlockSpec(memory_space=pl.ANY)],
            out_specs=pl.BlockSpec((tc, d), lambda i, ds: (i, 0)),
            scratch_shapes=[pltpu.VMEM((2, 2, tc, d), F32), pltpu.SemaphoreType.DMA((2,))]),
        out_shape=jax.ShapeDtypeStruct((t, d), F32),
        compiler_params=_params("arbitrary"),
        name="moe_combine",
    )(dest, x, routes, mods, y)


def _router(h, w_r, b_r, tm=512):
    t, d = h.shape
    e = w_r.shape[1]
    w_pad = jnp.zeros((d, ROUTER_LANES), BF16).at[:, :e].set(w_r.astype(BF16))
    b_pad = jnp.full((1, ROUTER_LANES), -jnp.inf, F32).at[0, :e].set(b_r)
    return pl.pallas_call(
        _router_kernel,
        grid=(t // tm,),
        in_specs=[pl.BlockSpec((tm, d), lambda i: (i, 0)),
                  pl.BlockSpec((d, ROUTER_LANES), lambda i: (0, 0)),
                  pl.BlockSpec((1, ROUTER_LANES), lambda i: (0, 0))],
        out_specs=pl.BlockSpec((tm, ROUTER_LANES), lambda i: (i, 0)),
        out_shape=jax.ShapeDtypeStruct((t, ROUTER_LANES), F32),
        compiler_params=_params("parallel"),
        name="router",
    )(h, w_pad, b_pad)


def kernel(x, c, ctx, c_ctx, ada_down, ada_up, ada_bias, norm1_g, norm2_g, w_in, sgu_norm_g, sgu_w,
           sgu_b, na_rpb, qk_norm_g, group_norm_g, w_out, ffn_w1, ffn_w3, ffn_w2, moe_router,
           moe_router_b, moe_w1, moe_w3, moe_w2, final_norm_g):
    n_batch, n_lat, d = x.shape
    n_ctx = ctx.shape[1]
    depth = w_in.shape[0]
    mix = w_out.shape[1]
    wa = wb = mix // 4
    wc = mix - wa - wb
    n_b_heads = wb // HEAD_DIM
    n_c_heads = wc // HEAD_DIM
    n_kv_heads = n_c_heads // 4
    wkv = n_kv_heads * HEAD_DIM
    group = n_c_heads // n_kv_heads
    n_experts = moe_router.shape[-1]
    dims = dict(n_lat=n_lat, n_batch=n_batch)
    lat_rows = n_batch * n_lat

    off_bq, off_bk, off_bv = 2 * wa, 2 * wa + wb, 2 * wa + 2 * wb
    src_cq = 2 * wa + 3 * wb
    off_ck = src_cq
    off_cv = off_ck + wkv
    off_cq = off_cv + wkv
    in_width = off_cq + wc
    tn_in = math.gcd(512, wkv)
    nb_pre, nb_kv, nb_q = src_cq // tn_in, 2 * wkv // tn_in, wc // tn_in

    def in_col_block(j):
        return jnp.where(j < nb_pre, j, jnp.where(j < nb_pre + nb_kv, j + nb_q, j - nb_kv))

    in_col_scale = jnp.ones((1, in_width), F32).at[:, off_bq:off_bk].set(QK_PRESCALE)

    xa = jnp.concatenate([x.reshape(lat_rows, d), ctx.reshape(n_batch * n_ctx, d)], axis=0)
    cond = jnp.zeros((8, d), F32).at[:n_batch].set(c).at[n_batch].set(c_ctx)
    mods_all = _ada_mods(cond, ada_down, ada_up, ada_bias)
    cos_t, sin_t = _rope_tables(n_lat, n_ctx)
    w_out_b, ffn_w2_b = w_out.astype(BF16), ffn_w2.astype(BF16)

    for l in range(depth):
        mods = mods_all[l].reshape(8 * N_MOD, 1, d)
        h = _norm_mod(xa, norm1_g[l], mods, 0, 1, out_dtype=BF16, **dims)
        p = _matmul_w32(h, w_in, l, in_col_scale, in_col_block, out_dtype=BF16, tn=tn_in)
        ya = _sgu(p, sgu_norm_g[l], sgu_w[l], sgu_b[l], group_norm_g[l, :wa], width=wa)
        bias, pair_idx = _na_bias_tables(na_rpb[l], n_lat // GRID_W)
        yb_lat = _neighbourhood_attention(
            p, bias, pair_idx, n_batch=n_batch, n_lat=n_lat, n_ctx=n_ctx, q_col=off_bq // HEAD_DIM,
            k_col=off_bk // HEAD_DIM, v_col=off_bv // HEAD_DIM, n_heads=n_b_heads)
        kb_ctx = p[lat_rows:, off_bk:off_bk + wb].reshape(n_batch, n_ctx, wb)
        vb_ctx = p[lat_rows:, off_bv:off_bv + wb].reshape(n_batch, n_ctx, wb)
        yb_ctx = _flash(p, kb_ctx, vb_ctx, n_batch=n_batch, n_q_rows=n_ctx, q_row0=lat_rows,
                        q_rows_per_batch=n_ctx, q_col=off_bq // HEAD_DIM, n_kv_heads=n_b_heads,
                        group=1, kv_row0=0, n_kv_rows=n_ctx)
        qc, kc, vc = _qk_prep(p, cos_t, sin_t, qk_norm_g[l, 0], qk_norm_g[l, 1], n_batch=n_batch,
                              n_lat=n_lat, n_ctx=n_ctx, q_off=off_cq // wc, n_q=n_c_heads,
                              k_off=off_ck // wkv, v_off=off_cv // wkv, n_kv=n_kv_heads)
        yc_lat = _flash_gqa(qc, kc, vc, n_batch=n_batch, n_q_rows=n_lat, q_row0=0,
                            q_rows_per_batch=n_lat, n_kv_heads=n_kv_heads, group=group,
                            kv_row0=0, n_kv_rows=n_lat + n_ctx)
        yc_ctx = _flash_gqa(qc, kc, vc, n_batch=n_batch, n_q_rows=n_ctx, q_row0=lat_rows,
                            q_rows_per_batch=n_ctx, n_kv_heads=n_kv_heads, group=group,
                            kv_row0=n_lat, n_kv_rows=n_ctx)
        merged = _merge(ya, yb_lat, yb_ctx, yc_lat, yc_ctx,
                        group_norm_g[l, wa:wa + wb], group_norm_g[l, wa + wb:])
        xa = _matmul_residual(merged, w_out_b, l, xa, mods, 2, tn=1024, **dims)
        j = l // 2
        if l % 2 == 0:
            h2 = _norm_mod(xa, norm2_g[l], mods, 3, 4, out_dtype=BF16, **dims)
            hid = _matmul_swiglu_w32(h2, ffn_w1, ffn_w3, j)
            xa = _matmul_residual(hid, ffn_w2_b, j, xa, mods, 5, **dims)
        else:
            h2 = _norm_mod(xa, norm2_g[l], mods, 3, 4, out_dtype=F32, **dims)
            routes = _router(h2, moe_router[j], moe_router_b[j])
            src, dest, tile_expert, tile_valid = _route_plan(routes, n_experts, MOE_TILE)
            xs = _gather_rows(h2, src, out_dtype=BF16)
            ys = _moe_experts(xs, moe_w1, moe_w3, moe_w2, j, tile_expert, tile_valid, tm=MOE_TILE)
            xa = _moe_combine(xa, ys, routes, dest, mods, 5, **dims)

    out = _final_norm(xa, final_norm_g, lat_rows)
    return out.reshape(n_batch, n_lat, d)
```

```python
import functools
import math

import numpy as np
import jax
import jax.numpy as jnp
from jax import lax
from jax.experimental import pallas as pl
from jax.experimental.pallas import tpu as pltpu

F32 = jnp.float32
BF16 = jnp.bfloat16

HEAD_DIM = 128
GRID_W = 64
CHUNK = 128
NA_ROWS = 8
NA_COLS = 16
ROPE_THETA = 10000.0
EPS = 1e-6
N_MOD = 6
MASK_VALUE = -1e30
ROUTER_LANES = 128
MOE_TILE = 512
FLASH_LOOKAHEAD = 3
ROW_TILES = (1536, 1024, 512, 256)
NA_Q_ROWS = 8
NA_BAND_ROWS = 16
NA_HEADS_PER_STEP = 4
LOG2E = 1.4426950408889634
QK_PRESCALE = HEAD_DIM ** -0.5 * LOG2E
V7X_VMEM_BYTES = 64 * 1024 * 1024
VMEM_LIMIT = V7X_VMEM_BYTES - 8 * 1024 * 1024


def _params(*sem):
    return pltpu.CompilerParams(dimension_semantics=sem, vmem_limit_bytes=VMEM_LIMIT)


def _rms(x, g):
    return x * lax.rsqrt(jnp.mean(x * x, axis=-1, keepdims=True) + EPS) * g


def _ada_down_kernel(c_ref, w_ref, o_ref):
    c = c_ref[...]
    a = (c * jax.nn.sigmoid(c)).astype(BF16)
    o_ref[0] = jnp.dot(a, w_ref[0].astype(BF16), preferred_element_type=F32)


def _ada_up_kernel(t_ref, w_ref, b_ref, o_ref):
    t = t_ref[0].astype(BF16)
    o_ref[0] = jnp.dot(t, w_ref[0].astype(BF16), preferred_element_type=F32) + b_ref[0]


def _ada_mods(cond, w_down, w_up, b_up):
    depth, d, r = w_down.shape
    n_out = w_up.shape[2]
    rows = cond.shape[0]
    tn1 = min(r, 512)
    t = pl.pallas_call(
        _ada_down_kernel,
        grid=(depth, r // tn1),
        in_specs=[pl.BlockSpec((rows, d), lambda l, j: (0, 0)),
                  pl.BlockSpec((1, d, tn1), lambda l, j: (l, 0, j))],
        out_specs=pl.BlockSpec((1, rows, tn1), lambda l, j: (l, 0, j)),
        out_shape=jax.ShapeDtypeStruct((depth, rows, r), F32),
        compiler_params=_params("parallel", "parallel"),
        name="ada_down",
    )(cond, w_down)
    tn2 = min(n_out, 2048)
    return pl.pallas_call(
        _ada_up_kernel,
        grid=(depth, n_out // tn2),
        in_specs=[pl.BlockSpec((1, rows, r), lambda l, j: (l, 0, 0)),
                  pl.BlockSpec((1, r, tn2), lambda l, j: (l, 0, j)),
                  pl.BlockSpec((1, 1, tn2), lambda l, j: (l, 0, j))],
        out_specs=pl.BlockSpec((1, rows, tn2), lambda l, j: (l, 0, j)),
        out_shape=jax.ShapeDtypeStruct((depth, rows, n_out), F32),
        compiler_params=_params("parallel", "parallel"),
        name="ada_up",
    )(t, w_up, b_up.reshape(depth, 1, n_out))


def _norm_mod_kernel(x_ref, g_ref, sh_ref, sc_ref, o_ref):
    y = _rms(x_ref[...], g_ref[...])
    o_ref[...] = (y * (1.0 + sc_ref[0]) + sh_ref[0]).astype(o_ref.dtype)


def _norm_kernel(x_ref, g_ref, o_ref):
    o_ref[...] = _rms(x_ref[...], g_ref[...]).astype(o_ref.dtype)


def _seg_fn(tm, n_lat, n_batch):
    return lambda i: jnp.minimum((i * tm) // n_lat, n_batch)


def _norm_mod(x, g, mods, k_shift, k_scale, *, n_lat, n_batch, out_dtype, tm=256):
    t, d = x.shape
    seg = _seg_fn(tm, n_lat, n_batch)
    return pl.pallas_call(
        _norm_mod_kernel,
        grid=(t // tm,),
        in_specs=[pl.BlockSpec((tm, d), lambda i: (i, 0)),
                  pl.BlockSpec((1, d), lambda i: (0, 0)),
                  pl.BlockSpec((1, 1, d), lambda i: (seg(i) * N_MOD + k_shift, 0, 0)),
                  pl.BlockSpec((1, 1, d), lambda i: (seg(i) * N_MOD + k_scale, 0, 0))],
        out_specs=pl.BlockSpec((tm, d), lambda i: (i, 0)),
        out_shape=jax.ShapeDtypeStruct((t, d), out_dtype),
        compiler_params=_params("parallel"),
        name="norm_mod",
    )(x, g.reshape(1, d), mods, mods)


def _final_norm(x, g, rows, tm=256):
    d = x.shape[1]
    return pl.pallas_call(
        _norm_kernel,
        grid=(rows // tm,),
        in_specs=[pl.BlockSpec((tm, d), lambda i: (i, 0)),
                  pl.BlockSpec((1, d), lambda i: (0, 0))],
        out_specs=pl.BlockSpec((tm, d), lambda i: (i, 0)),
        out_shape=jax.ShapeDtypeStruct((rows, d), F32),
        compiler_params=_params("parallel"),
        name="final_norm",
    )(x, g.reshape(1, d))


def _mmw_kernel(a_ref, w_ref, cs_ref, o_ref, wb_sc):
    @pl.when(pl.program_id(1) == 0)
    def _():
        wb_sc[...] = (w_ref[...] * cs_ref[...]).astype(BF16)

    o_ref[...] = jnp.dot(a_ref[...], wb_sc[...], preferred_element_type=F32).astype(o_ref.dtype)


def _mmw_swiglu_kernel(a_ref, w1_ref, w3_ref, o_ref, w1_sc, w3_sc):
    @pl.when(pl.program_id(1) == 0)
    def _():
        w1_sc[...] = w1_ref[...].astype(BF16)
        w3_sc[...] = w3_ref[...].astype(BF16)

    a = a_ref[...]
    h1 = jnp.dot(a, w1_sc[...], preferred_element_type=F32)
    h3 = jnp.dot(a, w3_sc[...], preferred_element_type=F32)
    o_ref[...] = (h1 * jax.nn.sigmoid(h1) * h3).astype(o_ref.dtype)


def _mm_res_kernel(a_ref, b_ref, x_ref, g_ref, o_ref):
    acc = jnp.dot(a_ref[...], b_ref[...], preferred_element_type=F32)
    o_ref[...] = x_ref[...] + g_ref[0] * acc


def _pick_tile(n, candidates):
    for c in candidates:
        if n % c == 0:
            return c
    raise ValueError(f"no tile for {n}")


def _matmul_w32(a, w, layer, col_scale, col_block, *, out_dtype, tn):
    m, k = a.shape
    n = w.shape[2]
    tm = _pick_tile(m, ROW_TILES)
    return pl.pallas_call(
        _mmw_kernel,
        grid=(n // tn, m // tm),
        in_specs=[pl.BlockSpec((tm, k), lambda j, i: (i, 0)),
                  pl.BlockSpec((None, k, tn), lambda j, i: (layer, 0, col_block(j))),
                  pl.BlockSpec((1, tn), lambda j, i: (0, col_block(j)))],
        out_specs=pl.BlockSpec((tm, tn), lambda j, i: (i, j)),
        out_shape=jax.ShapeDtypeStruct((m, n), out_dtype),
        scratch_shapes=[pltpu.VMEM((k, tn), BF16)],
        compiler_params=_params("parallel", "arbitrary"),
        name="matmul_w32",
    )(a, w, col_scale)


def _matmul_swiglu_w32(a, w1, w3, layer, *, tn=256):
    m, k = a.shape
    n = w1.shape[2]
    tn = min(tn, n)
    tm = _pick_tile(m, ROW_TILES)
    return pl.pallas_call(
        _mmw_swiglu_kernel,
        grid=(n // tn, m // tm),
        in_specs=[pl.BlockSpec((tm, k), lambda j, i: (i, 0)),
                  pl.BlockSpec((None, k, tn), lambda j, i: (layer, 0, j)),
                  pl.BlockSpec((None, k, tn), lambda j, i: (layer, 0, j))],
        out_specs=pl.BlockSpec((tm, tn), lambda j, i: (i, j)),
        out_shape=jax.ShapeDtypeStruct((m, n), BF16),
        scratch_shapes=[pltpu.VMEM((k, tn), BF16), pltpu.VMEM((k, tn), BF16)],
        compiler_params=_params("parallel", "arbitrary"),
        name="matmul_swiglu_w32",
    )(a, w1, w3)


def _matmul_residual(a, w, layer, x, mods, k_gate, *, n_lat, n_batch, tm=512, tn=512):
    m, k = a.shape
    n = w.shape[2]
    tn = min(tn, n)
    seg = _seg_fn(tm, n_lat, n_batch)
    return pl.pallas_call(
        _mm_res_kernel,
        grid=(m // tm, n // tn),
        in_specs=[pl.BlockSpec((tm, k), lambda i, j: (i, 0)),
                  pl.BlockSpec((None, k, tn), lambda i, j: (layer, 0, j)),
                  pl.BlockSpec((tm, tn), lambda i, j: (i, j)),
                  pl.BlockSpec((1, 1, tn), lambda i, j: (seg(i) * N_MOD + k_gate, 0, j))],
        out_specs=pl.BlockSpec((tm, tn), lambda i, j: (i, j)),
        out_shape=jax.ShapeDtypeStruct((m, n), F32),
        compiler_params=_params("parallel", "arbitrary"),
        name="matmul_residual",
    )(a, w, x, mods)


def _sgu_kernel(u_ref, v_ref, gv_ref, ws_ref, bs_ref, ga_ref, o_ref, *, n_chunks, n_groups):
    for c in range(n_chunks):
        rows = slice(c * CHUNK, (c + 1) * CHUNK)
        u = jax.nn.gelu(u_ref[rows, :].astype(F32), approximate=True)
        v = jax.nn.gelu(v_ref[rows, :].astype(F32), approximate=True)
        vb = _rms(v, gv_ref[...]).astype(BF16)
        parts = [jnp.dot(ws_ref[g], vb[:, g * HEAD_DIM:(g + 1) * HEAD_DIM],
                         preferred_element_type=F32) for g in range(n_groups)]
        y = u * (jnp.concatenate(parts, axis=1) + bs_ref[...])
        o_ref[rows, :] = _rms(y, ga_ref[...]).astype(o_ref.dtype)


def _sgu(p, g_v, w_s, b_s, g_a, *, width, ta=256):
    t = p.shape[0]
    n_groups = width // HEAD_DIM
    bias =jnp.repeat(b_s.T, HEAD_DIM, axis=1)
    return pl.pallas_call(
        functools.partial(_sgu_kernel, n_chunks=ta // CHUNK, n_groups=n_groups),
        grid=(t // ta,),
        in_specs=[pl.BlockSpec((ta, width), lambda i: (i, 0)),
                  pl.BlockSpec((ta, width), lambda i: (i, 1)),
                  pl.BlockSpec((1, width), lambda i: (0, 0)),
                  pl.BlockSpec((n_groups, CHUNK, CHUNK), lambda i: (0, 0, 0)),
                  pl.BlockSpec((CHUNK, width), lambda i: (0, 0)),
                  pl.BlockSpec((1, width), lambda i: (0, 0))],
        out_specs=pl.BlockSpec((ta, width), lambda i: (i, 0)),
        out_shape=jax.ShapeDtypeStruct((t, width), BF16),
        compiler_params=_params("parallel"),
        name="sgu",
    )(p, p, g_v.reshape(1, width), w_s.astype(BF16), bias, g_a.reshape(1, width))


def _na_bias_tables(rpb, rows):
    n_blocks = rows // NA_Q_ROWS
    cols = np.arange(GRID_W)
    col_start = np.clip(cols - NA_COLS // 2, 0, GRID_W - NA_COLS)
    dc = cols[None, :] - cols[:, None]
    col_ok = (cols[None, :] >= col_start[:, None]) & (cols[None, :] < col_start[:, None] + NA_COLS)
    col_idx = np.where(col_ok, dc + NA_COLS - 1, 0)
    n_dr = 2 * NA_ROWS - 1
    t = jnp.where(col_ok[None, None], rpb[:, :, col_idx] * LOG2E, MASK_VALUE)
    t = jnp.concatenate([t, jnp.full((rpb.shape[0], 1, GRID_W, GRID_W), MASK_VALUE, F32)], axis=1)
    slab = np.full((3, NA_Q_ROWS, NA_BAND_ROWS), n_dr, np.int32)
    rep = [0, min(1, n_blocks - 1), n_blocks - 1]
    for ty, blk in enumerate(rep):
        band0 = int(np.clip(blk * NA_Q_ROWS - NA_ROWS // 2, 0, rows - NA_BAND_ROWS))
        for rq in range(NA_Q_ROWS):
            r = blk * NA_Q_ROWS + rq
            start = int(np.clip(r - NA_ROWS // 2, 0, rows - NA_ROWS))
            for kr in range(NA_BAND_ROWS):
                ar = band0 + kr
                if start <= ar < start + NA_ROWS:
                    slab[ty, rq, kr] = ar - r + NA_ROWS - 1
    pairs = slab.reshape(3, NA_Q_ROWS, NA_BAND_ROWS // 2, 2)
    uniq, inverse = np.unique(pairs.reshape(-1, 2), axis=0, return_inverse=True)
    t2 = jnp.concatenate([t[:, uniq[:, 0]], t[:, uniq[:, 1]]], axis=-1)
    return t2, jnp.asarray(inverse.reshape(-1), jnp.int32)


def _na_kernel(pair_ref, q_ref, k_ref, v_ref, kc_ref, vc_ref, bias_ref, o_ref, *, rows, rc, n_blocks,
               heads):
    i = pl.program_id(2)
    block_type = jnp.where(i == 0, 0, jnp.where(i == n_blocks - 1, 2, 1))
    pairs_per_row = NA_BAND_ROWS // 2

    def bias_rows(hh, rq):
        base = (block_type * NA_Q_ROWS + rq) * pairs_per_row
        return jnp.concatenate([bias_ref[hh, pair_ref[base + m]] for m in range(pairs_per_row)], axis=1)

    band0 = jnp.clip(i * NA_Q_ROWS - NA_ROWS // 2, 0, rows - NA_BAND_ROWS)
    start = pl.multiple_of(band0 * GRID_W, GRID_W)
    nb = NA_BAND_ROWS * GRID_W
    dn = (((1,), (1,)), ((), ()))
    n_chunks = q_ref.shape[0] // rc
    n_loc = nb // HEAD_DIM
    units = [(hh, c) for hh in range(heads) for c in range(n_chunks)]

    def lanes(hh):
        return slice(hh * HEAD_DIM, (hh + 1) * HEAD_DIM)

    def scores(unit):
        hh, c = unit
        q = q_ref[c * rc:(c + 1) * rc, lanes(hh)]
        return (lax.dot_general(q, k_ref[pl.ds(start, nb), lanes(hh)], dn, preferred_element_type=F32),
                lax.dot_general(q, kc_ref[:, lanes(hh)], dn, preferred_element_type=F32))

    nxt = scores(units[0])
    for u, (hh, c) in enumerate(units):
        s_loc, s_ctx = nxt
        if u + 1 < len(units):
            nxt = scores(units[u + 1])
        vb = v_ref[pl.ds(start, nb), lanes(hh)]
        vc = vc_ref[:, lanes(hh)]
        q_rows = rc // GRID_W
        s_loc = s_loc + jnp.concatenate([bias_rows(hh, c * q_rows + r) for r in range(q_rows)], axis=0)
        cols = ([s_loc[:, j * HEAD_DIM:(j + 1) * HEAD_DIM] for j in range(n_loc)]
                + [s_ctx[:, j * HEAD_DIM:(j + 1) * HEAD_DIM] for j in range(s_ctx.shape[1] // HEAD_DIM)])
        m = jnp.max(functools.reduce(jnp.maximum, cols), axis=1, keepdims=True)
        ps = [jnp.exp2(x - m) for x in cols]
        l = jnp.sum(functools.reduce(jnp.add, ps), axis=1, keepdims=True)
        p_loc = jnp.concatenate([x.astype(BF16) for x in ps[:n_loc]], axis=1)
        p_ctx = jnp.concatenate([x.astype(BF16) for x in ps[n_loc:]], axis=1)
        o = (jnp.dot(p_loc, vb, preferred_element_type=F32)
             + jnp.dot(p_ctx, vc, preferred_element_type=F32))
        o_ref[c * rc:(c + 1) * rc, lanes(hh)] = (o / l).astype(o_ref.dtype)


def _neighbourhood_attention(p, bias, pair_idx, *, n_batch, n_lat, n_ctx, q_col, k_col, v_col, n_heads):
    rows = n_lat // GRID_W
    tq = NA_Q_ROWS * GRID_W
    n_blocks = n_lat // tq
    ctx_blk0 = (n_batch * n_lat) // n_ctx
    n_pairs = bias.shape[1]
    hp = math.gcd(NA_HEADS_PER_STEP, n_heads, q_col, k_col, v_col)
    w = hp * HEAD_DIM
    qc, kc, vc = q_col // hp, k_col // hp, v_col // hp
    return pl.pallas_call(
        functools.partial(_na_kernel, rows=rows, rc=min(256, tq), n_blocks=n_blocks, heads=hp),
        grid_spec=pltpu.PrefetchScalarGridSpec(
            num_scalar_prefetch=1, grid=(n_heads // hp, n_batch, n_blocks),
            in_specs=[pl.BlockSpec((tq, w), lambda h, b, i, pr: (b * n_blocks + i, qc + h)),
                      pl.BlockSpec((n_lat, w), lambda h, b, i, pr: (b, kc + h)),
                      pl.BlockSpec((n_lat, w), lambda h, b, i, pr: (b, vc + h)),
                      pl.BlockSpec((n_ctx, w), lambda h, b, i, pr: (ctx_blk0 + b, kc + h)),
                      pl.BlockSpec((n_ctx, w), lambda h, b, i, pr: (ctx_blk0 + b, vc + h)),
                      pl.BlockSpec((hp, n_pairs, GRID_W, 2 * GRID_W), lambda h, b, i, pr: (h, 0, 0, 0))],
            out_specs=pl.BlockSpec((tq, w), lambda h, b, i, pr: (b * n_blocks + i, h))),
        out_shape=jax.ShapeDtypeStruct((n_batch * n_lat, n_heads * HEAD_DIM), BF16),
        compiler_params=_params("parallel", "parallel", "arbitrary"),
        name="neighbourhood_attention",
    )(pair_idx, p, p, p, p, p, bias)


def _rope_tables(n_lat, n_ctx):
    t = jnp.arange(n_lat, dtype=jnp.int32)
    pos = jnp.stack([t // GRID_W, t % GRID_W], axis=-1).astype(F32)
    n_freq = HEAD_DIM // 4
    inv = 1.0 / (ROPE_THETA ** (jnp.arange(n_freq, dtype=F32) / n_freq))
    ang = pos[:, :, None] * inv
    cos, sin = jnp.cos(ang), jnp.sin(ang)
    cos_t = jnp.concatenate([cos[:, 0], cos[:, 0], cos[:, 1], cos[:, 1]], axis=-1)
    sin_t = jnp.concatenate([-sin[:, 0], sin[:, 0], -sin[:, 1], sin[:, 1]], axis=-1)
    cos_t = jnp.concatenate([cos_t, jnp.ones((n_ctx, HEAD_DIM), F32)], axis=0)
    sin_t = jnp.concatenate([sin_t, jnp.zeros((n_ctx, HEAD_DIM), F32)], axis=0)
    return cos_t, sin_t


def _qk_prep_kernel(q_ref, k_ref, v_ref, cos_ref, sin_ref, gq_ref, gk_ref,
                    qo_ref, ko_ref, vo_ref, *, n_q, n_kv):
    cos = cos_ref[...]
    sin = sin_ref[...]
    quarter = HEAD_DIM // 4
    r_idx = lax.broadcasted_iota(jnp.int32, (HEAD_DIM, HEAD_DIM), 0)
    c_idx = lax.broadcasted_iota(jnp.int32, (HEAD_DIM, HEAD_DIM), 1)
    exchange = jnp.where(r_idx == jnp.bitwise_xor(c_idx, quarter), 1.0, 0.0).astype(F32)

    def prep(x, g, out_scale):
        y = _rms(x.astype(F32), g)
        swapped = jnp.dot(y, exchange, preferred_element_type=F32, precision=lax.Precision.HIGHEST)
        return ((y * cos + swapped * sin) * out_scale).astype(BF16)

    for h in range(n_q):
        cs = slice(h * HEAD_DIM, (h + 1) * HEAD_DIM)
        qo_ref[:, cs] = prep(q_ref[:, cs], gq_ref[...], QK_PRESCALE)
    ones = jnp.ones((v_ref.shape[0], HEAD_DIM), BF16)
    for h in range(n_kv):
        cs = slice(h * HEAD_DIM, (h + 1) * HEAD_DIM)
        ko_ref[0, :, cs] = prep(k_ref[:, cs], gk_ref[...], 1.0)
        vo_ref[0, :, 2 * h * HEAD_DIM:(2 * h + 1) * HEAD_DIM] = v_ref[:, cs]
        vo_ref[0, :, (2 * h + 1) * HEAD_DIM:(2 * h + 2) * HEAD_DIM] = ones


def _qk_prep(p, cos_t, sin_t, g_q, g_k, *, n_batch, n_lat, n_ctx, q_off, n_q, k_off, v_off, n_kv):
    t = p.shape[0]
    tm = n_ctx
    lat_tiles = n_lat // tm
    n_lat_tiles = n_batch * lat_tiles

    def pos_blk(i):
        return jnp.where(i < n_lat_tiles, i % lat_tiles, lat_tiles)

    def kv_blk(i):
        c = i - n_lat_tiles
        return (jnp.where(i < n_lat_tiles, i // lat_tiles, c),
                jnp.where(i < n_lat_tiles, i % lat_tiles, lat_tiles), 0)

    wq, wkv = n_q * HEAD_DIM, n_kv * HEAD_DIM
    return pl.pallas_call(
        functools.partial(_qk_prep_kernel, n_q=n_q, n_kv=n_kv),
        grid=(t // tm,),
        in_specs=[pl.BlockSpec((tm, wq), lambda i: (i, q_off)),
                  pl.BlockSpec((tm, wkv), lambda i: (i, k_off)),
                  pl.BlockSpec((tm, wkv), lambda i: (i, v_off)),
                  pl.BlockSpec((tm, HEAD_DIM), lambda i: (pos_blk(i), 0)),
                  pl.BlockSpec((tm, HEAD_DIM), lambda i: (pos_blk(i), 0)),
                  pl.BlockSpec((1, HEAD_DIM), lambda i: (0, 0)),
                  pl.BlockSpec((1, HEAD_DIM), lambda i: (0, 0))],
        out_specs=[pl.BlockSpec((tm, wq), lambda i: (i, 0)),
                   pl.BlockSpec((1, tm, wkv), kv_blk),
                   pl.BlockSpec((1, tm, 2 * wkv), kv_blk)],
        out_shape=[jax.ShapeDtypeStruct((t, wq), BF16),
                   jax.ShapeDtypeStruct((n_batch, n_lat + n_ctx, wkv), BF16),
                   jax.ShapeDtypeStruct((n_batch, n_lat + n_ctx, 2 * wkv), BF16)],
        compiler_params=_params("parallel"),
        name="qk_prep",
    )(p, p, p, cos_t, sin_t, g_q.reshape(1, HEAD_DIM), g_k.reshape(1, HEAD_DIM))


def _flash_kernel(q_ref, k_ref, v_ref, o_ref, m_sc, l_sc, acc_sc, *, group, tq):
    kj = pl.program_id(3)

    @pl.when(kj == 0)
    def _():
        m_sc[...] = jnp.full_like(m_sc, -jnp.inf)
        l_sc[...] = jnp.zeros_like(l_sc)
        acc_sc[...] = jnp.zeros_like(acc_sc)

    if group == 1:
        q = q_ref[...]
    else:
        q = jnp.concatenate([q_ref[:, g * HEAD_DIM:(g + 1) * HEAD_DIM] for g in range(group)], axis=0)
    s = lax.dot_general(q, k_ref[0], (((1,), (1,)), ((), ())), preferred_element_type=F32)
    m_prev = m_sc[...]
    m_new = jnp.maximum(m_prev, jnp.max(s, axis=1, keepdims=True))
    alpha = jnp.exp2(m_prev - m_new)
    p = jnp.exp2(s - m_new)
    l_sc[...] = alpha * l_sc[...] + jnp.sum(p, axis=1, keepdims=True)
    acc_sc[...] = alpha * acc_sc[...] + jnp.dot(p.astype(BF16), v_ref[0], preferred_element_type=F32)
    m_sc[...] = m_new

    @pl.when(kj == pl.num_programs(3) - 1)
    def _():
        o = acc_sc[...] / l_sc[...]
        for g in range(group):
            o_ref[:, g * HEAD_DIM:(g + 1) * HEAD_DIM] = o[g * tq:(g + 1) * tq].astype(o_ref.dtype)


def _flash(q2d, k3d, v3d, *, n_batch, n_q_rows, q_row0, q_rows_per_batch, q_col, n_kv_heads,
           group, kv_row0, n_kv_rows, tq=256):
    tq = min(tq, n_q_rows)
    tk = _pick_tile(n_kv_rows, (1024, 768, 512, 256))
    qb0, qbb = q_row0 // tq, q_rows_per_batch // tq
    kb0 = kv_row0 // tk
    gw = group * HEAD_DIM
    return pl.pallas_call(
        functools.partial(_flash_kernel, group=group, tq=tq),
        grid=(n_batch, n_kv_heads, n_q_rows // tq, n_kv_rows // tk),
        in_specs=[pl.BlockSpec((tq, gw), lambda b, h, i, j: (qb0 + b * qbb + i, q_col + h)),
                  pl.BlockSpec((1, tk, HEAD_DIM), lambda b, h, i, j: (b, kb0 + j, h)),
                  pl.BlockSpec((1, tk, HEAD_DIM), lambda b, h, i, j: (b, kb0 + j, h))],
        out_specs=pl.BlockSpec((tq, gw), lambda b, h, i, j: (b * (n_q_rows // tq) + i, h)),
        out_shape=jax.ShapeDtypeStruct((n_batch * n_q_rows, n_kv_heads * gw), BF16),
        scratch_shapes=[pltpu.VMEM((group * tq, 1), F32),
                        pltpu.VMEM((group * tq, 1), F32),
                        pltpu.VMEM((group * tq, HEAD_DIM), F32)],
        compiler_params=_params("parallel", "parallel", "parallel", "arbitrary"),
        name="flash_attention",
    )(q2d, k3d, v3d)


def _flash_gqa_kernel(q_ref, k_ref, v_ref, o_ref, m_sc, acc_sc, *, group, tq, tk, rc):
    kj = pl.program_id(3)

    @pl.when(kj == 0)
    def _():
        m_sc[...] = jnp.full_like(m_sc, -jnp.inf)
        acc_sc[...] = jnp.zeros_like(acc_sc)

    n_cols = tk // HEAD_DIM
    k = k_ref[0]
    v = v_ref[0]
    chunks = [(g, r0) for g in range(group) for r0 in range(0, tq, rc)]

    def scores(chunk):
        g, r0 = chunk
        q = q_ref[r0:r0 + rc, g * HEAD_DIM:(g + 1) * HEAD_DIM]
        return lax.dot_general(q, k, (((1,), (1,)), ((), ())), preferred_element_type=F32)

    pending = [scores(ch) for ch in chunks[:FLASH_LOOKAHEAD]]
    for ci, (g, r0) in enumerate(chunks):
        if ci + FLASH_LOOKAHEAD < len(chunks):
            pending.append(scores(chunks[ci + FLASH_LOOKAHEAD]))
        s = pending.pop(0)
        rows = slice(g * tq + r0, g * tq + r0 + rc)
        cols = [s[:, i * HEAD_DIM:(i + 1) * HEAD_DIM] for i in range(n_cols)]
        col_max = functools.reduce(jnp.maximum, cols)
        m_prev = m_sc[rows, :]
        m_new = jnp.maximum(m_prev, jnp.max(col_max, axis=1, keepdims=True))
        alpha = jnp.exp2(m_prev - m_new)
        p = jnp.concatenate([jnp.exp2(c - m_new).astype(BF16) for c in cols], axis=1)
        pv = jnp.dot(p, v, preferred_element_type=F32)
        acc_sc[rows, :] = jnp.concatenate([alpha, alpha], axis=1) * acc_sc[rows, :] + pv
        m_sc[rows, :] = m_new

    @pl.when(kj == pl.num_programs(3) - 1)
    def _():
        o = acc_sc[:, :HEAD_DIM] / acc_sc[:, HEAD_DIM:]
        for g in range(group):
            o_ref[:, g * HEAD_DIM:(g + 1) * HEAD_DIM] = o[g * tq:(g + 1) * tq].astype(o_ref.dtype)


def _flash_gqa(q2d, k3d, v3d, *, n_batch, n_q_rows, q_row0, q_rows_per_batch, n_kv_heads, group,
               kv_row0, n_kv_rows, tq=512, rc=512):
    tq = min(tq, n_q_rows)
    rc = min(rc, tq)
    tk = _pick_tile(n_kv_rows, (2816, 1408, 1024, 768, 512, 256))
    qb0, qbb = q_row0 // tq, q_rows_per_batch // tq
    kb0 = kv_row0 // tk
    gw = group * HEAD_DIM
    return pl.pallas_call(
        functools.partial(_flash_gqa_kernel, group=group, tq=tq, tk=tk, rc=rc),
        grid=(n_batch, n_kv_heads, n_q_rows // tq, n_kv_rows // tk),
        in_specs=[pl.BlockSpec((tq, gw), lambda b, h, i, j: (qb0 + b * qbb + i, h)),
                  pl.BlockSpec((1, tk, HEAD_DIM), lambda b, h, i, j: (b, kb0 + j, h)),
                  pl.BlockSpec((1, tk, 2 * HEAD_DIM), lambda b, h, i, j: (b, kb0 + j, h))],
        out_specs=pl.BlockSpec((tq, gw), lambda b, h, i, j: (b * (n_q_rows // tq) + i, h)),
        out_shape=jax.ShapeDtypeStruct((n_batch * n_q_rows, n_kv_heads * gw), BF16),
        scratch_shapes=[pltpu.VMEM((group * tq, HEAD_DIM), F32),
                        pltpu.VMEM((group * tq, 2 * HEAD_DIM), F32)],
        compiler_params=_params("parallel", "parallel", "parallel", "arbitrary"),
        name="flash_gqa",
    )(q2d, k3d, v3d)


def _merge_kernel(ya_ref, yb_lat_ref, yb_ctx_ref, yc_lat_ref, yc_ctx_ref, gb_ref, gc_ref, o_ref,
                  *, wa, wb, n_lat_tiles):
    o_ref[:, :wa] = ya_ref[...]

    def emit(yb_ref, yc_ref):
        o_ref[:, wa:wa + wb] = _rms(yb_ref[...].astype(F32), gb_ref[...]).astype(o_ref.dtype)
        o_ref[:, wa + wb:] = _rms(yc_ref[...].astype(F32), gc_ref[...]).astype(o_ref.dtype)

    is_lat = pl.program_id(0) < n_lat_tiles
    pl.when(is_lat)(lambda: emit(yb_lat_ref, yc_lat_ref))
    pl.when(jnp.logical_not(is_lat))(lambda: emit(yb_ctx_ref, yc_ctx_ref))


def _merge(ya, yb_lat, yb_ctx, yc_lat, yc_ctx, g_b, g_c, tm=256):
    t, wa = ya.shape
    wb, wc = yb_lat.shape[1], yc_lat.shape[1]
    n_lat_tiles = yb_lat.shape[0] // tm

    def lat(i):
        return (jnp.minimum(i, n_lat_tiles - 1), 0)

    def ctx(i):
        return (jnp.maximum(i - n_lat_tiles, 0), 0)

    return pl.pallas_call(
        functools.partial(_merge_kernel, wa=wa, wb=wb, n_lat_tiles=n_lat_tiles),
        grid=(t // tm,),
        in_specs=[pl.BlockSpec((tm, wa), lambda i: (i, 0)),
                  pl.BlockSpec((tm, wb), lat), pl.BlockSpec((tm, wb), ctx),
                  pl.BlockSpec((tm, wc), lat), pl.BlockSpec((tm, wc), ctx),
                  pl.BlockSpec((1, wb), lambda i: (0, 0)),
                  pl.BlockSpec((1, wc), lambda i: (0, 0))],
        out_specs=pl.BlockSpec((tm, wa + wb + wc), lambda i: (i, 0)),
        out_shape=jax.ShapeDtypeStruct((t, wa + wb + wc), BF16),
        compiler_params=_params("parallel"),
        name="merge_groups",
    )(ya, yb_lat, yb_ctx, yc_lat, yc_ctx, g_b.reshape(1, wb), g_c.reshape(1, wc))


def _router_kernel(h_ref, w_ref, b_ref, o_ref):
    logits = jnp.dot(h_ref[...].astype(BF16), w_ref[...], preferred_element_type=F32) + b_ref[...]
    lane = lax.broadcasted_iota(jnp.int32, logits.shape, 1).astype(F32)
    m1 = jnp.max(logits, axis=1, keepdims=True)
    i1 = jnp.min(jnp.where(logits == m1, lane, float(ROUTER_LANES)), axis=1, keepdims=True)
    rest = jnp.where(lane == i1, -jnp.inf, logits)
    m2 = jnp.max(rest, axis=1, keepdims=True)
    i2 = jnp.min(jnp.where(rest == m2, lane, float(ROUTER_LANES)), axis=1, keepdims=True)
    e2 = jnp.exp(m2 - m1)
    g1 = 1.0 / (1.0 + e2)
    o_ref[...] = jnp.where(lane == 0.0, i1,
                           jnp.where(lane == 1.0, i2,
                                     jnp.where(lane == 2.0, g1,
                                               jnp.where(lane == 3.0, e2 * g1, 0.0))))


def _route_plan(routes, n_experts, tm):
    t = routes.shape[0]
    a = 2 * t
    flat_e = routes[:, :2].astype(jnp.int32).reshape(a)
    onehot = (flat_e[:, None] == jnp.arange(n_experts, dtype=jnp.int32)[None, :]).astype(jnp.int32)
    csum = jnp.cumsum(onehot, axis=0)
    counts = csum[-1]
    rank = jnp.take_along_axis(csum, flat_e[:, None], axis=1)[:, 0] - 1
    padded = ((counts + tm - 1) // tm) * tm
    ends = jnp.cumsum(padded)
    starts = ends - padded
    dest = starts[flat_e] + rank
    ustarts = jnp.cumsum(counts) - counts
    order = jnp.argsort(flat_e, stable=True).astype(jnp.int32)
    n_rows = a + n_experts * tm
    row = jnp.arange(n_rows, dtype=jnp.int32)
    row_e = jnp.minimum(jnp.searchsorted(ends, row, side="right"), n_experts - 1).astype(jnp.int32)
    k = row - starts[row_e]
    src_a = order[jnp.clip(ustarts[row_e] + k, 0, a - 1)]
    src = jnp.where(k < counts[row_e], src_a // 2, 0).astype(jnp.int32)
    tile_start = jnp.arange(n_rows // tm, dtype=jnp.int32) * tm
    tile_valid = (tile_start < ends[-1]).astype(jnp.int32)
    tile_expert = row_e[jnp.minimum(tile_start, ends[-1] - 1)]
    return src, dest.astype(jnp.int32), tile_expert, tile_valid


def _gather_rows_kernel(src_ref, h_hbm, o_ref, buf, sem, *, tg):
    i = pl.program_id(0)

    def copy(tile, j):
        slot = tile % 2
        return pltpu.make_async_copy(h_hbm.at[pl.ds(src_ref[tile * tg + j], 1)],
                                     buf.at[slot, pl.ds(j, 1)], sem.at[slot])

    def issue(tile):
        def body(j, carry):
            copy(tile, j).start()
            return carry
        lax.fori_loop(0, tg, body, 0, unroll=8)

    pl.when(i == 0)(lambda: issue(0))
    pl.when(i + 1 < pl.num_programs(0))(lambda: issue(i + 1))

    slot = i % 2
    pltpu.make_async_copy(h_hbm.at[pl.ds(0, tg)], buf.at[slot], sem.at[slot]).wait()
    o_ref[...] = buf[slot].astype(o_ref.dtype)


def _gather_rows(h, src, *, out_dtype, tg=512):
    n = src.shape[0]
    d = h.shape[1]
    return pl.pallas_call(
        functools.partial(_gather_rows_kernel, tg=tg),
        grid_spec=pltpu.PrefetchScalarGridSpec(
            num_scalar_prefetch=1, grid=(n // tg,),
            in_specs=[pl.BlockSpec(memory_space=pl.ANY)],
            out_specs=pl.BlockSpec((tg, d), lambda i, s: (i, 0)),
            scratch_shapes=[pltpu.VMEM((2, tg, d), h.dtype), pltpu.SemaphoreType.DMA((2,))]),
        out_shape=jax.ShapeDtypeStruct((n, d), out_dtype),
        compiler_params=_params("arbitrary"),
        name="moe_gather",
    )(src, h)


def _expert_changed(te_ref, i):
    return jnp.logical_or(i == 0, te_ref[i] != te_ref[jnp.maximum(i - 1, 0)])


def _moe_up_kernel(te_ref, tv_ref, x_ref, w1_ref, w3_ref, o_ref, w1_sc, w3_sc):
    i = pl.program_id(1)
    valid = tv_ref[i] == 1

    @pl.when(jnp.logical_and(valid, _expert_changed(te_ref, i)))
    def _():
        w1_sc[...] = w1_ref[...].astype(BF16)
        w3_sc[...] = w3_ref[...].astype(BF16)

    @pl.when(valid)
    def _():
        a = x_ref[...]
        h1 = jnp.dot(a, w1_sc[...], preferred_element_type=F32)
        h3 = jnp.dot(a, w3_sc[...], preferred_element_type=F32)
        o_ref[...] = (h1 * jax.nn.sigmoid(h1) * h3).astype(o_ref.dtype)

    @pl.when(jnp.logical_not(valid))
    def _():
        o_ref[...] = jnp.zeros_like(o_ref)


def _moe_down_kernel(te_ref, tv_ref, a_ref, w_ref, o_ref, w_sc):
    i = pl.program_id(1)
    valid = tv_ref[i] == 1

    @pl.when(jnp.logical_and(valid, _expert_changed(te_ref, i)))
    def _():
        w_sc[...] = w_ref[...].astype(BF16)

    @pl.when(valid)
    def _():
        o_ref[...] = jnp.dot(a_ref[...], w_sc[...], preferred_element_type=F32)

    @pl.when(jnp.logical_not(valid))
    def _():
        o_ref[...] = jnp.zeros_like(o_ref)


def _moe_experts(xs, w1, w3, w2, layer, tile_expert, tile_valid, *, tm, tn_up=512, tn_down=2048):
    rows, d = xs.shape
    f = w1.shape[-1]
    tn_up, tn_down = min(tn_up, f), min(tn_down, d)
    n_tiles = rows // tm
    hid = pl.pallas_call(
        _moe_up_kernel,
        grid_spec=pltpu.PrefetchScalarGridSpec(
            num_scalar_prefetch=2, grid=(f // tn_up, n_tiles),
            in_specs=[pl.BlockSpec((tm, d), lambda j, i, te, tv: (i, 0)),
                      pl.BlockSpec((None, None, d, tn_up), lambda j, i, te, tv: (layer, te[i], 0, j)),
                      pl.BlockSpec((None, None, d, tn_up), lambda j, i, te, tv: (layer, te[i], 0, j))],
            out_specs=pl.BlockSpec((tm, tn_up), lambda j, i, te, tv: (i, j)),
            scratch_shapes=[pltpu.VMEM((d, tn_up), BF16), pltpu.VMEM((d, tn_up), BF16)]),
        out_shape=jax.ShapeDtypeStruct((rows, f), BF16),
        compiler_params=_params("parallel", "arbitrary"),
        name="moe_up",
    )(tile_expert, tile_valid, xs, w1, w3)
    return pl.pallas_call(
        _moe_down_kernel,
        grid_spec=pltpu.PrefetchScalarGridSpec(
            num_scalar_prefetch=2, grid=(d // tn_down, n_tiles),
            in_specs=[pl.BlockSpec((tm, f), lambda j, i, te, tv: (i, 0)),
                      pl.BlockSpec((None, None, f, tn_down), lambda j, i, te, tv: (layer, te[i], 0, j))],
            out_specs=pl.BlockSpec((tm, tn_down), lambda j, i, te, tv: (i, j)),
            scratch_shapes=[pltpu.VMEM((f, tn_down), BF16)]),
        out_shape=jax.ShapeDtypeStruct((rows, d), F32),
        compiler_params=_params("parallel", "arbitrary"),
        name="moe_down",
    )(tile_expert, tile_valid, hid, w2)


def _moe_combine_kernel(dest_ref, x_ref, r_ref, g_ref, y_hbm, o_ref, ybuf, sem, *, tc):
    i = pl.program_id(0)

    def copy(tile, j, k):
        slot = tile % 2
        return pltpu.make_async_copy(y_hbm.at[pl.ds(dest_ref[(tile * tc + j) * 2 + k], 1)],
                                     ybuf.at[slot, k, pl.ds(j, 1)], sem.at[slot])

    def issue(tile):
        def body(j, carry):
            copy(tile, j, 0).start()
            copy(tile, j, 1).start()
            return carry
        lax.fori_loop(0, tc, body, 0, unroll=4)

    pl.when(i == 0)(lambda: issue(0))
    pl.when(i + 1 < pl.num_programs(0))(lambda: issue(i + 1))

    slot = i % 2
    for k in range(2):
        pltpu.make_async_copy(y_hbm.at[pl.ds(0, tc)], ybuf.at[slot, k], sem.at[slot]).wait()
    gate = g_ref[0]
    rows = 32

    def row_block(c, carry):
        rs = pl.ds(pl.multiple_of(c * rows, rows), rows)
        r = r_ref[rs, :]
        y = r[:, 2:3] * ybuf[slot, 0, rs, :] + r[:, 3:4] * ybuf[slot, 1, rs, :]
        o_ref[rs, :] = x_ref[rs, :] + gate * y
        return carry

    lax.fori_loop(0, tc // rows, row_block, 0, unroll=2)


def _moe_combine(x, y, routes, dest, mods, k_gate, *, n_lat, n_batch, tc=256):
    t, d = x.shape
    seg = _seg_fn(tc, n_lat, n_batch)
    return pl.pallas_call(
        functools.partial(_moe_combine_kernel, tc=tc),
        grid_spec=pltpu.PrefetchScalarGridSpec(
            num_scalar_prefetch=1, grid=(t // tc,),
            in_specs=[pl.BlockSpec((tc, d), lambda i, ds: (i, 0)),
                      pl.BlockSpec((tc, ROUTER_LANES), lambda i, ds: (i, 0)),
                      pl.BlockSpec((1, 1, d), lambda i, ds: (seg(i) * N_MOD + k_gate, 0, 0)),
                      pl.BlockSpec(memory_space=pl.ANY)],
            out_specs=pl.BlockSpec((tc, d), lambda i, ds: (i, 0)),
            scratch_shapes=[pltpu.VMEM((2, 2, tc, d), F32), pltpu.SemaphoreType.DMA((2,))]),
        out_shape=jax.ShapeDtypeStruct((t, d), F32),
        compiler_params=_params("arbitrary"),
        name="moe_combine",
    )(dest, x, routes, mods, y)


def _router(h, w_r, b_r, tm=512):
    t, d = h.shape
    e = w_r.shape[1]
    w_pad = jnp.zeros((d, ROUTER_LANES), BF16).at[:, :e].set(w_r.astype(BF16))
    b_pad = jnp.full((1, ROUTER_LANES), -jnp.inf, F32).at[0, :e].set(b_r)
    return pl.pallas_call(
        _router_kernel,
        grid=(t // tm,),
        in_specs=[pl.BlockSpec((tm, d), lambda i: (i, 0)),
                  pl.BlockSpec((d, ROUTER_LANES), lambda i: (0, 0)),
                  pl.BlockSpec((1, ROUTER_LANES), lambda i: (0, 0))],
        out_specs=pl.BlockSpec((tm, ROUTER_LANES), lambda i: (i, 0)),
        out_shape=jax.ShapeDtypeStruct((t, ROUTER_LANES), F32),
        compiler_params=_params("parallel"),
        name="router",
    )(h, w_pad, b_pad)


def kernel(x, c, ctx, c_ctx, ada_down, ada_up, ada_bias, norm1_g, norm2_g, w_in, sgu_norm_g, sgu_w,
           sgu_b, na_rpb, qk_norm_g, group_norm_g, w_out, ffn_w1, ffn_w3, ffn_w2, moe_router,
           moe_router_b, moe_w1, moe_w3, moe_w2, final_norm_g):
    n_batch, n_lat, d = x.shape
    n_ctx = ctx.shape[1]
    depth = w_in.shape[0]
    mix = w_out.shape[1]
    wa = wb = mix // 4
    wc = mix - wa - wb
    n_b_heads = wb // HEAD_DIM
    n_c_heads = wc // HEAD_DIM
    n_kv_heads = n_c_heads // 4
    wkv = n_kv_heads * HEAD_DIM
    group = n_c_heads // n_kv_heads
    n_experts = moe_router.shape[-1]
    dims = dict(n_lat=n_lat, n_batch=n_batch)
    lat_rows = n_batch * n_lat

    off_bq, off_bk, off_bv = 2 * wa, 2 * wa + wb, 2 * wa + 2 * wb
    src_cq = 2 * wa + 3 * wb
    off_ck = src_cq
    off_cv = off_ck + wkv
    off_cq = off_cv + wkv
    in_width = off_cq + wc
    tn_in = math.gcd(512, wkv)
    nb_pre, nb_kv, nb_q = src_cq // tn_in, 2 * wkv // tn_in, wc // tn_in

    def in_col_block(j):
        return jnp.where(j < nb_pre, j, jnp.where(j < nb_pre + nb_kv, j + nb_q, j - nb_kv))

    in_col_scale = jnp.ones((1, in_width), F32).at[:, off_bq:off_bk].set(QK_PRESCALE)

    xa = jnp.concatenate([x.reshape(lat_rows, d), ctx.reshape(n_batch * n_ctx, d)], axis=0)
    cond = jnp.zeros((8, d), F32).at[:n_batch].set(c).at[n_batch].set(c_ctx)
    mods_all = _ada_mods(cond, ada_down, ada_up, ada_bias)
    cos_t, sin_t = _rope_tables(n_lat, n_ctx)
    w_out_b, ffn_w2_b = w_out.astype(BF16), ffn_w2.astype(BF16)

    for l in range(depth):
        mods = mods_all[l].reshape(8 * N_MOD, 1, d)
        h = _norm_mod(xa, norm1_g[l], mods, 0, 1, out_dtype=BF16, **dims)
        p = _matmul_w32(h, w_in, l, in_col_scale, in_col_block, out_dtype=BF16, tn=tn_in)
        ya = _sgu(p, sgu_norm_g[l], sgu_w[l], sgu_b[l], group_norm_g[l, :wa], width=wa)
        bias, pair_idx = _na_bias_tables(na_rpb[l], n_lat // GRID_W)
        yb_lat = _neighbourhood_attention(
            p, bias, pair_idx, n_batch=n_batch, n_lat=n_lat, n_ctx=n_ctx, q_col=off_bq // HEAD_DIM,
            k_col=off_bk // HEAD_DIM, v_col=off_bv // HEAD_DIM, n_heads=n_b_heads)
        kb_ctx = p[lat_rows:, off_bk:off_bk + wb].reshape(n_batch, n_ctx, wb)
        vb_ctx = p[lat_rows:, off_bv:off_bv + wb].reshape(n_batch, n_ctx, wb)
        yb_ctx = _flash(p, kb_ctx, vb_ctx, n_batch=n_batch, n_q_rows=n_ctx, q_row0=lat_rows,
                        q_rows_per_batch=n_ctx, q_col=off_bq // HEAD_DIM, n_kv_heads=n_b_heads,
                        group=1, kv_row0=0, n_kv_rows=n_ctx)
        qc, kc, vc = _qk_prep(p, cos_t, sin_t, qk_norm_g[l, 0], qk_norm_g[l, 1], n_batch=n_batch,
                              n_lat=n_lat, n_ctx=n_ctx, q_off=off_cq // wc, n_q=n_c_heads,
                              k_off=off_ck // wkv, v_off=off_cv // wkv, n_kv=n_kv_heads)
        yc_lat = _flash_gqa(qc, kc, vc, n_batch=n_batch, n_q_rows=n_lat, q_row0=0,
                            q_rows_per_batch=n_lat, n_kv_heads=n_kv_heads, group=group,
                            kv_row0=0, n_kv_rows=n_lat + n_ctx)
        yc_ctx = _flash_gqa(qc, kc, vc, n_batch=n_batch, n_q_rows=n_ctx, q_row0=lat_rows,
                            q_rows_per_batch=n_ctx, n_kv_heads=n_kv_heads, group=group,
                            kv_row0=n_lat, n_kv_rows=n_ctx)
        merged = _merge(ya, yb_lat, yb_ctx, yc_lat, yc_ctx,
                        group_norm_g[l, wa:wa + wb], group_norm_g[l, wa + wb:])
        xa = _matmul_residual(merged, w_out_b, l, xa, mods, 2, tn=1024, **dims)
        j = l // 2
        if l % 2 == 0:
            h2 = _norm_mod(xa, norm2_g[l], mods, 3, 4, out_dtype=BF16, **dims)
            hid = _matmul_swiglu_w32(h2, ffn_w1, ffn_w3, j)
            xa = _matmul_residual(hid, ffn_w2_b, j, xa, mods, 5, **dims)
        else:
            h2 = _norm_mod(xa, norm2_g[l], mods, 3, 4, out_dtype=F32, **dims)
            routes = _router(h2, moe_router[j], moe_router_b[j])
            src, dest, tile_expert, tile_valid = _route_plan(routes, n_experts, MOE_TILE)
            xs = _gather_rows(h2, src, out_dtype=BF16)
            ys = _moe_experts(xs, moe_w1, moe_w3, moe_w2, j, tile_expert, tile_valid, tm=MOE_TILE)
            xa = _moe_combine(xa, ys, routes, dest, mods, 5, **dims)

    out = _final_norm(xa, final_norm_g, lat_rows)
    return out.reshape(n_batch, n_lat, d)
```

```python
import functools
import math

import numpy as np
import jax
import jax.numpy as jnp
from jax import lax
from jax.experimental import pallas as pl
from jax.experimental.pallas import tpu as pltpu

F32 = jnp.float32
BF16 = jnp.bfloat16

HEAD_DIM = 128
GRID_W = 64
CHUNK = 128
NA_ROWS = 8
NA_COLS = 16
ROPE_THETA = 10000.0
EPS = 1e-6
N_MOD = 6
MASK_VALUE = -1e30
ROUTER_LANES = 128
MOE_TILE = 512
FLASH_LOOKAHEAD = 3
ROW_TILES = (1536, 1024, 512, 256)
NA_Q_ROWS = 8
NA_BAND_ROWS = 16
NA_HEADS_PER_STEP = 4
LOG2E = 1.4426950408889634
QK_PRESCALE = HEAD_DIM ** -0.5 * LOG2E
V7X_VMEM_BYTES = 64 * 1024 * 1024
VMEM_LIMIT = V7X_VMEM_BYTES - 8 * 1024 * 1024


def _params(*sem):
    return pltpu.CompilerParams(dimension_semantics=sem, vmem_limit_bytes=VMEM_LIMIT)


def _rms(x, g):
    return x * lax.rsqrt(jnp.mean(x * x, axis=-1, keepdims=True) + EPS) * g


def _ada_down_kernel(c_ref, w_ref, o_ref):
    c = c_ref[...]
    a = (c * jax.nn.sigmoid(c)).astype(BF16)
    o_ref[0] = jnp.dot(a, w_ref[0].astype(BF16), preferred_element_type=F32)


def _ada_up_kernel(t_ref, w_ref, b_ref, o_ref):
    t = t_ref[0].astype(BF16)
    o_ref[0] = jnp.dot(t, w_ref[0].astype(BF16), preferred_element_type=F32) + b_ref[0]


def _ada_mods(cond, w_down, w_up, b_up):
    depth, d, r = w_down.shape
    n_out = w_up.shape[2]
    rows = cond.shape[0]
    tn1 = min(r, 512)
    t = pl.pallas_call(
        _ada_down_kernel,
        grid=(depth, r // tn1),
        in_specs=[pl.BlockSpec((rows, d), lambda l, j: (0, 0)),
                  pl.BlockSpec((1, d, tn1), lambda l, j: (l, 0, j))],
        out_specs=pl.BlockSpec((1, rows, tn1), lambda l, j: (l, 0, j)),
        out_shape=jax.ShapeDtypeStruct((depth, rows, r), F32),
        compiler_params=_params("parallel", "parallel"),
        name="ada_down",
    )(cond, w_down)
    tn2 = min(n_out, 2048)
    return pl.pallas_call(
        _ada_up_kernel,
        grid=(depth, n_out // tn2),
        in_specs=[pl.BlockSpec((1, rows, r), lambda l, j: (l, 0, 0)),
                  pl.BlockSpec((1, r, tn2), lambda l, j: (l, 0, j)),
                  pl.BlockSpec((1, 1, tn2), lambda l, j: (l, 0, j))],
        out_specs=pl.BlockSpec((1, rows, tn2), lambda l, j: (l, 0, j)),
        out_shape=jax.ShapeDtypeStruct((depth, rows, n_out), F32),
        compiler_params=_params("parallel", "parallel"),
        name="ada_up",
    )(t, w_up, b_up.reshape(depth, 1, n_out))


def _norm_mod_kernel(x_ref, g_ref, sh_ref, sc_ref, o_ref):
    y = _rms(x_ref[...], g_ref[...])
    o_ref[...] = (y * (1.0 + sc_ref[0]) + sh_ref[0]).astype(o_ref.dtype)


def _norm_kernel(x_ref, g_ref, o_ref):
    o_ref[...] = _rms(x_ref[...], g_ref[...]).astype(o_ref.dtype)


def _seg_fn(tm, n_lat, n_batch):
    return lambda i: jnp.minimum((i * tm) // n_lat, n_batch)


def _norm_mod(x, g, mods, k_shift, k_scale, *, n_lat, n_batch, out_dtype, tm=256):
    t, d = x.shape
    seg = _seg_fn(tm, n_lat, n_batch)
    return pl.pallas_call(
        _norm_mod_kernel,
        grid=(t // tm,),
        in_specs=[pl.BlockSpec((tm, d), lambda i: (i, 0)),
                  pl.BlockSpec((1, d), lambda i: (0, 0)),
                  pl.BlockSpec((1, 1, d), lambda i: (seg(i) * N_MOD + k_shift, 0, 0)),
                  pl.BlockSpec((1, 1, d), lambda i: (seg(i) * N_MOD + k_scale, 0, 0))],
        out_specs=pl.BlockSpec((tm, d), lambda i: (i, 0)),
        out_shape=jax.ShapeDtypeStruct((t, d), out_dtype),
        compiler_params=_params("parallel"),
        name="norm_mod",
    )(x, g.reshape(1, d), mods, mods)


def _final_norm(x, g, rows, tm=256):
    d = x.shape[1]
    return pl.pallas_call(
        _norm_kernel,
        grid=(rows // tm,),
        in_specs=[pl.BlockSpec((tm, d), lambda i: (i, 0)),
                  pl.BlockSpec((1, d), lambda i: (0, 0))],
        out_specs=pl.BlockSpec((tm, d), lambda i: (i, 0)),
        out_shape=jax.ShapeDtypeStruct((rows, d), F32),
        compiler_params=_params("parallel"),
        name="final_norm",
    )(x, g.reshape(1, d))


def _mmw_kernel(a_ref, w_ref, cs_ref, o_ref, wb_sc):
    @pl.when(pl.program_id(1) == 0)
    def _():
        wb_sc[...] = (w_ref[...] * cs_ref[...]).astype(BF16)

    o_ref[...] = jnp.dot(a_ref[...], wb_sc[...], preferred_element_type=F32).astype(o_ref.dtype)


def _mmw_swiglu_kernel(a_ref, w1_ref, w3_ref, o_ref, w1_sc, w3_sc):
    @pl.when(pl.program_id(1) == 0)
    def _():
        w1_sc[...] = w1_ref[...].astype(BF16)
        w3_sc[...] = w3_ref[...].astype(BF16)

    a = a_ref[...]
    h1 = jnp.dot(a, w1_sc[...], preferred_element_type=F32)
    h3 = jnp.dot(a, w3_sc[...], preferred_element_type=F32)
    o_ref[...] = (h1 * jax.nn.sigmoid(h1) * h3).astype(o_ref.dtype)


def _mm_res_kernel(a_ref, b_ref, x_ref, g_ref, o_ref):
    acc = jnp.dot(a_ref[...], b_ref[...], preferred_element_type=F32)
    o_ref[...] = x_ref[...] + g_ref[0] * acc


def _pick_tile(n, candidates):
    for c in candidates:
        if n % c == 0:
            return c
    raise ValueError(f"no tile for {n}")


def _matmul_w32(a, w, layer, col_scale, col_block, *, out_dtype, tn):
    m, k = a.shape
    n = w.shape[2]
    tm = _pick_tile(m, ROW_TILES)
    return pl.pallas_call(
        _mmw_kernel,
        grid=(n // tn, m // tm),
        in_specs=[pl.BlockSpec((tm, k), lambda j, i: (i, 0)),
                  pl.BlockSpec((None, k, tn), lambda j, i: (layer, 0, col_block(j))),
                  pl.BlockSpec((1, tn), lambda j, i: (0, col_block(j)))],
        out_specs=pl.BlockSpec((tm, tn), lambda j, i: (i, j)),
        out_shape=jax.ShapeDtypeStruct((m, n), out_dtype),
        scratch_shapes=[pltpu.VMEM((k, tn), BF16)],
        compiler_params=_params("parallel", "arbitrary"),
        name="matmul_w32",
    )(a, w, col_scale)


def _matmul_swiglu_w32(a, w1, w3, layer, *, tn=256):
    m, k = a.shape
    n = w1.shape[2]
    tn = min(tn, n)
    tm = _pick_tile(m, ROW_TILES)
    return pl.pallas_call(
        _mmw_swiglu_kernel,
        grid=(n // tn, m // tm),
        in_specs=[pl.BlockSpec((tm, k), lambda j, i: (i, 0)),
                  pl.BlockSpec((None, k, tn), lambda j, i: (layer, 0, j)),
                  pl.BlockSpec((None, k, tn), lambda j, i: (layer, 0, j))],
        out_specs=pl.BlockSpec((tm, tn), lambda j, i: (i, j)),
        out_shape=jax.ShapeDtypeStruct((m, n), BF16),
        scratch_shapes=[pltpu.VMEM((k, tn), BF16), pltpu.VMEM((k, tn), BF16)],
        compiler_params=_params("parallel", "arbitrary"),
        name="matmul_swiglu_w32",
    )(a, w1, w3)


def _mm_res_wide_kernel(a_ref, b_ref, x_ref, glo_ref, ghi_ref, o_ref, *, tm, n_lat, n_batch):
    acc = jnp.dot(a_ref[...], b_ref[...], preferred_element_type=F32)
    row0 = pl.program_id(0) * tm
    seg_hi = jnp.minimum((row0 + tm - 1) // n_lat, n_batch)
    boundary = seg_hi * n_lat - row0
    rows = lax.broadcasted_iota(jnp.int32, (tm, 1), 0)
    gate = jnp.where(rows < boundary, glo_ref[0], ghi_ref[0])
    o_ref[...] = x_ref[...] + gate * acc


def _matmul_residual_wide(a, w, layer, x, mods, k_gate, *, n_lat, n_batch, tn=512):
    m, k = a.shape
    n = w.shape[2]
    tn = min(tn, n)
    tm = _pick_tile(m, ROW_TILES)
    assert tm <= n_lat

    def seg_lo(i):
        return jnp.minimum((i * tm) // n_lat, n_batch)

    def seg_hi(i):
        return jnp.minimum((i * tm + tm - 1) // n_lat, n_batch)

    return pl.pallas_call(
        functools.partial(_mm_res_wide_kernel, tm=tm, n_lat=n_lat, n_batch=n_batch),
        grid=(m // tm, n // tn),
        in_specs=[pl.BlockSpec((tm, k), lambda i, j: (i, 0)),
                  pl.BlockSpec((None, k, tn), lambda i, j: (layer, 0, j)),
                  pl.BlockSpec((tm, tn), lambda i, j: (i, j)),
                  pl.BlockSpec((1, 1, tn), lambda i, j: (seg_lo(i) * N_MOD + k_gate, 0, j)),
                  pl.BlockSpec((1, 1, tn), lambda i, j: (seg_hi(i) * N_MOD + k_gate, 0, j))],
        out_specs=pl.BlockSpec((tm, tn), lambda i, j: (i, j)),
        out_shape=jax.ShapeDtypeStruct((m, n), F32),
        compiler_params=_params("parallel", "arbitrary"),
        name="matmul_residual_wide",
    )(a, w, x, mods, mods)


def _matmul_residual(a, w, layer, x, mods, k_gate, *, n_lat, n_batch, tm=512, tn=512):
    m, k = a.shape
    n = w.shape[2]
    tn = min(tn, n)
    seg = _seg_fn(tm, n_lat, n_batch)
    return pl.pallas_call(
        _mm_res_kernel,
        grid=(m // tm, n // tn),
        in_specs=[pl.BlockSpec((tm, k), lambda i, j: (i, 0)),
                  pl.BlockSpec((None, k, tn), lambda i, j: (layer, 0, j)),
                  pl.BlockSpec((tm, tn), lambda i, j: (i, j)),
                  pl.BlockSpec((1, 1, tn), lambda i, j: (seg(i) * N_MOD + k_gate, 0, j))],
        out_specs=pl.BlockSpec((tm, tn), lambda i, j: (i, j)),
        out_shape=jax.ShapeDtypeStruct((m, n), F32),
        compiler_params=_params("parallel", "arbitrary"),
        name="matmul_residual",
    )(a, w, x, mods)


def _sgu_kernel(u_ref, v_ref, gv_ref, ws_ref, bs_ref, ga_ref, o_ref, *, n_chunks, n_groups):
    for c in range(n_chunks):
        rows = slice(c * CHUNK, (c + 1) * CHUNK)
        u = jax.nn.gelu(u_ref[rows, :].astype(F32), approximate=True)
        v = jax.nn.gelu(v_ref[rows, :].astype(F32), approximate=True)
        vb = _rms(v, gv_ref[...]).astype(BF16)
        parts = [jnp.dot(ws_ref[g], vb[:, g * HEAD_DIM:(g + 1) * HEAD_DIM],
                         preferred_element_type=F32) for g in range(n_groups)]
        y = u * (jnp.concatenate(parts, axis=1) + bs_ref[...])
        o_ref[rows, :] = _rms(y, ga_ref[...]).astype(o_ref.dtype)


def _sgu(p, g_v, w_s, b_s, g_a, *, width, ta=256):
    t = p.shape[0]
    n_groups = width // HEAD_DIM
    bias =jnp.repeat(b_s.T, HEAD_DIM, axis=1)
    return pl.pallas_call(
        functools.partial(_sgu_kernel, n_chunks=ta // CHUNK, n_groups=n_groups),
        grid=(t // ta,),
        in_specs=[pl.BlockSpec((ta, width), lambda i: (i, 0)),
                  pl.BlockSpec((ta, width), lambda i: (i, 1)),
                  pl.BlockSpec((1, width), lambda i: (0, 0)),
                  pl.BlockSpec((n_groups, CHUNK, CHUNK), lambda i: (0, 0, 0)),
                  pl.BlockSpec((CHUNK, width), lambda i: (0, 0)),
                  pl.BlockSpec((1, width), lambda i: (0, 0))],
        out_specs=pl.BlockSpec((ta, width), lambda i: (i, 0)),
        out_shape=jax.ShapeDtypeStruct((t, width), BF16),
        compiler_params=_params("parallel"),
        name="sgu",
    )(p, p, g_v.reshape(1, width), w_s.astype(BF16), bias, g_a.reshape(1, width))


def _na_bias_tables(rpb, rows):
    n_blocks = rows // NA_Q_ROWS
    cols = np.arange(GRID_W)
    col_start = np.clip(cols - NA_COLS // 2, 0, GRID_W - NA_COLS)
    dc = cols[None, :] - cols[:, None]
    col_ok = (cols[None, :] >= col_start[:, None]) & (cols[None, :] < col_start[:, None] + NA_COLS)
    col_idx = np.where(col_ok, dc + NA_COLS - 1, 0)
    n_dr = 2 * NA_ROWS - 1
    t = jnp.where(col_ok[None, None], rpb[:, :, col_idx] * LOG2E, MASK_VALUE)
    t = jnp.concatenate([t, jnp.full((rpb.shape[0], 1, GRID_W, GRID_W), MASK_VALUE, F32)], axis=1)
    slab = np.full((3, NA_Q_ROWS, NA_BAND_ROWS), n_dr, np.int32)
    rep = [0, min(1, n_blocks - 1), n_blocks - 1]
    for ty, blk in enumerate(rep):
        band0 = int(np.clip(blk * NA_Q_ROWS - NA_ROWS // 2, 0, rows - NA_BAND_ROWS))
        for rq in range(NA_Q_ROWS):
            r = blk * NA_Q_ROWS + rq
            start = int(np.clip(r - NA_ROWS // 2, 0, rows - NA_ROWS))
            for kr in range(NA_BAND_ROWS):
                ar = band0 + kr
                if start <= ar < start + NA_ROWS:
                    slab[ty, rq, kr] = ar - r + NA_ROWS - 1
    pairs = slab.reshape(3, NA_Q_ROWS, NA_BAND_ROWS // 2, 2)
    uniq, inverse = np.unique(pairs.reshape(-1, 2), axis=0, return_inverse=True)
    t2 = jnp.concatenate([t[:, uniq[:, 0]], t[:, uniq[:, 1]]], axis=-1)
    return t2, jnp.asarray(inverse.reshape(-1), jnp.int32)


def _na_kernel(pair_ref, q_ref, k_ref, v_ref, kc_ref, vc_ref, bias_ref, o_ref, *, rows, rc, n_blocks,
               heads):
    i = pl.program_id(2)
    block_type = jnp.where(i == 0, 0, jnp.where(i == n_blocks - 1, 2, 1))
    pairs_per_row = NA_BAND_ROWS // 2

    def bias_rows(hh, rq):
        base = (block_type * NA_Q_ROWS + rq) * pairs_per_row
        return jnp.concatenate([bias_ref[hh, pair_ref[base + m]] for m in range(pairs_per_row)], axis=1)

    band0 = jnp.clip(i * NA_Q_ROWS - NA_ROWS // 2, 0, rows - NA_BAND_ROWS)
    start = pl.multiple_of(band0 * GRID_W, GRID_W)
    nb = NA_BAND_ROWS * GRID_W
    dn = (((1,), (1,)), ((), ()))
    n_chunks = q_ref.shape[0] // rc
    n_loc = nb // HEAD_DIM
    units = [(hh, c) for hh in range(heads) for c in range(n_chunks)]

    def lanes(hh):
        return slice(hh * HEAD_DIM, (hh + 1) * HEAD_DIM)

    def scores(unit):
        hh, c = unit
        q = q_ref[c * rc:(c + 1) * rc, lanes(hh)]
        return (lax.dot_general(q, k_ref[pl.ds(start, nb), lanes(hh)], dn, preferred_element_type=F32),
                lax.dot_general(q, kc_ref[:, lanes(hh)], dn, preferred_element_type=F32))

    nxt = scores(units[0])
    for u, (hh, c) in enumerate(units):
        s_loc, s_ctx = nxt
        if u + 1 < len(units):
            nxt = scores(units[u + 1])
        vb = v_ref[pl.ds(start, nb), lanes(hh)]
        vc = vc_ref[:, lanes(hh)]
        q_rows = rc // GRID_W
        s_loc = s_loc + jnp.concatenate([bias_rows(hh, c * q_rows + r) for r in range(q_rows)], axis=0)
        cols = ([s_loc[:, j * HEAD_DIM:(j + 1) * HEAD_DIM] for j in range(n_loc)]
                + [s_ctx[:, j * HEAD_DIM:(j + 1) * HEAD_DIM] for j in range(s_ctx.shape[1] // HEAD_DIM)])
        m = jnp.max(functools.reduce(jnp.maximum, cols), axis=1, keepdims=True)
        ps = [jnp.exp2(x - m) for x in cols]
        l = jnp.sum(functools.reduce(jnp.add, ps), axis=1, keepdims=True)
        p_loc = jnp.concatenate([x.astype(BF16) for x in ps[:n_loc]], axis=1)
        p_ctx = jnp.concatenate([x.astype(BF16) for x in ps[n_loc:]], axis=1)
        o = (jnp.dot(p_loc, vb, preferred_element_type=F32)
             + jnp.dot(p_ctx, vc, preferred_element_type=F32))
        o_ref[c * rc:(c + 1) * rc, lanes(hh)] = (o / l).astype(o_ref.dtype)


def _neighbourhood_attention(p, bias, pair_idx, *, n_batch, n_lat, n_ctx, q_col, k_col, v_col, n_heads):
    rows = n_lat // GRID_W
    tq = NA_Q_ROWS * GRID_W
    n_blocks = n_lat // tq
    ctx_blk0 = (n_batch * n_lat) // n_ctx
    n_pairs = bias.shape[1]
    hp = math.gcd(NA_HEADS_PER_STEP, n_heads, q_col, k_col, v_col)
    w = hp * HEAD_DIM
    qc, kc, vc = q_col // hp, k_col // hp, v_col // hp
    return pl.pallas_call(
        functools.partial(_na_kernel, rows=rows, rc=tq, n_blocks=n_blocks, heads=hp),
        grid_spec=pltpu.PrefetchScalarGridSpec(
            num_scalar_prefetch=1, grid=(n_heads // hp, n_batch, n_blocks),
            in_specs=[pl.BlockSpec((tq, w), lambda h, b, i, pr: (b * n_blocks + i, qc + h)),
                      pl.BlockSpec((n_lat, w), lambda h, b, i, pr: (b, kc + h)),
                      pl.BlockSpec((n_lat, w), lambda h, b, i, pr: (b, vc + h)),
                      pl.BlockSpec((n_ctx, w), lambda h, b, i, pr: (ctx_blk0 + b, kc + h)),
                      pl.BlockSpec((n_ctx, w), lambda h, b, i, pr: (ctx_blk0 + b, vc + h)),
                      pl.BlockSpec((hp, n_pairs, GRID_W, 2 * GRID_W), lambda h, b, i, pr: (h, 0, 0, 0))],
            out_specs=pl.BlockSpec((tq, w), lambda h, b, i, pr: (b * n_blocks + i, h))),
        out_shape=jax.ShapeDtypeStruct((n_batch * n_lat, n_heads * HEAD_DIM), BF16),
        compiler_params=_params("parallel", "parallel", "arbitrary"),
        name="neighbourhood_attention",
    )(pair_idx, p, p, p, p, p, bias)


def _rope_tables(n_lat, n_ctx):
    t = jnp.arange(n_lat, dtype=jnp.int32)
    pos = jnp.stack([t // GRID_W, t % GRID_W], axis=-1).astype(F32)
    n_freq = HEAD_DIM // 4
    inv = 1.0 / (ROPE_THETA ** (jnp.arange(n_freq, dtype=F32) / n_freq))
    ang = pos[:, :, None] * inv
    cos, sin = jnp.cos(ang), jnp.sin(ang)
    cos_t = jnp.concatenate([cos[:, 0], cos[:, 0], cos[:, 1], cos[:, 1]], axis=-1)
    sin_t = jnp.concatenate([-sin[:, 0], sin[:, 0], -sin[:, 1], sin[:, 1]], axis=-1)
    cos_t = jnp.concatenate([cos_t, jnp.ones((n_ctx, HEAD_DIM), F32)], axis=0)
    sin_t = jnp.concatenate([sin_t, jnp.zeros((n_ctx, HEAD_DIM), F32)], axis=0)
    return cos_t, sin_t


def _qk_prep_kernel(q_ref, k_ref, v_ref, cos_ref, sin_ref, gq_ref, gk_ref,
                    qo_ref, ko_ref, vo_ref, *, n_q, n_kv):
    cos = cos_ref[...]
    sin = sin_ref[...]
    quarter = HEAD_DIM // 4
    r_idx = lax.broadcasted_iota(jnp.int32, (HEAD_DIM, HEAD_DIM), 0)
    c_idx = lax.broadcasted_iota(jnp.int32, (HEAD_DIM, HEAD_DIM), 1)
    exchange = jnp.where(r_idx == jnp.bitwise_xor(c_idx, quarter), 1.0, 0.0).astype(F32)

    def prep(x, g, out_scale):
        y = _rms(x.astype(F32), g)
        swapped = jnp.dot(y, exchange, preferred_element_type=F32, precision=lax.Precision.HIGHEST)
        return ((y * cos + swapped * sin) * out_scale).astype(BF16)

    for h in range(n_q):
        cs = slice(h * HEAD_DIM, (h + 1) * HEAD_DIM)
        qo_ref[:, cs] = prep(q_ref[:, cs], gq_ref[...], QK_PRESCALE)
    ones = jnp.ones((v_ref.shape[0], HEAD_DIM), BF16)
    for h in range(n_kv):
        cs = slice(h * HEAD_DIM, (h + 1) * HEAD_DIM)
        ko_ref[0, :, cs] = prep(k_ref[:, cs], gk_ref[...], 1.0)
        vo_ref[0, :, 2 * h * HEAD_DIM:(2 * h + 1) * HEAD_DIM] = v_ref[:, cs]
        vo_ref[0, :, (2 * h + 1) * HEAD_DIM:(2 * h + 2) * HEAD_DIM] = ones


def _qk_prep(p, cos_t, sin_t, g_q, g_k, *, n_batch, n_lat, n_ctx, q_off, n_q, k_off, v_off, n_kv):
    t = p.shape[0]
    tm = n_ctx
    lat_tiles = n_lat // tm
    n_lat_tiles = n_batch * lat_tiles

    def pos_blk(i):
        return jnp.where(i < n_lat_tiles, i % lat_tiles, lat_tiles)

    def kv_blk(i):
        c = i - n_lat_tiles
        return (jnp.where(i < n_lat_tiles, i // lat_tiles, c),
                jnp.where(i < n_lat_tiles, i % lat_tiles, lat_tiles), 0)

    wq, wkv = n_q * HEAD_DIM, n_kv * HEAD_DIM
    return pl.pallas_call(
        functools.partial(_qk_prep_kernel, n_q=n_q, n_kv=n_kv),
        grid=(t // tm,),
        in_specs=[pl.BlockSpec((tm, wq), lambda i: (i, q_off)),
                  pl.BlockSpec((tm, wkv), lambda i: (i, k_off)),
                  pl.BlockSpec((tm, wkv), lambda i: (i, v_off)),
                  pl.BlockSpec((tm, HEAD_DIM), lambda i: (pos_blk(i), 0)),
                  pl.BlockSpec((tm, HEAD_DIM), lambda i: (pos_blk(i), 0)),
                  pl.BlockSpec((1, HEAD_DIM), lambda i: (0, 0)),
                  pl.BlockSpec((1, HEAD_DIM), lambda i: (0, 0))],
        out_specs=[pl.BlockSpec((tm, wq), lambda i: (i, 0)),
                   pl.BlockSpec((1, tm, wkv), kv_blk),
                   pl.BlockSpec((1, tm, 2 * wkv), kv_blk)],
        out_shape=[jax.ShapeDtypeStruct((t, wq), BF16),
                   jax.ShapeDtypeStruct((n_batch, n_lat + n_ctx, wkv), BF16),
                   jax.ShapeDtypeStruct((n_batch, n_lat + n_ctx, 2 * wkv), BF16)],
        compiler_params=_params("parallel"),
        name="qk_prep",
    )(p, p, p, cos_t, sin_t, g_q.reshape(1, HEAD_DIM), g_k.reshape(1, HEAD_DIM))


def _flash_kernel(q_ref, k_ref, v_ref, o_ref, m_sc, l_sc, acc_sc, *, group, tq):
    kj = pl.program_id(3)

    @pl.when(kj == 0)
    def _():
        m_sc[...] = jnp.full_like(m_sc, -jnp.inf)
        l_sc[...] = jnp.zeros_like(l_sc)
        acc_sc[...] = jnp.zeros_like(acc_sc)

    if group == 1:
        q = q_ref[...]
    else:
        q = jnp.concatenate([q_ref[:, g * HEAD_DIM:(g + 1) * HEAD_DIM] for g in range(group)], axis=0)
    s = lax.dot_general(q, k_ref[0], (((1,), (1,)), ((), ())), preferred_element_type=F32)
    m_prev = m_sc[...]
    m_new = jnp.maximum(m_prev, jnp.max(s, axis=1, keepdims=True))
    alpha = jnp.exp2(m_prev - m_new)
    p = jnp.exp2(s - m_new)
    l_sc[...] = alpha * l_sc[...] + jnp.sum(p, axis=1, keepdims=True)
    acc_sc[...] = alpha * acc_sc[...] + jnp.dot(p.astype(BF16), v_ref[0], preferred_element_type=F32)
    m_sc[...] = m_new

    @pl.when(kj == pl.num_programs(3) - 1)
    def _():
        o = acc_sc[...] / l_sc[...]
        for g in range(group):
            o_ref[:, g * HEAD_DIM:(g + 1) * HEAD_DIM] = o[g * tq:(g + 1) * tq].astype(o_ref.dtype)


def _flash(q2d, k3d, v3d, *, n_batch, n_q_rows, q_row0, q_rows_per_batch, q_col, n_kv_heads,
           group, kv_row0, n_kv_rows, tq=256):
    tq = min(tq, n_q_rows)
    tk = _pick_tile(n_kv_rows, (1024, 768, 512, 256))
    qb0, qbb = q_row0 // tq, q_rows_per_batch // tq
    kb0 = kv_row0 // tk
    gw = group * HEAD_DIM
    return pl.pallas_call(
        functools.partial(_flash_kernel, group=group, tq=tq),
        grid=(n_batch, n_kv_heads, n_q_rows // tq, n_kv_rows // tk),
        in_specs=[pl.BlockSpec((tq, gw), lambda b, h, i, j: (qb0 + b * qbb + i, q_col + h)),
                  pl.BlockSpec((1, tk, HEAD_DIM), lambda b, h, i, j: (b, kb0 + j, h)),
                  pl.BlockSpec((1, tk, HEAD_DIM), lambda b, h, i, j: (b, kb0 + j, h))],
        out_specs=pl.BlockSpec((tq, gw), lambda b, h, i, j: (b * (n_q_rows // tq) + i, h)),
        out_shape=jax.ShapeDtypeStruct((n_batch * n_q_rows, n_kv_heads * gw), BF16),
        scratch_shapes=[pltpu.VMEM((group * tq, 1), F32),
                        pltpu.VMEM((group * tq, 1), F32),
                        pltpu.VMEM((group * tq, HEAD_DIM), F32)],
        compiler_params=_params("parallel", "parallel", "parallel", "arbitrary"),
        name="flash_attention",
    )(q2d, k3d, v3d)


def _flash_gqa_kernel(q_ref, k_ref, v_ref, o_ref, m_sc, acc_sc, *, group, tq, tk, rc):
    kj = pl.program_id(3)

    @pl.when(kj == 0)
    def _():
        m_sc[...] = jnp.full_like(m_sc, -jnp.inf)
        acc_sc[...] = jnp.zeros_like(acc_sc)

    n_cols = tk // HEAD_DIM
    k = k_ref[0]
    v = v_ref[0]
    chunks = [(g, r0) for g in range(group) for r0 in range(0, tq, rc)]

    def scores(chunk):
        g, r0 = chunk
        q = q_ref[r0:r0 + rc, g * HEAD_DIM:(g + 1) * HEAD_DIM]
        return lax.dot_general(q, k, (((1,), (1,)), ((), ())), preferred_element_type=F32)

    pending = [scores(ch) for ch in chunks[:FLASH_LOOKAHEAD]]
    for ci, (g, r0) in enumerate(chunks):
        if ci + FLASH_LOOKAHEAD < len(chunks):
            pending.append(scores(chunks[ci + FLASH_LOOKAHEAD]))
        s = pending.pop(0)
        rows = slice(g * tq + r0, g * tq + r0 + rc)
        cols = [s[:, i * HEAD_DIM:(i + 1) * HEAD_DIM] for i in range(n_cols)]
        col_max = functools.reduce(jnp.maximum, cols)
        m_prev = m_sc[rows, :]
        m_new = jnp.maximum(m_prev, jnp.max(col_max, axis=1, keepdims=True))
        alpha = jnp.exp2(m_prev - m_new)
        p = jnp.concatenate([jnp.exp2(c - m_new).astype(BF16) for c in cols], axis=1)
        pv = jnp.dot(p, v, preferred_element_type=F32)
        acc_sc[rows, :] = jnp.concatenate([alpha, alpha], axis=1) * acc_sc[rows, :] + pv
        m_sc[rows, :] = m_new

    @pl.when(kj == pl.num_programs(3) - 1)
    def _():
        o = acc_sc[:, :HEAD_DIM] / acc_sc[:, HEAD_DIM:]
        for g in range(group):
            o_ref[:, g * HEAD_DIM:(g + 1) * HEAD_DIM] = o[g * tq:(g + 1) * tq].astype(o_ref.dtype)


def _flash_gqa(q2d, k3d, v3d, *, n_batch, n_q_rows, q_row0, q_rows_per_batch, n_kv_heads, group,
               kv_row0, n_kv_rows, tq=512, rc=512):
    tq = min(tq, n_q_rows)
    rc = min(rc, tq)
    tk = _pick_tile(n_kv_rows, (2816, 1408, 1024, 768, 512, 256))
    qb0, qbb = q_row0 // tq, q_rows_per_batch // tq
    kb0 = kv_row0 // tk
    gw = group * HEAD_DIM
    return pl.pallas_call(
        functools.partial(_flash_gqa_kernel, group=group, tq=tq, tk=tk, rc=rc),
        grid=(n_batch, n_kv_heads, n_q_rows // tq, n_kv_rows // tk),
        in_specs=[pl.BlockSpec((tq, gw), lambda b, h, i, j: (qb0 + b * qbb + i, h)),
                  pl.BlockSpec((1, tk, HEAD_DIM), lambda b, h, i, j: (b, kb0 + j, h)),
                  pl.BlockSpec((1, tk, 2 * HEAD_DIM), lambda b, h, i, j: (b, kb0 + j, h))],
        out_specs=pl.BlockSpec((tq, gw), lambda b, h, i, j: (b * (n_q_rows // tq) + i, h)),
        out_shape=jax.ShapeDtypeStruct((n_batch * n_q_rows, n_kv_heads * gw), BF16),
        scratch_shapes=[pltpu.VMEM((group * tq, HEAD_DIM), F32),
                        pltpu.VMEM((group * tq, 2 * HEAD_DIM), F32)],
        compiler_params=_params("parallel", "parallel", "parallel", "arbitrary"),
        name="flash_gqa",
    )(q2d, k3d, v3d)


def _merge_kernel(ya_ref, yb_lat_ref, yb_ctx_ref, yc_lat_ref, yc_ctx_ref, gb_ref, gc_ref, o_ref,
                  *, wa, wb, n_lat_tiles):
    o_ref[:, :wa] = ya_ref[...]

    def emit(yb_ref, yc_ref):
        o_ref[:, wa:wa + wb] = _rms(yb_ref[...].astype(F32), gb_ref[...]).astype(o_ref.dtype)
        o_ref[:, wa + wb:] = _rms(yc_ref[...].astype(F32), gc_ref[...]).astype(o_ref.dtype)

    is_lat = pl.program_id(0) < n_lat_tiles
    pl.when(is_lat)(lambda: emit(yb_lat_ref, yc_lat_ref))
    pl.when(jnp.logical_not(is_lat))(lambda: emit(yb_ctx_ref, yc_ctx_ref))


def _merge(ya, yb_lat, yb_ctx, yc_lat, yc_ctx, g_b, g_c, tm=256):
    t, wa = ya.shape
    wb, wc = yb_lat.shape[1], yc_lat.shape[1]
    n_lat_tiles = yb_lat.shape[0] // tm

    def lat(i):
        return (jnp.minimum(i, n_lat_tiles - 1), 0)

    def ctx(i):
        return (jnp.maximum(i - n_lat_tiles, 0), 0)

    return pl.pallas_call(
        functools.partial(_merge_kernel, wa=wa, wb=wb, n_lat_tiles=n_lat_tiles),
        grid=(t // tm,),
        in_specs=[pl.BlockSpec((tm, wa), lambda i: (i, 0)),
                  pl.BlockSpec((tm, wb), lat), pl.BlockSpec((tm, wb), ctx),
                  pl.BlockSpec((tm, wc), lat), pl.BlockSpec((tm, wc), ctx),
                  pl.BlockSpec((1, wb), lambda i: (0, 0)),
                  pl.BlockSpec((1, wc), lambda i: (0, 0))],
        out_specs=pl.BlockSpec((tm, wa + wb + wc), lambda i: (i, 0)),
        out_shape=jax.ShapeDtypeStruct((t, wa + wb + wc), BF16),
        compiler_params=_params("parallel"),
        name="merge_groups",
    )(ya, yb_lat, yb_ctx, yc_lat, yc_ctx, g_b.reshape(1, wb), g_c.reshape(1, wc))


def _router_kernel(h_ref, w_ref, b_ref, o_ref):
    logits = jnp.dot(h_ref[...].astype(BF16), w_ref[...], preferred_element_type=F32) + b_ref[...]
    lane = lax.broadcasted_iota(jnp.int32, logits.shape, 1).astype(F32)
    m1 = jnp.max(logits, axis=1, keepdims=True)
    i1 = jnp.min(jnp.where(logits == m1, lane, float(ROUTER_LANES)), axis=1, keepdims=True)
    rest = jnp.where(lane == i1, -jnp.inf, logits)
    m2 = jnp.max(rest, axis=1, keepdims=True)
    i2 = jnp.min(jnp.where(rest == m2, lane, float(ROUTER_LANES)), axis=1, keepdims=True)
    e2 = jnp.exp(m2 - m1)
    g1 = 1.0 / (1.0 + e2)
    o_ref[...] = jnp.where(lane == 0.0, i1,
                           jnp.where(lane == 1.0, i2,
                                     jnp.where(lane == 2.0, g1,
                                               jnp.where(lane == 3.0, e2 * g1, 0.0))))


def _route_plan(routes, n_experts, tm):
    t = routes.shape[0]
    a = 2 * t
    flat_e = routes[:, :2].astype(jnp.int32).reshape(a)
    onehot = (flat_e[:, None] == jnp.arange(n_experts, dtype=jnp.int32)[None, :]).astype(jnp.int32)
    csum = jnp.cumsum(onehot, axis=0)
    counts = csum[-1]
    rank = jnp.take_along_axis(csum, flat_e[:, None], axis=1)[:, 0] - 1
    padded = ((counts + tm - 1) // tm) * tm
    ends = jnp.cumsum(padded)
    starts = ends - padded
    dest = starts[flat_e] + rank
    ustarts = jnp.cumsum(counts) - counts
    order = jnp.argsort(flat_e, stable=True).astype(jnp.int32)
    n_rows = a + n_experts * tm
    row = jnp.arange(n_rows, dtype=jnp.int32)
    row_e = jnp.minimum(jnp.searchsorted(ends, row, side="right"), n_experts - 1).astype(jnp.int32)
    k = row - starts[row_e]
    src_a = order[jnp.clip(ustarts[row_e] + k, 0, a - 1)]
    src = jnp.where(k < counts[row_e], src_a // 2, 0).astype(jnp.int32)
    tile_start = jnp.arange(n_rows // tm, dtype=jnp.int32) * tm
    tile_valid = (tile_start < ends[-1]).astype(jnp.int32)
    tile_expert = row_e[jnp.minimum(tile_start, ends[-1] - 1)]
    return src, dest.astype(jnp.int32), tile_expert, tile_valid


def _gather_rows_kernel(src_ref, h_hbm, o_ref, buf, sem, *, tg):
    i = pl.program_id(0)

    def copy(tile, j):
        slot = tile % 2
        return pltpu.make_async_copy(h_hbm.at[pl.ds(src_ref[tile * tg + j], 1)],
                                     buf.at[slot, pl.ds(j, 1)], sem.at[slot])

    def issue(tile):
        def body(j, carry):
            copy(tile, j).start()
            return carry
        lax.fori_loop(0, tg, body, 0, unroll=8)

    pl.when(i == 0)(lambda: issue(0))
    pl.when(i + 1 < pl.num_programs(0))(lambda: issue(i + 1))

    slot = i % 2
    pltpu.make_async_copy(h_hbm.at[pl.ds(0, tg)], buf.at[slot], sem.at[slot]).wait()
    o_ref[...] = buf[slot].astype(o_ref.dtype)


def _gather_rows(h, src, *, out_dtype, tg=512):
    n = src.shape[0]
    d = h.shape[1]
    return pl.pallas_call(
        functools.partial(_gather_rows_kernel, tg=tg),
        grid_spec=pltpu.PrefetchScalarGridSpec(
            num_scalar_prefetch=1, grid=(n // tg,),
            in_specs=[pl.BlockSpec(memory_space=pl.ANY)],
            out_specs=pl.BlockSpec((tg, d), lambda i, s: (i, 0)),
            scratch_shapes=[pltpu.VMEM((2, tg, d), h.dtype), pltpu.SemaphoreType.DMA((2,))]),
        out_shape=jax.ShapeDtypeStruct((n, d), out_dtype),
        compiler_params=_params("arbitrary"),
        name="moe_gather",
    )(src, h)


def _expert_changed(te_ref, i):
    return jnp.logical_or(i == 0, te_ref[i] != te_ref[jnp.maximum(i - 1, 0)])


def _moe_up_kernel(te_ref, tv_ref, x_ref, w1_ref, w3_ref, o_ref, w1_sc, w3_sc):
    i = pl.program_id(1)
    valid = tv_ref[i] == 1

    @pl.when(jnp.logical_and(valid, _expert_changed(te_ref, i)))
    def _():
        w1_sc[...] = w1_ref[...].astype(BF16)
        w3_sc[...] = w3_ref[...].astype(BF16)

    @pl.when(valid)
    def _():
        a = x_ref[...]
        h1 = jnp.dot(a, w1_sc[...], preferred_element_type=F32)
        h3 = jnp.dot(a, w3_sc[...], preferred_element_type=F32)
        o_ref[...] = (h1 * jax.nn.sigmoid(h1) * h3).astype(o_ref.dtype)

    @pl.when(jnp.logical_not(valid))
    def _():
        o_ref[...] = jnp.zeros_like(o_ref)


def _moe_down_kernel(te_ref, tv_ref, a_ref, w_ref, o_ref, w_sc):
    i = pl.program_id(1)
    valid = tv_ref[i] == 1

    @pl.when(jnp.logical_and(valid, _expert_changed(te_ref, i)))
    def _():
        w_sc[...] = w_ref[...].astype(BF16)

    @pl.when(valid)
    def _():
        o_ref[...] = jnp.dot(a_ref[...], w_sc[...], preferred_element_type=F32)

    @pl.when(jnp.logical_not(valid))
    def _():
        o_ref[...] = jnp.zeros_like(o_ref)


def _moe_experts(xs, w1, w3, w2, layer, tile_expert, tile_valid, *, tm, tn_up=512, tn_down=2048):
    rows, d = xs.shape
    f = w1.shape[-1]
    tn_up, tn_down = min(tn_up, f), min(tn_down, d)
    n_tiles = rows // tm
    hid = pl.pallas_call(
        _moe_up_kernel,
        grid_spec=pltpu.PrefetchScalarGridSpec(
            num_scalar_prefetch=2, grid=(f // tn_up, n_tiles),
            in_specs=[pl.BlockSpec((tm, d), lambda j, i, te, tv: (i, 0)),
                      pl.BlockSpec((None, None, d, tn_up), lambda j, i, te, tv: (layer, te[i], 0, j)),
                      pl.BlockSpec((None, None, d, tn_up), lambda j, i, te, tv: (layer, te[i], 0, j))],
            out_specs=pl.BlockSpec((tm, tn_up), lambda j, i, te, tv: (i, j)),
            scratch_shapes=[pltpu.VMEM((d, tn_up), BF16), pltpu.VMEM((d, tn_up), BF16)]),
        out_shape=jax.ShapeDtypeStruct((rows, f), BF16),
        compiler_params=_params("parallel", "arbitrary"),
        name="moe_up",
    )(tile_expert, tile_valid, xs, w1, w3)
    return pl.pallas_call(
        _moe_down_kernel,
        grid_spec=pltpu.PrefetchScalarGridSpec(
            num_scalar_prefetch=2, grid=(d // tn_down, n_tiles),
            in_specs=[pl.BlockSpec((tm, f), lambda j, i, te, tv: (i, 0)),
                      pl.BlockSpec((None, None, f, tn_down), lambda j, i, te, tv: (layer, te[i], 0, j))],
            out_specs=pl.BlockSpec((tm, tn_down), lambda j, i, te, tv: (i, j)),
            scratch_shapes=[pltpu.VMEM((f, tn_down), BF16)]),
        out_shape=jax.ShapeDtypeStruct((rows, d), F32),
        compiler_params=_params("parallel", "arbitrary"),
        name="moe_down",
    )(tile_expert, tile_valid, hid, w2)


def _moe_combine_kernel(dest_ref, x_ref, r_ref, g_ref, y_hbm, o_ref, ybuf, sem, *, tc):
    i = pl.program_id(0)

    def copy(tile, j, k):
        slot = tile % 2
        return pltpu.make_async_copy(y_hbm.at[pl.ds(dest_ref[(tile * tc + j) * 2 + k], 1)],
                                     ybuf.at[slot, k, pl.ds(j, 1)], sem.at[slot])

    def issue(tile):
        def body(j, carry):
            copy(tile, j, 0).start()
            copy(tile, j, 1).start()
            return carry
        lax.fori_loop(0, tc, body, 0, unroll=4)

    pl.when(i == 0)(lambda: issue(0))
    pl.when(i + 1 < pl.num_programs(0))(lambda: issue(i + 1))

    slot = i % 2
    for k in range(2):
        pltpu.make_async_copy(y_hbm.at[pl.ds(0, tc)], ybuf.at[slot, k], sem.at[slot]).wait()
    gate = g_ref[0]
    rows = 32

    def row_block(c, carry):
        rs = pl.ds(pl.multiple_of(c * rows, rows), rows)
        r = r_ref[rs, :]
        y = r[:, 2:3] * ybuf[slot, 0, rs, :] + r[:, 3:4] * ybuf[slot, 1, rs, :]
        o_ref[rs, :] = x_ref[rs, :] + gate * y
        return carry

    lax.fori_loop(0, tc // rows, row_block, 0, unroll=2)


def _moe_combine(x, y, routes, dest, mods, k_gate, *, n_lat, n_batch, tc=256):
    t, d = x.shape
    seg = _seg_fn(tc, n_lat, n_batch)
    return pl.pallas_call(
        functools.partial(_moe_combine_kernel, tc=tc),
        grid_spec=pltpu.PrefetchScalarGridSpec(
            num_scalar_prefetch=1, grid=(t // tc,),
            in_specs=[pl.BlockSpec((tc, d), lambda i, ds: (i, 0)),
                      pl.BlockSpec((tc, ROUTER_LANES), lambda i, ds: (i, 0)),
                      pl.BlockSpec((1, 1, d), lambda i, ds: (seg(i) * N_MOD + k_gate, 0, 0)),
                      pl.BlockSpec(memory_space=pl.ANY)],
            out_specs=pl.BlockSpec((tc, d), lambda i, ds: (i, 0)),
            scratch_shapes=[pltpu.VMEM((2, 2, tc, d), F32), pltpu.SemaphoreType.DMA((2,))]),
        out_shape=jax.ShapeDtypeStruct((t, d), F32),
        compiler_params=_params("arbitrary"),
        name="moe_combine",
    )(dest, x, routes, mods, y)


def _router(h, w_r, b_r, tm=512):
    t, d = h.shape
    e = w_r.shape[1]
    w_pad = jnp.zeros((d, ROUTER_LANES), BF16).at[:, :e].set(w_r.astype(BF16))
    b_pad = jnp.full((1, ROUTER_LANES), -jnp.inf, F32).at[0, :e].set(b_r)
    return pl.pallas_call(
        _router_kernel,
        grid=(t // tm,),
        in_specs=[pl.BlockSpec((tm, d), lambda i: (i, 0)),
                  pl.BlockSpec((d, ROUTER_LANES), lambda i: (0, 0)),
                  pl.BlockSpec((1, ROUTER_LANES), lambda i: (0, 0))],
        out_specs=pl.BlockSpec((tm, ROUTER_LANES), lambda i: (i, 0)),
        out_shape=jax.ShapeDtypeStruct((t, ROUTER_LANES), F32),
        compiler_params=_params("parallel"),
        name="router",
    )(h, w_pad, b_pad)


def kernel(x, c, ctx, c_ctx, ada_down, ada_up, ada_bias, norm1_g, norm2_g, w_in, sgu_norm_g, sgu_w,
           sgu_b, na_rpb, qk_norm_g, group_norm_g, w_out, ffn_w1, ffn_w3, ffn_w2, moe_router,
           moe_router_b, moe_w1, moe_w3, moe_w2, final_norm_g):
    n_batch, n_lat, d = x.shape
    n_ctx = ctx.shape[1]
    depth = w_in.shape[0]
    mix = w_out.shape[1]
    wa = wb = mix // 4
    wc = mix - wa - wb
    n_b_heads = wb // HEAD_DIM
    n_c_heads = wc // HEAD_DIM
    n_kv_heads = n_c_heads // 4
    wkv = n_kv_heads * HEAD_DIM
    group = n_c_heads // n_kv_heads
    n_experts = moe_router.shape[-1]
    dims = dict(n_lat=n_lat, n_batch=n_batch)
    lat_rows = n_batch * n_lat

    off_bq, off_bk, off_bv = 2 * wa, 2 * wa + wb, 2 * wa + 2 * wb
    src_cq = 2 * wa + 3 * wb
    off_ck = src_cq
    off_cv = off_ck + wkv
    off_cq = off_cv + wkv
    in_width = off_cq + wc
    tn_in = math.gcd(512, wkv)
    nb_pre, nb_kv, nb_q = src_cq // tn_in, 2 * wkv // tn_in, wc // tn_in

    def in_col_block(j):
        return jnp.where(j < nb_pre, j, jnp.where(j < nb_pre + nb_kv, j + nb_q, j - nb_kv))

    in_col_scale = jnp.ones((1, in_width), F32).at[:, off_bq:off_bk].set(QK_PRESCALE)

    xa = jnp.concatenate([x.reshape(lat_rows, d), ctx.reshape(n_batch * n_ctx, d)], axis=0)
    cond = jnp.zeros((8, d), F32).at[:n_batch].set(c).at[n_batch].set(c_ctx)
    mods_all = _ada_mods(cond, ada_down, ada_up, ada_bias)
    cos_t, sin_t = _rope_tables(n_lat, n_ctx)
    w_out_b, ffn_w2_b = w_out.astype(BF16), ffn_w2.astype(BF16)

    for l in range(depth):
        mods = mods_all[l].reshape(8 * N_MOD, 1, d)
        h = _norm_mod(xa, norm1_g[l], mods, 0, 1, out_dtype=BF16, **dims)
        p = _matmul_w32(h, w_in, l, in_col_scale, in_col_block, out_dtype=BF16, tn=tn_in)
        ya = _sgu(p, sgu_norm_g[l], sgu_w[l], sgu_b[l], group_norm_g[l, :wa], width=wa)
        bias, pair_idx = _na_bias_tables(na_rpb[l], n_lat // GRID_W)
        yb_lat = _neighbourhood_attention(
            p, bias, pair_idx, n_batch=n_batch, n_lat=n_lat, n_ctx=n_ctx, q_col=off_bq // HEAD_DIM,
            k_col=off_bk // HEAD_DIM, v_col=off_bv // HEAD_DIM, n_heads=n_b_heads)
        kb_ctx = p[lat_rows:, off_bk:off_bk + wb].reshape(n_batch, n_ctx, wb)
        vb_ctx = p[lat_rows:, off_bv:off_bv + wb].reshape(n_batch, n_ctx, wb)
        yb_ctx = _flash(p, kb_ctx, vb_ctx, n_batch=n_batch, n_q_rows=n_ctx, q_row0=lat_rows,
                        q_rows_per_batch=n_ctx, q_col=off_bq // HEAD_DIM, n_kv_heads=n_b_heads,
                        group=1, kv_row0=0, n_kv_rows=n_ctx)
        qc, kc, vc = _qk_prep(p, cos_t, sin_t, qk_norm_g[l, 0], qk_norm_g[l, 1], n_batch=n_batch,
                              n_lat=n_lat, n_ctx=n_ctx, q_off=off_cq // wc, n_q=n_c_heads,
                              k_off=off_ck // wkv, v_off=off_cv // wkv, n_kv=n_kv_heads)
        yc_lat = _flash_gqa(qc, kc, vc, n_batch=n_batch, n_q_rows=n_lat, q_row0=0,
                            q_rows_per_batch=n_lat, n_kv_heads=n_kv_heads, group=group,
                            kv_row0=0, n_kv_rows=n_lat + n_ctx)
        yc_ctx = _flash_gqa(qc, kc, vc, n_batch=n_batch, n_q_rows=n_ctx, q_row0=lat_rows,
                            q_rows_per_batch=n_ctx, n_kv_heads=n_kv_heads, group=group,
                            kv_row0=n_lat, n_kv_rows=n_ctx)
        merged = _merge(ya, yb_lat, yb_ctx, yc_lat, yc_ctx,
                        group_norm_g[l, wa:wa + wb], group_norm_g[l, wa + wb:])
        xa = _matmul_residual_wide(merged, w_out_b, l, xa, mods, 2, **dims)
        j = l // 2
        if l % 2 == 0:
            h2 = _norm_mod(xa, norm2_g[l], mods, 3, 4, out_dtype=BF16, **dims)
            hid = _matmul_swiglu_w32(h2, ffn_w1, ffn_w3, j)
            xa = _matmul_residual(hid, ffn_w2_b, j, xa, mods, 5, **dims)
        else:
            h2 = _norm_mod(xa, norm2_g[l], mods, 3, 4, out_dtype=F32, **dims)
            routes = _router(h2, moe_router[j], moe_router_b[j])
            src, dest, tile_expert, tile_valid = _route_plan(routes, n_experts, MOE_TILE)
            xs = _gather_rows(h2, src, out_dtype=BF16)
            ys = _moe_experts(xs, moe_w1, moe_w3, moe_w2, j, tile_expert, tile_valid, tm=MOE_TILE)
            xa = _moe_combine(xa, ys, routes, dest, mods, 5, **dims)

    out = _final_norm(xa, final_norm_g, lat_rows)
    return out.reshape(n_batch, n_lat, d)
```

```python
import functools
import math

import numpy as np
import jax
import jax.numpy as jnp
from jax import lax
from jax.experimental import pallas as pl
from jax.experimental.pallas import tpu as pltpu

F32 = jnp.float32
BF16 = jnp.bfloat16

HEAD_DIM = 128
GRID_W = 64
CHUNK = 128
NA_ROWS = 8
NA_COLS = 16
ROPE_THETA = 10000.0
EPS = 1e-6
N_MOD = 6
MASK_VALUE = -1e30
ROUTER_LANES = 128
MOE_TILE = 512
FLASH_LOOKAHEAD = 3
ROW_TILES = (1536, 1024, 512, 256)
NA_Q_ROWS = 8
NA_BAND_ROWS = 16
NA_HEADS_PER_STEP = 4
LOG2E = 1.4426950408889634
QK_PRESCALE = HEAD_DIM ** -0.5 * LOG2E
V7X_VMEM_BYTES = 64 * 1024 * 1024
VMEM_LIMIT = V7X_VMEM_BYTES - 8 * 1024 * 1024


def _params(*sem):
    return pltpu.CompilerParams(dimension_semantics=sem, vmem_limit_bytes=VMEM_LIMIT)


def _rms(x, g):
    return x * lax.rsqrt(jnp.mean(x * x, axis=-1, keepdims=True) + EPS) * g


def _ada_down_kernel(c_ref, w_ref, o_ref):
    c = c_ref[...]
    a = (c * jax.nn.sigmoid(c)).astype(BF16)
    o_ref[0] = jnp.dot(a, w_ref[0].astype(BF16), preferred_element_type=F32)


def _ada_up_kernel(t_ref, w_ref, b_ref, o_ref):
    t = t_ref[0].astype(BF16)
    o_ref[0] = jnp.dot(t, w_ref[0].astype(BF16), preferred_element_type=F32) + b_ref[0]


def _ada_mods(cond, w_down, w_up, b_up):
    depth, d, r = w_down.shape
    n_out = w_up.shape[2]
    rows = cond.shape[0]
    tn1 = min(r, 512)
    t = pl.pallas_call(
        _ada_down_kernel,
        grid=(depth, r // tn1),
        in_specs=[pl.BlockSpec((rows, d), lambda l, j: (0, 0)),
                  pl.BlockSpec((1, d, tn1), lambda l, j: (l, 0, j))],
        out_specs=pl.BlockSpec((1, rows, tn1), lambda l, j: (l, 0, j)),
        out_shape=jax.ShapeDtypeStruct((depth, rows, r), F32),
        compiler_params=_params("parallel", "parallel"),
        name="ada_down",
    )(cond, w_down)
    tn2 = min(n_out, 2048)
    return pl.pallas_call(
        _ada_up_kernel,
        grid=(depth, n_out // tn2),
        in_specs=[pl.BlockSpec((1, rows, r), lambda l, j: (l, 0, 0)),
                  pl.BlockSpec((1, r, tn2), lambda l, j: (l, 0, j)),
                  pl.BlockSpec((1, 1, tn2), lambda l, j: (l, 0, j))],
        out_specs=pl.BlockSpec((1, rows, tn2), lambda l, j: (l, 0, j)),
        out_shape=jax.ShapeDtypeStruct((depth, rows, n_out), F32),
        compiler_params=_params("parallel", "parallel"),
        name="ada_up",
    )(t, w_up, b_up.reshape(depth, 1, n_out))


def _norm_mod_kernel(x_ref, g_ref, sh_ref, sc_ref, o_ref):
    y = _rms(x_ref[...], g_ref[...])
    o_ref[...] = (y * (1.0 + sc_ref[0]) + sh_ref[0]).astype(o_ref.dtype)


def _norm_kernel(x_ref, g_ref, o_ref):
    o_ref[...] = _rms(x_ref[...], g_ref[...]).astype(o_ref.dtype)


def _seg_fn(tm, n_lat, n_batch):
    return lambda i: jnp.minimum((i * tm) // n_lat, n_batch)


def _norm_mod(x, g, mods, k_shift, k_scale, *, n_lat, n_batch, out_dtype, tm=256):
    t, d = x.shape
    seg = _seg_fn(tm, n_lat, n_batch)
    return pl.pallas_call(
        _norm_mod_kernel,
        grid=(t // tm,),
        in_specs=[pl.BlockSpec((tm, d), lambda i: (i, 0)),
                  pl.BlockSpec((1, d), lambda i: (0, 0)),
                  pl.BlockSpec((1, 1, d), lambda i: (seg(i) * N_MOD + k_shift, 0, 0)),
                  pl.BlockSpec((1, 1, d), lambda i: (seg(i) * N_MOD + k_scale, 0, 0))],
        out_specs=pl.BlockSpec((tm, d), lambda i: (i, 0)),
        out_shape=jax.ShapeDtypeStruct((t, d), out_dtype),
        compiler_params=_params("parallel"),
        name="norm_mod",
    )(x, g.reshape(1, d), mods, mods)


def _final_norm(x, g, rows, tm=256):
    d = x.shape[1]
    return pl.pallas_call(
        _norm_kernel,
        grid=(rows // tm,),
        in_specs=[pl.BlockSpec((tm, d), lambda i: (i, 0)),
                  pl.BlockSpec((1, d), lambda i: (0, 0))],
        out_specs=pl.BlockSpec((tm, d), lambda i: (i, 0)),
        out_shape=jax.ShapeDtypeStruct((rows, d), F32),
        compiler_params=_params("parallel"),
        name="final_norm",
    )(x, g.reshape(1, d))


def _mmw_kernel(a_ref, w_ref, cs_ref, o_ref, wb_sc):
    @pl.when(pl.program_id(1) == 0)
    def _():
        wb_sc[...] = (w_ref[...] * cs_ref[...]).astype(BF16)

    o_ref[...] = jnp.dot(a_ref[...], wb_sc[...], preferred_element_type=F32).astype(o_ref.dtype)


def _mmw_swiglu_kernel(a_ref, w1_ref, w3_ref, o_ref, w1_sc, w3_sc):
    @pl.when(pl.program_id(1) == 0)
    def _():
        w1_sc[...] = w1_ref[...].astype(BF16)
        w3_sc[...] = w3_ref[...].astype(BF16)

    a = a_ref[...]
    h1 = jnp.dot(a, w1_sc[...], preferred_element_type=F32)
    h3 = jnp.dot(a, w3_sc[...], preferred_element_type=F32)
    o_ref[...] = (h1 * jax.nn.sigmoid(h1) * h3).astype(o_ref.dtype)


def _mm_res_kernel(a_ref, b_ref, x_ref, g_ref, o_ref):
    acc = jnp.dot(a_ref[...], b_ref[...], preferred_element_type=F32)
    o_ref[...] = x_ref[...] + g_ref[0] * acc


def _pick_tile(n, candidates):
    for c in candidates:
        if n % c == 0:
            return c
    raise ValueError(f"no tile for {n}")


def _matmul_w32(a, w, layer, col_scale, col_block, *, out_dtype, tn):
    m, k = a.shape
    n = w.shape[2]
    tm = _pick_tile(m, ROW_TILES)
    return pl.pallas_call(
        _mmw_kernel,
        grid=(n // tn, m // tm),
        in_specs=[pl.BlockSpec((tm, k), lambda j, i: (i, 0)),
                  pl.BlockSpec((None, k, tn), lambda j, i: (layer, 0, col_block(j))),
                  pl.BlockSpec((1, tn), lambda j, i: (0, col_block(j)))],
        out_specs=pl.BlockSpec((tm, tn), lambda j, i: (i, j)),
        out_shape=jax.ShapeDtypeStruct((m, n), out_dtype),
        scratch_shapes=[pltpu.VMEM((k, tn), BF16)],
        compiler_params=_params("parallel", "arbitrary"),
        name="matmul_w32",
    )(a, w, col_scale)


def _matmul_swiglu_w32(a, w1, w3, layer, *, tn=256):
    m, k = a.shape
    n = w1.shape[2]
    tn = min(tn, n)
    tm = _pick_tile(m, ROW_TILES)
    return pl.pallas_call(
        _mmw_swiglu_kernel,
        grid=(n // tn, m // tm),
        in_specs=[pl.BlockSpec((tm, k), lambda j, i: (i, 0)),
                  pl.BlockSpec((None, k, tn), lambda j, i: (layer, 0, j)),
                  pl.BlockSpec((None, k, tn), lambda j, i: (layer, 0, j))],
        out_specs=pl.BlockSpec((tm, tn), lambda j, i: (i, j)),
        out_shape=jax.ShapeDtypeStruct((m, n), BF16),
        scratch_shapes=[pltpu.VMEM((k, tn), BF16), pltpu.VMEM((k, tn), BF16)],
        compiler_params=_params("parallel", "arbitrary"),
        name="matmul_swiglu_w32",
    )(a, w1, w3)


def _mm_res_wide_kernel(a_ref, b_ref, x_ref, glo_ref, ghi_ref, o_ref, *, tm, n_lat, n_batch):
    acc = jnp.dot(a_ref[...], b_ref[...], preferred_element_type=F32)
    row0 = pl.program_id(0) * tm
    seg_hi = jnp.minimum((row0 + tm - 1) // n_lat, n_batch)
    boundary = seg_hi * n_lat - row0
    rows = lax.broadcasted_iota(jnp.int32, (tm, 1), 0)
    gate = jnp.where(rows < boundary, glo_ref[0], ghi_ref[0])
    o_ref[...] = x_ref[...] + gate * acc


def _matmul_residual_wide(a, w, layer, x, mods, k_gate, *, n_lat, n_batch, tn=512):
    m, k = a.shape
    n = w.shape[2]
    tn = min(tn, n)
    tm = _pick_tile(m, ROW_TILES)
    assert tm <= n_lat

    def seg_lo(i):
        return jnp.minimum((i * tm) // n_lat, n_batch)

    def seg_hi(i):
        return jnp.minimum((i * tm + tm - 1) // n_lat, n_batch)

    return pl.pallas_call(
        functools.partial(_mm_res_wide_kernel, tm=tm, n_lat=n_lat, n_batch=n_batch),
        grid=(m // tm, n // tn),
        in_specs=[pl.BlockSpec((tm, k), lambda i, j: (i, 0)),
                  pl.BlockSpec((None, k, tn), lambda i, j: (layer, 0, j)),
                  pl.BlockSpec((tm, tn), lambda i, j: (i, j)),
                  pl.BlockSpec((1, 1, tn), lambda i, j: (seg_lo(i) * N_MOD + k_gate, 0, j)),
                  pl.BlockSpec((1, 1, tn), lambda i, j: (seg_hi(i) * N_MOD + k_gate, 0, j))],
        out_specs=pl.BlockSpec((tm, tn), lambda i, j: (i, j)),
        out_shape=jax.ShapeDtypeStruct((m, n), F32),
        compiler_params=_params("parallel", "arbitrary"),
        name="matmul_residual_wide",
    )(a, w, x, mods, mods)


def _matmul_residual(a, w, layer, x, mods, k_gate, *, n_lat, n_batch, tm=512, tn=512):
    m, k = a.shape
    n = w.shape[2]
    tn = min(tn, n)
    seg = _seg_fn(tm, n_lat, n_batch)
    return pl.pallas_call(
        _mm_res_kernel,
        grid=(m // tm, n // tn),
        in_specs=[pl.BlockSpec((tm, k), lambda i, j: (i, 0)),
                  pl.BlockSpec((None, k, tn), lambda i, j: (layer, 0, j)),
                  pl.BlockSpec((tm, tn), lambda i, j: (i, j)),
                  pl.BlockSpec((1, 1, tn), lambda i, j: (seg(i) * N_MOD + k_gate, 0, j))],
        out_specs=pl.BlockSpec((tm, tn), lambda i, j: (i, j)),
        out_shape=jax.ShapeDtypeStruct((m, n), F32),
        compiler_params=_params("parallel", "arbitrary"),
        name="matmul_residual",
    )(a, w, x, mods)


def _sgu_kernel(u_ref, v_ref, gv_ref, ws_ref, bs_ref, ga_ref, o_ref, *, n_chunks, n_groups):
    for c in range(n_chunks):
        rows = slice(c * CHUNK, (c + 1) * CHUNK)
        u = jax.nn.gelu(u_ref[rows, :].astype(F32), approximate=True)
        v = jax.nn.gelu(v_ref[rows, :].astype(F32), approximate=True)
        vb = _rms(v, gv_ref[...]).astype(BF16)
        parts = [jnp.dot(ws_ref[g], vb[:, g * HEAD_DIM:(g + 1) * HEAD_DIM],
                         preferred_element_type=F32) for g in range(n_groups)]
        y = u * (jnp.concatenate(parts, axis=1) + bs_ref[...])
        o_ref[rows, :] = _rms(y, ga_ref[...]).astype(o_ref.dtype)


def _sgu(p, g_v, w_s, b_s, g_a, *, width, ta=256):
    t = p.shape[0]
    n_groups = width // HEAD_DIM
    bias =jnp.repeat(b_s.T, HEAD_DIM, axis=1)
    return pl.pallas_call(
        functools.partial(_sgu_kernel, n_chunks=ta // CHUNK, n_groups=n_groups),
        grid=(t // ta,),
        in_specs=[pl.BlockSpec((ta, width), lambda i: (i, 0)),
                  pl.BlockSpec((ta, width), lambda i: (i, 1)),
                  pl.BlockSpec((1, width), lambda i: (0, 0)),
                  pl.BlockSpec((n_groups, CHUNK, CHUNK), lambda i: (0, 0, 0)),
                  pl.BlockSpec((CHUNK, width), lambda i: (0, 0)),
                  pl.BlockSpec((1, width), lambda i: (0, 0))],
        out_specs=pl.BlockSpec((ta, width), lambda i: (i, 0)),
        out_shape=jax.ShapeDtypeStruct((t, width), BF16),
        compiler_params=_params("parallel"),
        name="sgu",
    )(p, p, g_v.reshape(1, width), w_s.astype(BF16), bias, g_a.reshape(1, width))


def _na_bias_tables(rpb, rows):
    n_blocks = rows // NA_Q_ROWS
    cols = np.arange(GRID_W)
    col_start = np.clip(cols - NA_COLS // 2, 0, GRID_W - NA_COLS)
    dc = cols[None, :] - cols[:, None]
    col_ok = (cols[None, :] >= col_start[:, None]) & (cols[None, :] < col_start[:, None] + NA_COLS)
    col_idx = np.where(col_ok, dc + NA_COLS - 1, 0)
    n_dr = 2 * NA_ROWS - 1
    t = jnp.where(col_ok[None, None], rpb[:, :, col_idx] * LOG2E, MASK_VALUE)
    t = jnp.concatenate([t, jnp.full((rpb.shape[0], 1, GRID_W, GRID_W), MASK_VALUE, F32)], axis=1)
    slab = np.full((3, NA_Q_ROWS, NA_BAND_ROWS), n_dr, np.int32)
    rep = [0, min(1, n_blocks - 1), n_blocks - 1]
    for ty, blk in enumerate(rep):
        band0 = int(np.clip(blk * NA_Q_ROWS - NA_ROWS // 2, 0, rows - NA_BAND_ROWS))
        for rq in range(NA_Q_ROWS):
            r = blk * NA_Q_ROWS + rq
            start = int(np.clip(r - NA_ROWS // 2, 0, rows - NA_ROWS))
            for kr in range(NA_BAND_ROWS):
                ar = band0 + kr
                if start <= ar < start + NA_ROWS:
                    slab[ty, rq, kr] = ar - r + NA_ROWS - 1
    pairs = slab.reshape(3, NA_Q_ROWS, NA_BAND_ROWS // 2, 2)
    uniq, inverse = np.unique(pairs.reshape(-1, 2), axis=0, return_inverse=True)
    t2 = jnp.concatenate([t[:, uniq[:, 0]], t[:, uniq[:, 1]]], axis=-1)
    return t2, jnp.asarray(inverse.reshape(-1), jnp.int32)


def _na_kernel(pair_ref, q_ref, k_ref, v_ref, kc_ref, vc_ref, bias_ref, o_ref, *, rows, rc, n_blocks,
               heads):
    i = pl.program_id(2)
    block_type = jnp.where(i == 0, 0, jnp.where(i == n_blocks - 1, 2, 1))
    pairs_per_row = NA_BAND_ROWS // 2

    def bias_rows(hh, rq):
        base = (block_type * NA_Q_ROWS + rq) * pairs_per_row
        return jnp.concatenate([bias_ref[hh, pair_ref[base + m]] for m in range(pairs_per_row)], axis=1)

    band0 = jnp.clip(i * NA_Q_ROWS - NA_ROWS // 2, 0, rows - NA_BAND_ROWS)
    start = pl.multiple_of(band0 * GRID_W, GRID_W)
    nb = NA_BAND_ROWS * GRID_W
    dn = (((1,), (1,)), ((), ()))
    n_chunks = q_ref.shape[0] // rc
    n_loc = nb // HEAD_DIM
    units = [(hh, c) for hh in range(heads) for c in range(n_chunks)]

    def lanes(hh):
        return slice(hh * HEAD_DIM, (hh + 1) * HEAD_DIM)

    def scores(unit):
        hh, c = unit
        q = q_ref[c * rc:(c + 1) * rc, lanes(hh)]
        return (lax.dot_general(q, k_ref[pl.ds(start, nb), lanes(hh)], dn, preferred_element_type=F32),
                lax.dot_general(q, kc_ref[:, lanes(hh)], dn, preferred_element_type=F32))

    nxt = scores(units[0])
    for u, (hh, c) in enumerate(units):
        s_loc, s_ctx = nxt
        if u + 1 < len(units):
            nxt = scores(units[u + 1])
        vb = v_ref[pl.ds(start, nb), lanes(hh)]
        vc = vc_ref[:, lanes(hh)]
        q_rows = rc // GRID_W
        s_loc = s_loc + jnp.concatenate([bias_rows(hh, c * q_rows + r) for r in range(q_rows)], axis=0)
        cols = ([s_loc[:, j * HEAD_DIM:(j + 1) * HEAD_DIM] for j in range(n_loc)]
                + [s_ctx[:, j * HEAD_DIM:(j + 1) * HEAD_DIM] for j in range(s_ctx.shape[1] // HEAD_DIM)])
        m = jnp.max(functools.reduce(jnp.maximum, cols), axis=1, keepdims=True)
        ps = [jnp.exp2(x - m) for x in cols]
        l = jnp.sum(functools.reduce(jnp.add, ps), axis=1, keepdims=True)
        p_loc = jnp.concatenate([x.astype(BF16) for x in ps[:n_loc]], axis=1)
        p_ctx = jnp.concatenate([x.astype(BF16) for x in ps[n_loc:]], axis=1)
        o = (jnp.dot(p_loc, vb, preferred_element_type=F32)
             + jnp.dot(p_ctx, vc, preferred_element_type=F32))
        o_ref[c * rc:(c + 1) * rc, lanes(hh)] = (o / l).astype(o_ref.dtype)


def _neighbourhood_attention(p, bias, pair_idx, *, n_batch, n_lat, n_ctx, q_col, k_col, v_col, n_heads):
    rows = n_lat // GRID_W
    tq = NA_Q_ROWS * GRID_W
    n_blocks = n_lat // tq
    ctx_blk0 = (n_batch * n_lat) // n_ctx
    n_pairs = bias.shape[1]
    hp = math.gcd(NA_HEADS_PER_STEP, n_heads, q_col, k_col, v_col)
    w = hp * HEAD_DIM
    qc, kc, vc = q_col // hp, k_col // hp, v_col // hp
    return pl.pallas_call(
        functools.partial(_na_kernel, rows=rows, rc=tq, n_blocks=n_blocks, heads=hp),
        grid_spec=pltpu.PrefetchScalarGridSpec(
            num_scalar_prefetch=1, grid=(n_heads // hp, n_batch, n_blocks),
            in_specs=[pl.BlockSpec((tq, w), lambda h, b, i, pr: (b * n_blocks + i, qc + h)),
                      pl.BlockSpec((n_lat, w), lambda h, b, i, pr: (b, kc + h)),
                      pl.BlockSpec((n_lat, w), lambda h, b, i, pr: (b, vc + h)),
                      pl.BlockSpec((n_ctx, w), lambda h, b, i, pr: (ctx_blk0 + b, kc + h)),
                      pl.BlockSpec((n_ctx, w), lambda h, b, i, pr: (ctx_blk0 + b, vc + h)),
                      pl.BlockSpec((hp, n_pairs, GRID_W, 2 * GRID_W), lambda h, b, i, pr: (h, 0, 0, 0))],
            out_specs=pl.BlockSpec((tq, w), lambda h, b, i, pr: (b * n_blocks + i, h))),
        out_shape=jax.ShapeDtypeStruct((n_batch * n_lat, n_heads * HEAD_DIM), BF16),
        compiler_params=_params("parallel", "parallel", "arbitrary"),
        name="neighbourhood_attention",
    )(pair_idx, p, p, p, p, p, bias)


def _rope_tables(n_lat, n_ctx):
    t = jnp.arange(n_lat, dtype=jnp.int32)
    pos = jnp.stack([t // GRID_W, t % GRID_W], axis=-1).astype(F32)
    n_freq = HEAD_DIM // 4
    inv = 1.0 / (ROPE_THETA ** (jnp.arange(n_freq, dtype=F32) / n_freq))
    ang = pos[:, :, None] * inv
    cos, sin = jnp.cos(ang), jnp.sin(ang)
    cos_t = jnp.concatenate([cos[:, 0], cos[:, 0], cos[:, 1], cos[:, 1]], axis=-1)
    sin_t = jnp.concatenate([-sin[:, 0], sin[:, 0], -sin[:, 1], sin[:, 1]], axis=-1)
    cos_t = jnp.concatenate([cos_t, jnp.ones((n_ctx, HEAD_DIM), F32)], axis=0)
    sin_t = jnp.concatenate([sin_t, jnp.zeros((n_ctx, HEAD_DIM), F32)], axis=0)
    return cos_t, sin_t


def _qk_prep_kernel(q_ref, k_ref, v_ref, cos_ref, sin_ref, gq_ref, gk_ref,
                    qo_ref, ko_ref, vo_ref, *, n_q, n_kv):
    cos = cos_ref[...]
    sin = sin_ref[...]
    quarter = HEAD_DIM // 4
    r_idx = lax.broadcasted_iota(jnp.int32, (HEAD_DIM, HEAD_DIM), 0)
    c_idx = lax.broadcasted_iota(jnp.int32, (HEAD_DIM, HEAD_DIM), 1)
    exchange = jnp.where(r_idx == jnp.bitwise_xor(c_idx, quarter), 1.0, 0.0).astype(F32)

    def prep(x, g, out_scale):
        y = _rms(x.astype(F32), g)
        swapped = jnp.dot(y, exchange, preferred_element_type=F32, precision=lax.Precision.HIGHEST)
        return ((y * cos + swapped * sin) * out_scale).astype(BF16)

    for h in range(n_q):
        cs = slice(h * HEAD_DIM, (h + 1) * HEAD_DIM)
        qo_ref[:, cs] = prep(q_ref[:, cs], gq_ref[...], QK_PRESCALE)
    ones = jnp.ones((v_ref.shape[0], HEAD_DIM), BF16)
    for h in range(n_kv):
        cs = slice(h * HEAD_DIM, (h + 1) * HEAD_DIM)
        ko_ref[0, :, cs] = prep(k_ref[:, cs], gk_ref[...], 1.0)
        vo_ref[0, :, 2 * h * HEAD_DIM:(2 * h + 1) * HEAD_DIM] = v_ref[:, cs]
        vo_ref[0, :, (2 * h + 1) * HEAD_DIM:(2 * h + 2) * HEAD_DIM] = ones


def _qk_prep(p, cos_t, sin_t, g_q, g_k, *, n_batch, n_lat, n_ctx, q_off, n_q, k_off, v_off, n_kv):
    t = p.shape[0]
    tm = n_ctx
    lat_tiles = n_lat // tm
    n_lat_tiles = n_batch * lat_tiles

    def pos_blk(i):
        return jnp.where(i < n_lat_tiles, i % lat_tiles, lat_tiles)

    def kv_blk(i):
        c = i - n_lat_tiles
        return (jnp.where(i < n_lat_tiles, i // lat_tiles, c),
                jnp.where(i < n_lat_tiles, i % lat_tiles, lat_tiles), 0)

    wq, wkv = n_q * HEAD_DIM, n_kv * HEAD_DIM
    return pl.pallas_call(
        functools.partial(_qk_prep_kernel, n_q=n_q, n_kv=n_kv),
        grid=(t // tm,),
        in_specs=[pl.BlockSpec((tm, wq), lambda i: (i, q_off)),
                  pl.BlockSpec((tm, wkv), lambda i: (i, k_off)),
                  pl.BlockSpec((tm, wkv), lambda i: (i, v_off)),
                  pl.BlockSpec((tm, HEAD_DIM), lambda i: (pos_blk(i), 0)),
                  pl.BlockSpec((tm, HEAD_DIM), lambda i: (pos_blk(i), 0)),
                  pl.BlockSpec((1, HEAD_DIM), lambda i: (0, 0)),
                  pl.BlockSpec((1, HEAD_DIM), lambda i: (0, 0))],
        out_specs=[pl.BlockSpec((tm, wq), lambda i: (i, 0)),
                   pl.BlockSpec((1, tm, wkv), kv_blk),
                   pl.BlockSpec((1, tm, 2 * wkv), kv_blk)],
        out_shape=[jax.ShapeDtypeStruct((t, wq), BF16),
                   jax.ShapeDtypeStruct((n_batch, n_lat + n_ctx, wkv), BF16),
                   jax.ShapeDtypeStruct((n_batch, n_lat + n_ctx, 2 * wkv), BF16)],
        compiler_params=_params("parallel"),
        name="qk_prep",
    )(p, p, p, cos_t, sin_t, g_q.reshape(1, HEAD_DIM), g_k.reshape(1, HEAD_DIM))


def _flash_kernel(q_ref, k_ref, v_ref, o_ref, m_sc, l_sc, acc_sc, *, group, tq):
    kj = pl.program_id(3)

    @pl.when(kj == 0)
    def _():
        m_sc[...] = jnp.full_like(m_sc, -jnp.inf)
        l_sc[...] = jnp.zeros_like(l_sc)
        acc_sc[...] = jnp.zeros_like(acc_sc)

    if group == 1:
        q = q_ref[...]
    else:
        q = jnp.concatenate([q_ref[:, g * HEAD_DIM:(g + 1) * HEAD_DIM] for g in range(group)], axis=0)
    s = lax.dot_general(q, k_ref[0], (((1,), (1,)), ((), ())), preferred_element_type=F32)
    m_prev = m_sc[...]
    m_new = jnp.maximum(m_prev, jnp.max(s, axis=1, keepdims=True))
    alpha = jnp.exp2(m_prev - m_new)
    p = jnp.exp2(s - m_new)
    l_sc[...] = alpha * l_sc[...] + jnp.sum(p, axis=1, keepdims=True)
    acc_sc[...] = alpha * acc_sc[...] + jnp.dot(p.astype(BF16), v_ref[0], preferred_element_type=F32)
    m_sc[...] = m_new

    @pl.when(kj == pl.num_programs(3) - 1)
    def _():
        o = acc_sc[...] / l_sc[...]
        for g in range(group):
            o_ref[:, g * HEAD_DIM:(g + 1) * HEAD_DIM] = o[g * tq:(g + 1) * tq].astype(o_ref.dtype)


def _flash(q2d, k3d, v3d, *, n_batch, n_q_rows, q_row0, q_rows_per_batch, q_col, n_kv_heads,
           group, kv_row0, n_kv_rows, tq=256):
    tq = min(tq, n_q_rows)
    tk = _pick_tile(n_kv_rows, (1024, 768, 512, 256))
    qb0, qbb = q_row0 // tq, q_rows_per_batch // tq
    kb0 = kv_row0 // tk
    gw = group * HEAD_DIM
    return pl.pallas_call(
        functools.partial(_flash_kernel, group=group, tq=tq),
        grid=(n_batch, n_kv_heads, n_q_rows // tq, n_kv_rows // tk),
        in_specs=[pl.BlockSpec((tq, gw), lambda b, h, i, j: (qb0 + b * qbb + i, q_col + h)),
                  pl.BlockSpec((1, tk, HEAD_DIM), lambda b, h, i, j: (b, kb0 + j, h)),
                  pl.BlockSpec((1, tk, HEAD_DIM), lambda b, h, i, j: (b, kb0 + j, h))],
        out_specs=pl.BlockSpec((tq, gw), lambda b, h, i, j: (b * (n_q_rows // tq) + i, h)),
        out_shape=jax.ShapeDtypeStruct((n_batch * n_q_rows, n_kv_heads * gw), BF16),
        scratch_shapes=[pltpu.VMEM((group * tq, 1), F32),
                        pltpu.VMEM((group * tq, 1), F32),
                        pltpu.VMEM((group * tq, HEAD_DIM), F32)],
        compiler_params=_params("parallel", "parallel", "parallel", "arbitrary"),
        name="flash_attention",
    )(q2d, k3d, v3d)


def _flash_gqa_kernel(q_ref, k_ref, v_ref, o_ref, m_sc, acc_sc, *, group, tq, tk, rc):
    kj = pl.program_id(3)

    @pl.when(kj == 0)
    def _():
        m_sc[...] = jnp.full_like(m_sc, -jnp.inf)
        acc_sc[...] = jnp.zeros_like(acc_sc)

    n_cols = tk // HEAD_DIM
    k = k_ref[0]
    v = v_ref[0]
    chunks = [(g, r0) for g in range(group) for r0 in range(0, tq, rc)]

    def scores(chunk):
        g, r0 = chunk
        q = q_ref[r0:r0 + rc, g * HEAD_DIM:(g + 1) * HEAD_DIM]
        return lax.dot_general(q, k, (((1,), (1,)), ((), ())), preferred_element_type=F32)

    pending = [scores(ch) for ch in chunks[:FLASH_LOOKAHEAD]]
    for ci, (g, r0) in enumerate(chunks):
        if ci + FLASH_LOOKAHEAD < len(chunks):
            pending.append(scores(chunks[ci + FLASH_LOOKAHEAD]))
        s = pending.pop(0)
        rows = slice(g * tq + r0, g * tq + r0 + rc)
        cols = [s[:, i * HEAD_DIM:(i + 1) * HEAD_DIM] for i in range(n_cols)]
        col_max = functools.reduce(jnp.maximum, cols)
        m_prev = m_sc[rows, :]
        m_new = jnp.maximum(m_prev, jnp.max(col_max, axis=1, keepdims=True))
        alpha = jnp.exp2(m_prev - m_new)
        p = jnp.concatenate([jnp.exp2(c - m_new).astype(BF16) for c in cols], axis=1)
        pv = jnp.dot(p, v, preferred_element_type=F32)
        acc_sc[rows, :] = jnp.concatenate([alpha, alpha], axis=1) * acc_sc[rows, :] + pv
        m_sc[rows, :] = m_new

    @pl.when(kj == pl.num_programs(3) - 1)
    def _():
        o = acc_sc[:, :HEAD_DIM] / acc_sc[:, HEAD_DIM:]
        for g in range(group):
            o_ref[:, g * HEAD_DIM:(g + 1) * HEAD_DIM] = o[g * tq:(g + 1) * tq].astype(o_ref.dtype)


def _flash_gqa(q2d, k3d, v3d, *, n_batch, n_q_rows, q_row0, q_rows_per_batch, n_kv_heads, group,
               kv_row0, n_kv_rows, tq=512, rc=512):
    tq = min(tq, n_q_rows)
    rc = min(rc, tq)
    tk = _pick_tile(n_kv_rows, (2816, 1408, 1024, 768, 512, 256))
    qb0, qbb = q_row0 // tq, q_rows_per_batch // tq
    kb0 = kv_row0 // tk
    gw = group * HEAD_DIM
    return pl.pallas_call(
        functools.partial(_flash_gqa_kernel, group=group, tq=tq, tk=tk, rc=rc),
        grid=(n_batch, n_kv_heads, n_q_rows // tq, n_kv_rows // tk),
        in_specs=[pl.BlockSpec((tq, gw), lambda b, h, i, j: (qb0 + b * qbb + i, h)),
                  pl.BlockSpec((1, tk, HEAD_DIM), lambda b, h, i, j: (b, kb0 + j, h)),
                  pl.BlockSpec((1, tk, 2 * HEAD_DIM), lambda b, h, i, j: (b, kb0 + j, h))],
        out_specs=pl.BlockSpec((tq, gw), lambda b, h, i, j: (b * (n_q_rows // tq) + i, h)),
        out_shape=jax.ShapeDtypeStruct((n_batch * n_q_rows, n_kv_heads * gw), BF16),
        scratch_shapes=[pltpu.VMEM((group * tq, HEAD_DIM), F32),
                        pltpu.VMEM((group * tq, 2 * HEAD_DIM), F32)],
        compiler_params=_params("parallel", "parallel", "parallel", "arbitrary"),
        name="flash_gqa",
    )(q2d, k3d, v3d)


def _merge_kernel(ya_ref, yb_lat_ref, yb_ctx_ref, yc_lat_ref, yc_ctx_ref, gb_ref, gc_ref, o_ref,
                  *, wa, wb, n_lat_tiles):
    o_ref[:, :wa] = ya_ref[...]

    def emit(yb_ref, yc_ref):
        o_ref[:, wa:wa + wb] = _rms(yb_ref[...].astype(F32), gb_ref[...]).astype(o_ref.dtype)
        o_ref[:, wa + wb:] = _rms(yc_ref[...].astype(F32), gc_ref[...]).astype(o_ref.dtype)

    is_lat = pl.program_id(0) < n_lat_tiles
    pl.when(is_lat)(lambda: emit(yb_lat_ref, yc_lat_ref))
    pl.when(jnp.logical_not(is_lat))(lambda: emit(yb_ctx_ref, yc_ctx_ref))


def _merge(ya, yb_lat, yb_ctx, yc_lat, yc_ctx, g_b, g_c, tm=256):
    t, wa = ya.shape
    wb, wc = yb_lat.shape[1], yc_lat.shape[1]
    n_lat_tiles = yb_lat.shape[0] // tm

    def lat(i):
        return (jnp.minimum(i, n_lat_tiles - 1), 0)

    def ctx(i):
        return (jnp.maximum(i - n_lat_tiles, 0), 0)

    return pl.pallas_call(
        functools.partial(_merge_kernel, wa=wa, wb=wb, n_lat_tiles=n_lat_tiles),
        grid=(t // tm,),
        in_specs=[pl.BlockSpec((tm, wa), lambda i: (i, 0)),
                  pl.BlockSpec((tm, wb), lat), pl.BlockSpec((tm, wb), ctx),
                  pl.BlockSpec((tm, wc), lat), pl.BlockSpec((tm, wc), ctx),
                  pl.BlockSpec((1, wb), lambda i: (0, 0)),
                  pl.BlockSpec((1, wc), lambda i: (0, 0))],
        out_specs=pl.BlockSpec((tm, wa + wb + wc), lambda i: (i, 0)),
        out_shape=jax.ShapeDtypeStruct((t, wa + wb + wc), BF16),
        compiler_params=_params("parallel"),
        name="merge_groups",
    )(ya, yb_lat, yb_ctx, yc_lat, yc_ctx, g_b.reshape(1, wb), g_c.reshape(1, wc))


def _norm_mod_route_kernel(x_ref, g_ref, sh_ref, sc_ref, w_ref, b_ref, o_ref, r_ref):
    y = _rms(x_ref[...], g_ref[...])
    o_ref[...] = y * (1.0 + sc_ref[0]) + sh_ref[0]
    _router_kernel(o_ref, w_ref, b_ref, r_ref)


def _norm_mod_route(x, g, mods, k_shift, k_scale, w_r, b_r, *, n_lat, n_batch, tm=256):
    t, d = x.shape
    e = w_r.shape[1]
    seg = _seg_fn(tm, n_lat, n_batch)
    w_pad = jnp.zeros((d, ROUTER_LANES), BF16).at[:, :e].set(w_r.astype(BF16))
    b_pad = jnp.full((1, ROUTER_LANES), -jnp.inf, F32).at[0, :e].set(b_r)
    return pl.pallas_call(
        _norm_mod_route_kernel,
        grid=(t // tm,),
        in_specs=[pl.BlockSpec((tm, d), lambda i: (i, 0)),
                  pl.BlockSpec((1, d), lambda i: (0, 0)),
                  pl.BlockSpec((1, 1, d), lambda i: (seg(i) * N_MOD + k_shift, 0, 0)),
                  pl.BlockSpec((1, 1, d), lambda i: (seg(i) * N_MOD + k_scale, 0, 0)),
                  pl.BlockSpec((d, ROUTER_LANES), lambda i: (0, 0)),
                  pl.BlockSpec((1, ROUTER_LANES), lambda i: (0, 0))],
        out_specs=[pl.BlockSpec((tm, d), lambda i: (i, 0)),
                   pl.BlockSpec((tm, ROUTER_LANES), lambda i: (i, 0))],
        out_shape=[jax.ShapeDtypeStruct((t, d), F32),
                   jax.ShapeDtypeStruct((t, ROUTER_LANES), F32)],
        compiler_params=_params("parallel"),
        name="norm_mod_route",
    )(x, g.reshape(1, d), mods, mods, w_pad, b_pad)


def _router_kernel(h_ref, w_ref, b_ref, o_ref):
    logits = jnp.dot(h_ref[...].astype(BF16), w_ref[...], preferred_element_type=F32) + b_ref[...]
    lane = lax.broadcasted_iota(jnp.int32, logits.shape, 1).astype(F32)
    m1 = jnp.max(logits, axis=1, keepdims=True)
    i1 = jnp.min(jnp.where(logits == m1, lane, float(ROUTER_LANES)), axis=1, keepdims=True)
    rest = jnp.where(lane == i1, -jnp.inf, logits)
    m2 = jnp.max(rest, axis=1, keepdims=True)
    i2 = jnp.min(jnp.where(rest == m2, lane, float(ROUTER_LANES)), axis=1, keepdims=True)
    e2 = jnp.exp(m2 - m1)
    g1 = 1.0 / (1.0 + e2)
    o_ref[...] = jnp.where(lane == 0.0, i1,
                           jnp.where(lane == 1.0, i2,
                                     jnp.where(lane == 2.0, g1,
                                               jnp.where(lane == 3.0, e2 * g1, 0.0))))


def _route_plan(routes, n_experts, tm):
    t = routes.shape[0]
    a = 2 * t
    flat_e = routes[:, :2].astype(jnp.int32).reshape(a)
    onehot = (flat_e[:, None] == jnp.arange(n_experts, dtype=jnp.int32)[None, :]).astype(jnp.int32)
    csum = jnp.cumsum(onehot, axis=0)
    counts = csum[-1]
    rank = jnp.take_along_axis(csum, flat_e[:, None], axis=1)[:, 0] - 1
    padded = ((counts + tm - 1) // tm) * tm
    ends = jnp.cumsum(padded)
    starts = ends - padded
    dest = starts[flat_e] + rank
    ustarts = jnp.cumsum(counts) - counts
    order = jnp.argsort(flat_e, stable=True).astype(jnp.int32)
    n_rows = a + n_experts * tm
    row = jnp.arange(n_rows, dtype=jnp.int32)
    row_e = jnp.minimum(jnp.searchsorted(ends, row, side="right"), n_experts - 1).astype(jnp.int32)
    k = row - starts[row_e]
    src_a = order[jnp.clip(ustarts[row_e] + k, 0, a - 1)]
    src = jnp.where(k < counts[row_e], src_a // 2, 0).astype(jnp.int32)
    tile_start = jnp.arange(n_rows // tm, dtype=jnp.int32) * tm
    tile_valid = (tile_start < ends[-1]).astype(jnp.int32)
    tile_expert = row_e[jnp.minimum(tile_start, ends[-1] - 1)]
    return src, dest.astype(jnp.int32), tile_expert, tile_valid


def _gather_rows_kernel(src_ref, h_hbm, o_ref, buf, sem, *, tg):
    i = pl.program_id(0)

    def copy(tile, j):
        slot = tile % 2
        return pltpu.make_async_copy(h_hbm.at[pl.ds(src_ref[tile * tg + j], 1)],
                                     buf.at[slot, pl.ds(j, 1)], sem.at[slot])

    def issue(tile):
        def body(j, carry):
            copy(tile, j).start()
            return carry
        lax.fori_loop(0, tg, body, 0, unroll=8)

    pl.when(i == 0)(lambda: issue(0))
    pl.when(i + 1 < pl.num_programs(0))(lambda: issue(i + 1))

    slot = i % 2
    pltpu.make_async_copy(h_hbm.at[pl.ds(0, tg)], buf.at[slot], sem.at[slot]).wait()
    o_ref[...] = buf[slot].astype(o_ref.dtype)


def _gather_rows(h, src, *, out_dtype, tg=512):
    n = src.shape[0]
    d = h.shape[1]
    return pl.pallas_call(
        functools.partial(_gather_rows_kernel, tg=tg),
        grid_spec=pltpu.PrefetchScalarGridSpec(
            num_scalar_prefetch=1, grid=(n // tg,),
            in_specs=[pl.BlockSpec(memory_space=pl.ANY)],
            out_specs=pl.BlockSpec((tg, d), lambda i, s: (i, 0)),
            scratch_shapes=[pltpu.VMEM((2, tg, d), h.dtype), pltpu.SemaphoreType.DMA((2,))]),
        out_shape=jax.ShapeDtypeStruct((n, d), out_dtype),
        compiler_params=_params("arbitrary"),
        name="moe_gather",
    )(src, h)


def _expert_changed(te_ref, i):
    return jnp.logical_or(i == 0, te_ref[i] != te_ref[jnp.maximum(i - 1, 0)])


def _moe_up_kernel(te_ref, tv_ref, x_ref, w1_ref, w3_ref, o_ref, w1_sc, w3_sc):
    i = pl.program_id(1)
    valid = tv_ref[i] == 1

    @pl.when(jnp.logical_and(valid, _expert_changed(te_ref, i)))
    def _():
        w1_sc[...] = w1_ref[...].astype(BF16)
        w3_sc[...] = w3_ref[...].astype(BF16)

    @pl.when(valid)
    def _():
        a = x_ref[...]
        h1 = jnp.dot(a, w1_sc[...], preferred_element_type=F32)
        h3 = jnp.dot(a, w3_sc[...], preferred_element_type=F32)
        o_ref[...] = (h1 * jax.nn.sigmoid(h1) * h3).astype(o_ref.dtype)

    @pl.when(jnp.logical_not(valid))
    def _():
        o_ref[...] = jnp.zeros_like(o_ref)


def _moe_down_kernel(te_ref, tv_ref, a_ref, w_ref, o_ref, w_sc):
    i = pl.program_id(1)
    valid = tv_ref[i] == 1

    @pl.when(jnp.logical_and(valid, _expert_changed(te_ref, i)))
    def _():
        w_sc[...] = w_ref[...].astype(BF16)

    @pl.when(valid)
    def _():
        o_ref[...] = jnp.dot(a_ref[...], w_sc[...], preferred_element_type=F32)

    @pl.when(jnp.logical_not(valid))
    def _():
        o_ref[...] = jnp.zeros_like(o_ref)


def _moe_experts(xs, w1, w3, w2, layer, tile_expert, tile_valid, *, tm, tn_up=512, tn_down=2048):
    rows, d = xs.shape
    f = w1.shape[-1]
    tn_up, tn_down = min(tn_up, f), min(tn_down, d)
    n_tiles = rows // tm
    hid = pl.pallas_call(
        _moe_up_kernel,
        grid_spec=pltpu.PrefetchScalarGridSpec(
            num_scalar_prefetch=2, grid=(f // tn_up, n_tiles),
            in_specs=[pl.BlockSpec((tm, d), lambda j, i, te, tv: (i, 0)),
                      pl.BlockSpec((None, None, d, tn_up), lambda j, i, te, tv: (layer, te[i], 0, j)),
                      pl.BlockSpec((None, None, d, tn_up), lambda j, i, te, tv: (layer, te[i], 0, j))],
            out_specs=pl.BlockSpec((tm, tn_up), lambda j, i, te, tv: (i, j)),
            scratch_shapes=[pltpu.VMEM((d, tn_up), BF16), pltpu.VMEM((d, tn_up), BF16)]),
        out_shape=jax.ShapeDtypeStruct((rows, f), BF16),
        compiler_params=_params("parallel", "arbitrary"),
        name="moe_up",
    )(tile_expert, tile_valid, xs, w1, w3)
    return pl.pallas_call(
        _moe_down_kernel,
        grid_spec=pltpu.PrefetchScalarGridSpec(
            num_scalar_prefetch=2, grid=(d // tn_down, n_tiles),
            in_specs=[pl.BlockSpec((tm, f), lambda j, i, te, tv: (i, 0)),
                      pl.BlockSpec((None, None, f, tn_down), lambda j, i, te, tv: (layer, te[i], 0, j))],
            out_specs=pl.BlockSpec((tm, tn_down), lambda j, i, te, tv: (i, j)),
            scratch_shapes=[pltpu.VMEM((f, tn_down), BF16)]),
        out_shape=jax.ShapeDtypeStruct((rows, d), F32),
        compiler_params=_params("parallel", "arbitrary"),
        name="moe_down",
    )(tile_expert, tile_valid, hid, w2)


def _moe_combine_kernel(dest_ref, x_ref, r_ref, g_ref, y_hbm, o_ref, ybuf, sem, *, tc):
    i = pl.program_id(0)

    def copy(tile, j, k):
        slot = tile % 2
        return pltpu.make_async_copy(y_hbm.at[pl.ds(dest_ref[(tile * tc + j) * 2 + k], 1)],
                                     ybuf.at[slot, k, pl.ds(j, 1)], sem.at[slot])

    def issue(tile):
        def body(j, carry):
            copy(tile, j, 0).start()
            copy(tile, j, 1).start()
            return carry
        lax.fori_loop(0, tc, body, 0, unroll=4)

    pl.when(i == 0)(lambda: issue(0))
    pl.when(i + 1 < pl.num_programs(0))(lambda: issue(i + 1))

    slot = i % 2
    for k in range(2):
        pltpu.make_async_copy(y_hbm.at[pl.ds(0, tc)], ybuf.at[slot, k], sem.at[slot]).wait()
    gate = g_ref[0]
    rows = 32

    def row_block(c, carry):
        rs = pl.ds(pl.multiple_of(c * rows, rows), rows)
        r = r_ref[rs, :]
        y = r[:, 2:3] * ybuf[slot, 0, rs, :] + r[:, 3:4] * ybuf[slot, 1, rs, :]
        o_ref[rs, :] = x_ref[rs, :] + gate * y
        return carry

    lax.fori_loop(0, tc // rows, row_block, 0, unroll=2)


def _moe_combine(x, y, routes, dest, mods, k_gate, *, n_lat, n_batch, tc=256):
    t, d = x.shape
    seg = _seg_fn(tc, n_lat, n_batch)
    return pl.pallas_call(
        functools.partial(_moe_combine_kernel, tc=tc),
        grid_spec=pltpu.PrefetchScalarGridSpec(
            num_scalar_prefetch=1, grid=(t // tc,),
            in_specs=[pl.BlockSpec((tc, d), lambda i, ds: (i, 0)),
                      pl.BlockSpec((tc, ROUTER_LANES), lambda i, ds: (i, 0)),
                      pl.BlockSpec((1, 1, d), lambda i, ds: (seg(i) * N_MOD + k_gate, 0, 0)),
                      pl.BlockSpec(memory_space=pl.ANY)],
            out_specs=pl.BlockSpec((tc, d), lambda i, ds: (i, 0)),
            scratch_shapes=[pltpu.VMEM((2, 2, tc, d), F32), pltpu.SemaphoreType.DMA((2,))]),
        out_shape=jax.ShapeDtypeStruct((t, d), F32),
        compiler_params=_params("arbitrary"),
        name="moe_combine",
    )(dest, x, routes, mods, y)


def _router(h, w_r, b_r, tm=512):
    t, d = h.shape
    e = w_r.shape[1]
    w_pad = jnp.zeros((d, ROUTER_LANES), BF16).at[:, :e].set(w_r.astype(BF16))
    b_pad = jnp.full((1, ROUTER_LANES), -jnp.inf, F32).at[0, :e].set(b_r)
    return pl.pallas_call(
        _router_kernel,
        grid=(t // tm,),
        in_specs=[pl.BlockSpec((tm, d), lambda i: (i, 0)),
                  pl.BlockSpec((d, ROUTER_LANES), lambda i: (0, 0)),
                  pl.BlockSpec((1, ROUTER_LANES), lambda i: (0, 0))],
        out_specs=pl.BlockSpec((tm, ROUTER_LANES), lambda i: (i, 0)),
        out_shape=jax.ShapeDtypeStruct((t, ROUTER_LANES), F32),
        compiler_params=_params("parallel"),
        name="router",
    )(h, w_pad, b_pad)


def kernel(x, c, ctx, c_ctx, ada_down, ada_up, ada_bias, norm1_g, norm2_g, w_in, sgu_norm_g, sgu_w,
           sgu_b, na_rpb, qk_norm_g, group_norm_g, w_out, ffn_w1, ffn_w3, ffn_w2, moe_router,
           moe_router_b, moe_w1, moe_w3, moe_w2, final_norm_g):
    n_batch, n_lat, d = x.shape
    n_ctx = ctx.shape[1]
    depth = w_in.shape[0]
    mix = w_out.shape[1]
    wa = wb = mix // 4
    wc = mix - wa - wb
    n_b_heads = wb // HEAD_DIM
    n_c_heads = wc // HEAD_DIM
    n_kv_heads = n_c_heads // 4
    wkv = n_kv_heads * HEAD_DIM
    group = n_c_heads // n_kv_heads
    n_experts = moe_router.shape[-1]
    dims = dict(n_lat=n_lat, n_batch=n_batch)
    lat_rows = n_batch * n_lat

    off_bq, off_bk, off_bv = 2 * wa, 2 * wa + wb, 2 * wa + 2 * wb
    src_cq = 2 * wa + 3 * wb
    off_ck = src_cq
    off_cv = off_ck + wkv
    off_cq = off_cv + wkv
    in_width = off_cq + wc
    tn_in = math.gcd(512, wkv)
    nb_pre, nb_kv, nb_q = src_cq // tn_in, 2 * wkv // tn_in, wc // tn_in

    def in_col_block(j):
        return jnp.where(j < nb_pre, j, jnp.where(j < nb_pre + nb_kv, j + nb_q, j - nb_kv))

    in_col_scale = jnp.ones((1, in_width), F32).at[:, off_bq:off_bk].set(QK_PRESCALE)

    xa = jnp.concatenate([x.reshape(lat_rows, d), ctx.reshape(n_batch * n_ctx, d)], axis=0)
    cond = jnp.zeros((8, d), F32).at[:n_batch].set(c).at[n_batch].set(c_ctx)
    mods_all = _ada_mods(cond, ada_down, ada_up, ada_bias)
    cos_t, sin_t = _rope_tables(n_lat, n_ctx)
    w_out_b, ffn_w2_b = w_out.astype(BF16), ffn_w2.astype(BF16)

    for l in range(depth):
        mods = mods_all[l].reshape(8 * N_MOD, 1, d)
        h = _norm_mod(xa, norm1_g[l], mods, 0, 1, out_dtype=BF16, **dims)
        p = _matmul_w32(h, w_in, l, in_col_scale, in_col_block, out_dtype=BF16, tn=tn_in)
        ya = _sgu(p, sgu_norm_g[l], sgu_w[l], sgu_b[l], group_norm_g[l, :wa], width=wa)
        bias, pair_idx = _na_bias_tables(na_rpb[l], n_lat // GRID_W)
        yb_lat = _neighbourhood_attention(
            p, bias, pair_idx, n_batch=n_batch, n_lat=n_lat, n_ctx=n_ctx, q_col=off_bq // HEAD_DIM,
            k_col=off_bk // HEAD_DIM, v_col=off_bv // HEAD_DIM, n_heads=n_b_heads)
        kb_ctx = p[lat_rows:, off_bk:off_bk + wb].reshape(n_batch, n_ctx, wb)
        vb_ctx = p[lat_rows:, off_bv:off_bv + wb].reshape(n_batch, n_ctx, wb)
        yb_ctx = _flash(p, kb_ctx, vb_ctx, n_batch=n_batch, n_q_rows=n_ctx, q_row0=lat_rows,
                        q_rows_per_batch=n_ctx, q_col=off_bq // HEAD_DIM, n_kv_heads=n_b_heads,
                        group=1, kv_row0=0, n_kv_rows=n_ctx)
        qc, kc, vc = _qk_prep(p, cos_t, sin_t, qk_norm_g[l, 0], qk_norm_g[l, 1], n_batch=n_batch,
                              n_lat=n_lat, n_ctx=n_ctx, q_off=off_cq // wc, n_q=n_c_heads,
                              k_off=off_ck // wkv, v_off=off_cv // wkv, n_kv=n_kv_heads)
        yc_lat = _flash_gqa(qc, kc, vc, n_batch=n_batch, n_q_rows=n_lat, q_row0=0,
                            q_rows_per_batch=n_lat, n_kv_heads=n_kv_heads, group=group,
                            kv_row0=0, n_kv_rows=n_lat + n_ctx)
        yc_ctx = _flash_gqa(qc, kc, vc, n_batch=n_batch, n_q_rows=n_ctx, q_row0=lat_rows,
                            q_rows_per_batch=n_ctx, n_kv_heads=n_kv_heads, group=group,
                            kv_row0=n_lat, n_kv_rows=n_ctx)
        merged = _merge(ya, yb_lat, yb_ctx, yc_lat, yc_ctx,
                        group_norm_g[l, wa:wa + wb], group_norm_g[l, wa + wb:])
        xa = _matmul_residual_wide(merged, w_out_b, l, xa, mods, 2, **dims)
        j = l // 2
        if l % 2 == 0:
            h2 = _norm_mod(xa, norm2_g[l], mods, 3, 4, out_dtype=BF16, **dims)
            hid = _matmul_swiglu_w32(h2, ffn_w1, ffn_w3, j)
            xa = _matmul_residual(hid, ffn_w2_b, j, xa, mods, 5, **dims)
        else:
            h2, routes = _norm_mod_route(xa, norm2_g[l], mods, 3, 4, moe_router[j], moe_router_b[j],
                                         **dims)
            src, dest, tile_expert, tile_valid = _route_plan(routes, n_experts, MOE_TILE)
            xs = _gather_rows(h2, src, out_dtype=BF16)
            ys = _moe_experts(xs, moe_w1, moe_w3, moe_w2, j, tile_expert, tile_valid, tm=MOE_TILE)
            xa = _moe_combine(xa, ys, routes, dest, mods, 5, **dims)

    out = _final_norm(xa, final_norm_g, lat_rows)
    return out.reshape(n_batch, n_lat, d)
```
